```python
import jax, jax.numpy as jnp
from jax import lax
import numpy as np

D_MODEL = 1024
BATCH = 32
SEQ = 256
DEPTH = 2
DEC_BATCH = 4
DEC_SEQ = 4096
PAST_LEN = 256

GRID_W = 64
N_HEADS_A = 8
N_KV_A = 2
HEAD_DIM_A = 128
ROT_FREQS = HEAD_DIM_A // 4
ROPE_THETA = 10000.0
N_HEADS_B = 16
HEAD_DIM_B = 64
WIN_H_MAX = 8
WIN_W = 16
D_FF = 2816
N_EXPERTS = 8
TOP_K = 2
D_FF_EXPERT = 2816
Q_BLOCK = 128
EPS = 1e-6
NEG_INF = -1e30
N_EVEN = (DEPTH + 1) // 2
N_ODD = DEPTH // 2
ALPHA = (2.0 * DEPTH) ** 0.25
BETA = (8.0 * DEPTH) ** -0.25

kernel_name = 'hybrid_dit_gqa_natten_moe_step'


def layer_norm(x, g, b):
    xf = x.astype(jnp.float32)
    mu = jnp.mean(xf, axis=-1, keepdims=True)
    var = jnp.mean(jnp.square(xf - mu), axis=-1, keepdims=True)
    return ((xf - mu) * lax.rsqrt(var + EPS) * g + b).astype(x.dtype)


def rms_norm(x, g):
    xf = x.astype(jnp.float32)
    return (xf * lax.rsqrt(jnp.mean(jnp.square(xf), axis=-1, keepdims=True) + EPS) * g).astype(x.dtype)


def adaln(silu_cond, w, b):
    m = (silu_cond @ w + b)[:, None, :]
    return jnp.split(m, 6, axis=-1)


def modulate(x, shift, scale):
    return x * (1.0 + scale) + shift


def post_norm(x, out, gate, g, b):
    return layer_norm(ALPHA * x + gate * out, g, b)


def axial_rope_tables(n_tokens):
    t = jnp.arange(n_tokens)
    row = (t // GRID_W).astype(jnp.float32)
    col = (t % GRID_W).astype(jnp.float32)
    freqs = ROPE_THETA ** (-jnp.arange(ROT_FREQS, dtype=jnp.float32) / ROT_FREQS)
    ang = jnp.stack([row[:, None] * freqs, col[:, None] * freqs], axis=1)
    return jnp.cos(ang), jnp.sin(ang)


def apply_axial_rope(x, cos, sin):
    B, S, H, d = x.shape
    xs = x.reshape(B, S, H, 2, 2, ROT_FREQS)
    x1, x2 = xs[..., 0, :], xs[..., 1, :]
    c = cos[None, :, None]
    s = sin[None, :, None]
    out = jnp.stack([x1 * c - x2 * s, x2 * c + x1 * s], axis=-2)
    return out.reshape(B, S, H, d).astype(x.dtype)


def blocked_gqa(q, k, v):
    B, S, Hq, d = q.shape
    Hkv = k.shape[2]
    G = Hq // Hkv
    nb = S // Q_BLOCK
    qb = jnp.moveaxis((q * (d ** -0.5)).reshape(B, nb, Q_BLOCK, Hkv, G, d), 1, 0)

    def one_block(q_blk):
        s = jnp.einsum('bqkgd,btkd->bkgqt', q_blk, k).astype(jnp.float32)
        p = jax.nn.softmax(s, axis=-1).astype(v.dtype)
        return jnp.einsum('bkgqt,btkd->bqkgd', p, v)

    o = lax.map(one_block, qb)
    return jnp.moveaxis(o, 0, 1).reshape(B, S, Hq * d)


def neighbourhood_attention(q, k, v, k_ctx, v_ctx, rpb):
    B, S, H, d = q.shape
    rows = S // GRID_W
    wh = min(WIN_H_MAX, rows)
    L = wh * GRID_W
    qg = (q * (d ** -0.5)).reshape(B, rows, GRID_W, H, d)
    kg = k.reshape(B, rows, GRID_W, H, d)
    vg = v.reshape(B, rows, GRID_W, H, d)
    col = jnp.arange(GRID_W)
    col_start = jnp.clip(col - WIN_W // 2, 0, GRID_W - WIN_W)
    in_win = (col[None, :] >= col_start[:, None]) & (col[None, :] < col_start[:, None] + WIN_W)
    mask = jnp.broadcast_to(in_win[:, None, :], (GRID_W, wh, GRID_W)).reshape(GRID_W, L)
    dc_idx = jnp.clip(col[None, :] - col[:, None], -(WIN_W - 1), WIN_W - 1) + (WIN_W - 1)

    def one_row(args):
        r, q_r = args
        start = jnp.clip(r - wh // 2, 0, rows - wh)
        k_r = lax.dynamic_slice_in_dim(kg, start, wh, axis=1).reshape(B, L, H, d)
        v_r = lax.dynamic_slice_in_dim(vg, start, wh, axis=1).reshape(B, L, H, d)
        dr_idx = start + jnp.arange(wh) - r + (WIN_H_MAX - 1)
        bias = rpb[:, dr_idx[None, :, None], dc_idx[:, None, :]].reshape(H, GRID_W, L)
        s_loc = jnp.einsum('bqhd,bkhd->bhqk', q_r, k_r).astype(jnp.float32) + bias.astype(jnp.float32)[None]
        s_loc = jnp.where(mask, s_loc, NEG_INF)
        s_ctx = jnp.einsum('bqhd,bphd->bhqp', q_r, k_ctx).astype(jnp.float32)
        p = jax.nn.softmax(jnp.concatenate([s_loc, s_ctx], axis=-1), axis=-1).astype(v.dtype)
        return (jnp.einsum('bhqk,bkhd->bqhd', p[..., :L], v_r)
                + jnp.einsum('bhqp,bphd->bqhd', p[..., L:], v_ctx))

    o = lax.map(one_row, (jnp.arange(rows), jnp.moveaxis(qg, 1, 0)))
    return jnp.moveaxis(o, 0, 1).reshape(B, S, H * d)


def gqa_qkv(h, w_qkv, q_gain, k_gain):
    B, S, _ = h.shape
    qkv = h @ w_qkv
    nq = N_HEADS_A * HEAD_DIM_A
    nk = N_KV_A * HEAD_DIM_A
    q = rms_norm(qkv[..., :nq].reshape(B, S, N_HEADS_A, HEAD_DIM_A), q_gain)
    k = rms_norm(qkv[..., nq:nq + nk].reshape(B, S, N_KV_A, HEAD_DIM_A), k_gain)
    v = qkv[..., nq + nk:].reshape(B, S, N_KV_A, HEAD_DIM_A)
    return q, k, v


def mha_qkv(h, w_qkv):
    B, S, _ = h.shape
    q, k, v = jnp.split(h @ w_qkv, 3, axis=-1)
    shp = (B, S, N_HEADS_B, HEAD_DIM_B)
    return q.reshape(shp), k.reshape(shp), v.reshape(shp)


def swiglu(x, w_gu, w_down):
    g, u = jnp.split(x @ w_gu, 2, axis=-1)
    return (jax.nn.silu(g) * u) @ w_down


def moe_swiglu(h, router_w, w_gu, w_down):
    B, S, D = h.shape
    x = h.reshape(B * S, D)
    logits = (x @ router_w).astype(jnp.float32)
    top_val, top_idx = lax.top_k(logits, TOP_K)
    top_w = jax.nn.softmax(top_val, axis=-1)
    gates = jnp.sum(jax.nn.one_hot(top_idx, N_EXPERTS, dtype=jnp.float32) * top_w[..., None], axis=1)
    y = jnp.zeros_like(x)
    for e in range(N_EXPERTS):
        y = y + gates[:, e:e + 1].astype(x.dtype) * swiglu(x, w_gu[e], w_down[e])
    return y.reshape(B, S, D)


def _normal(key, shape, scale):
    return jax.random.normal(key, shape, jnp.float32) * scale


def setup_inputs(seed: int = 0) -> dict:
    key = jax.random.key(seed)
    ks = jax.random.split(key, 32)
    D = D_MODEL
    s_d = D ** -0.5
    wa = N_HEADS_A * HEAD_DIM_A
    wb = N_HEADS_B * HEAD_DIM_B
    wqkv_a = jnp.concatenate([
        _normal(ks[14], (N_EVEN, D, wa + N_KV_A * HEAD_DIM_A), s_d),
        _normal(ks[15], (N_EVEN, D, N_KV_A * HEAD_DIM_A), s_d * BETA)], axis=-1)
    wqkv_b = jnp.concatenate([
        _normal(ks[20], (N_ODD, D, 2 * wb), s_d),
        _normal(ks[21], (N_ODD, D, wb), s_d * BETA)], axis=-1)
    return {
        'x_prompt': _normal(ks[0], (BATCH, SEQ, D), 1.0),
        'x_sample': _normal(ks[1], (DEC_BATCH, DEC_SEQ, D), 1.0),
        'cache_k_a': _normal(ks[2], (DEC_BATCH, N_EVEN, PAST_LEN, N_KV_A, HEAD_DIM_A), 1.0),
        'cache_v_a': _normal(ks[3], (DEC_BATCH, N_EVEN, PAST_LEN, N_KV_A, HEAD_DIM_A), 1.0),
        'cache_k_b': _normal(ks[4], (DEC_BATCH, N_ODD, PAST_LEN, N_HEADS_B, HEAD_DIM_B), 1.0),
        'cache_v_b': _normal(ks[5], (DEC_BATCH, N_ODD, PAST_LEN, N_HEADS_B, HEAD_DIM_B), 1.0),
        'c': _normal(ks[6], (DEC_BATCH, D), 1.0),
        'c_ctx': _normal(ks[7], (D,), 1.0),
        'ada_w': _normal(ks[8], (DEPTH, D, 6 * D), 0.5 * s_d),
        'ada_b': _normal(ks[9], (DEPTH, 6 * D), 0.02),
        'ln_attn_g': 1.0 + _normal(ks[10], (DEPTH, D), 0.02),
        'ln_attn_b': _normal(ks[11], (DEPTH, D), 0.02),
        'ln_ffn_g': 1.0 + _normal(ks[12], (DEPTH, D), 0.02),
        'ln_ffn_b': _normal(ks[13], (DEPTH, D), 0.02),
        'wqkv_a': wqkv_a,
        'qnorm_a': 1.0 + _normal(ks[16], (N_EVEN, HEAD_DIM_A), 0.02),
        'knorm_a': 1.0 + _normal(ks[17], (N_EVEN, HEAD_DIM_A), 0.02),
        'wo_a': _normal(ks[18], (N_EVEN, wa, D), wa ** -0.5 * BETA),
        'wqkv_b': wqkv_b,
        'rpb_b': _normal(ks[19], (N_ODD, N_HEADS_B, 2 * WIN_H_MAX - 1, 2 * WIN_W - 1), 0.5),
        'wo_b': _normal(ks[22], (N_ODD, wb, D), wb ** -0.5 * BETA),
        'ffn_w_gu': _normal(ks[23], (N_EVEN, D, 2 * D_FF), s_d),
        'ffn_w_down': _normal(ks[24], (N_EVEN, D_FF, D), D_FF ** -0.5 * BETA),
        'router_w': _normal(ks[25], (N_ODD, D, N_EXPERTS), s_d),
        'moe_w_gu': _normal(ks[26], (N_ODD, N_EXPERTS, D, 2 * D_FF_EXPERT), s_d),
        'moe_w_down': _normal(ks[27], (N_ODD, N_EXPERTS, D_FF_EXPERT, D), D_FF_EXPERT ** -0.5 * BETA),
    }


def reference(x_prompt, x_sample, cache_k_a, cache_v_a, cache_k_b, cache_v_b, c, c_ctx,
              ada_w, ada_b, ln_attn_g, ln_attn_b, ln_ffn_g, ln_ffn_b,
              wqkv_a, qnorm_a, knorm_a, wo_a, wqkv_b, rpb_b, wo_b,
              ffn_w_gu, ffn_w_down, router_w, moe_w_gu, moe_w_down):
    silu_ctx = jax.nn.silu(c_ctx)[None, :]
    silu_lat = jax.nn.silu(c)
    xp, xs = x_prompt, x_sample
    ka_list, va_list, kb_list, vb_list = [], [], [], []
    for i in range(DEPTH):
        j = i // 2
        mp = adaln(silu_ctx, ada_w[i], ada_b[i])
        ms = adaln(silu_lat, ada_w[i], ada_b[i])
        hp = modulate(xp, mp[0], mp[1])
        hs = modulate(xs, ms[0], ms[1])
        if i % 2 == 0:
            qp, kp, vp = gqa_qkv(hp, wqkv_a[j], qnorm_a[j], knorm_a[j])
            ka_list.append(kp)
            va_list.append(vp)
            op = blocked_gqa(qp, kp, vp) @ wo_a[j]
            qs, k_s, v_s = gqa_qkv(hs, wqkv_a[j], qnorm_a[j], knorm_a[j])
            cos, sin = axial_rope_tables(hs.shape[1])
            qs = apply_axial_rope(qs, cos, sin)
            k_s = apply_axial_rope(k_s, cos, sin)
            k_all = jnp.concatenate([k_s, cache_k_a[:, j]], axis=1)
            v_all = jnp.concatenate([v_s, cache_v_a[:, j]], axis=1)
            os_ = blocked_gqa(qs, k_all, v_all) @ wo_a[j]
        else:
            qp, kp, vp = mha_qkv(hp, wqkv_b[j])
            kb_list.append(kp)
            vb_list.append(vp)
            op = blocked_gqa(qp, kp, vp) @ wo_b[j]
            qs, k_s, v_s = mha_qkv(hs, wqkv_b[j])
            os_ = neighbourhood_attention(qs, k_s, v_s, cache_k_b[:, j], cache_v_b[:, j], rpb_b[j]) @ wo_b[j]
        xp = post_norm(xp, op, mp[2], ln_attn_g[i], ln_attn_b[i])
        xs = post_norm(xs, os_, ms[2], ln_attn_g[i], ln_attn_b[i])
        hp = modulate(xp, mp[3], mp[4])
        hs = modulate(xs, ms[3], ms[4])
        if i % 2 == 0:
            fp = swiglu(hp, ffn_w_gu[j], ffn_w_down[j])
            fs = swiglu(hs, ffn_w_gu[j], ffn_w_down[j])
        else:
            fp = moe_swiglu(hp, router_w[j], moe_w_gu[j], moe_w_down[j])
            fs = moe_swiglu(hs, router_w[j], moe_w_gu[j], moe_w_down[j])
        xp = post_norm(xp, fp, mp[5], ln_ffn_g[i], ln_ffn_b[i])
        xs = post_norm(xs, fs, ms[5], ln_ffn_g[i], ln_ffn_b[i])
    new_k_a = jnp.stack(ka_list, axis=1)
    new_v_a = jnp.stack(va_list, axis=1)
    new_k_b = jnp.stack(kb_list, axis=1)
    new_v_b = jnp.stack(vb_list, axis=1)
    return (xp, xs, new_k_a, new_v_a, new_k_b, new_v_b)
```

```python
import functools

import jax
import jax.numpy as jnp
from jax import lax
from jax.experimental import pallas as pl
from jax.experimental.pallas import tpu as pltpu

F32 = jnp.float32
BF16 = jnp.bfloat16
HIGHEST = lax.Precision.HIGHEST

D_MODEL = 1024
BATCH, SEQ = 32, 256
DEC_BATCH, DEC_SEQ = 4, 4096
PAST_LEN = 256
DEPTH = 2
GRID_W = 64
GRID_H = DEC_SEQ // GRID_W
N_HEADS_A, N_KV_A, HEAD_DIM_A = 8, 2, 128
ROT_FREQS = HEAD_DIM_A // 4
ROPE_THETA = 10000.0
N_HEADS_B, HEAD_DIM_B = 16, 64
WIN_H, WIN_W = 8, 16
D_FF = 2816
N_EXPERTS = 8
EPS = 1e-6
NEG_INF = -1e30
ALPHA = (2.0 * DEPTH) ** 0.25

N_CTX = BATCH * SEQ
N_LAT = DEC_BATCH * DEC_SEQ
N_GROUPS = 1 + DEC_BATCH
GROUP_PAD = 8

LANES = 128
SUBLANES = 8
VMEM_LIMIT = 56 * 2**20

TM = 512
TF = 1408
TMG = 512
TQ_A = 128
TN_ADA = 1536
N_PAIRS = 2 * (N_CTX + N_LAT)
NT_MOE = N_PAIRS // TMG + N_EXPERTS


class Stream:
    def __init__(self, n_rows, group_offset, rows_per_group):
        self.n = n_rows
        self.goff = group_offset
        self.rpg = rows_per_group

    def group_map(self, tm):
        tiles_per_group = self.rpg // tm
        goff = self.goff
        return lambda i, *_: (goff + i // tiles_per_group, 0, 0)


CTX = Stream(N_CTX, 0, N_CTX)
LAT = Stream(N_LAT, 1, DEC_SEQ)


def _params(*sem):
    return pltpu.CompilerParams(dimension_semantics=sem, vmem_limit_bytes=VMEM_LIMIT)


def _mod_spec(stream, tm):
    return pl.BlockSpec((None, 1, D_MODEL), stream.group_map(tm))


def _row_spec(tm, width):
    return pl.BlockSpec((tm, width), lambda i, *_: (i, 0))


def _full_spec(shape):
    nd = len(shape)
    return pl.BlockSpec(shape, lambda *_: (0,) * nd)


def _layer_norm(y, g, b):
    mu = jnp.mean(y, axis=-1, keepdims=True)
    d = y - mu
    var = jnp.mean(d * d, axis=-1, keepdims=True)
    return d * lax.rsqrt(var + EPS) * g + b


def _ada_kernel(c_ref, w_ref, b_ref, o_ref):
    c = c_ref[...]
    s = c * jax.nn.sigmoid(c)
    o_ref[...] = jnp.dot(s, w_ref[...], preferred_element_type=F32, precision=HIGHEST) + b_ref[...]


def _ada_mods(cond, ada_w, ada_b):
    n_out = 6 * D_MODEL
    out = pl.pallas_call(
        _ada_kernel,
        grid=(DEPTH, n_out // TN_ADA),
        in_specs=[
            pl.BlockSpec((GROUP_PAD, D_MODEL), lambda l, n: (0, 0)),
            pl.BlockSpec((None, D_MODEL, TN_ADA), lambda l, n: (l, 0, n)),
            pl.BlockSpec((None, 1, TN_ADA), lambda l, n: (l, 0, n)),
        ],
        out_specs=pl.BlockSpec((None, GROUP_PAD, TN_ADA), lambda l, n: (l, 0, n)),
        out_shape=jax.ShapeDtypeStruct((DEPTH, GROUP_PAD, n_out), F32),
        compiler_params=_params("arbitrary", "arbitrary"),
        name="ada_mods",
    )(cond, ada_w, ada_b.reshape(DEPTH, 1, n_out))
    out = out.reshape(DEPTH, GROUP_PAD, 6, D_MODEL).transpose(0, 2, 1, 3)
    return out[:, :, :, None, :]


def _swap_halves(t):
    lane = lax.broadcasted_iota(jnp.int32, t.shape, 1)
    fwd = pltpu.roll(t, LANES - ROT_FREQS, 1)
    bwd = pltpu.roll(t, ROT_FREQS, 1)
    return jnp.where((lane % (2 * ROT_FREQS)) < ROT_FREQS, fwd, bwd)


def _qkv_kernel(*refs, nq, nk, norm, rope, emit_f32, qscale):
    refs = list(refs)
    x_ref, sh_ref, sc_ref, w_ref = refs[:4]
    pos = 4
    if norm:
        qg_ref, kg_ref = refs[pos:pos + 2]
        pos += 2
    if rope:
        cos_ref, sin_ref = refs[pos:pos + 2]
        pos += 2
    q_ref, k_ref, v_ref = refs[pos:pos + 3]
    pos += 3
    if emit_f32:
        kf_ref, vf_ref = refs[pos:pos + 2]

    h = (x_ref[...] * (1.0 + sc_ref[...]) + sh_ref[...]).astype(BF16)
    qkv = jnp.dot(h, w_ref[...], preferred_element_type=F32)
    if norm:
        n_heads = (nq + nk) // HEAD_DIM_A
        for hd in range(n_heads):
            lo = hd * HEAD_DIM_A
            t = qkv[:, lo:lo + HEAD_DIM_A]
            ms = jnp.mean(t * t, axis=-1, keepdims=True)
            gain = qg_ref[...] if lo < nq else kg_ref[...]
            t = t * lax.rsqrt(ms + EPS) * gain
            if rope:
                t = t * cos_ref[...] + _swap_halves(t) * sin_ref[...]
            if lo < nq:
                q_ref[:, lo:lo + HEAD_DIM_A] = (t * qscale).astype(BF16)
            else:
                k_ref[:, lo - nq:lo - nq + HEAD_DIM_A] = t.astype(BF16)
                if emit_f32:
                    kf_ref[:, lo - nq:lo - nq + HEAD_DIM_A] = t
    else:
        q_ref[...] = (qkv[:, :nq] * qscale).astype(BF16)
        k = qkv[:, nq:nq + nk]
        k_ref[...] = k.astype(BF16)
        if emit_f32:
            kf_ref[...] = k
    v = qkv[:, nq + nk:]
    v_ref[...] = v.astype(BF16)
    if emit_f32:
        vf_ref[...] = v


def _qkv(stream, x, shift, scale, w, *, nq, nk, qscale, gains=None, rope_tables=None, emit_f32=False):
    n = stream.n
    nw = w.shape[1]
    norm = gains is not None
    rope = rope_tables is not None
    in_specs = [_row_spec(TM, D_MODEL), _mod_spec(stream, TM), _mod_spec(stream, TM), _full_spec((D_MODEL, nw))]
    args = [x, shift, scale, w]
    if norm:
        in_specs += [_full_spec((1, HEAD_DIM_A))] * 2
        args += list(gains)
    if rope:
        tiles_per_seq = DEC_SEQ // TM
        tbl_spec = pl.BlockSpec((TM, HEAD_DIM_A), lambda i: (i % tiles_per_seq, 0))
        in_specs += [tbl_spec, tbl_spec]
        args += list(rope_tables)
    out_specs = [_row_spec(TM, nq), _row_spec(TM, nk), _row_spec(TM, nk)]
    out_shape = [jax.ShapeDtypeStruct((n, nq), BF16), jax.ShapeDtypeStruct((n, nk), BF16),
                 jax.ShapeDtypeStruct((n, nk), BF16)]
    if emit_f32:
        out_specs += [_row_spec(TM, nk), _row_spec(TM, nk)]
        out_shape += [jax.ShapeDtypeStruct((n, nk), F32)] * 2
    return pl.pallas_call(
        functools.partial(_qkv_kernel, nq=nq, nk=nk, norm=norm, rope=rope, emit_f32=emit_f32, qscale=qscale),
        grid=(n // TM,),
        in_specs=in_specs,
        out_specs=out_specs,
        out_shape=out_shape,
        compiler_params=_params("arbitrary"),
        name="qkv_norm_rope" if norm else "qkv",
    )(*args)


def _attend(q, kvs, biases=None):
    scores = []
    for idx, (k, _) in enumerate(kvs):
        s = lax.dot_general(q, k, (((1,), (1,)), ((), ())), preferred_element_type=F32)
        if biases is not None and biases[idx] is not None:
            s = s + biases[idx]
        scores.append(s)
    m = jnp.max(scores[0], axis=-1, keepdims=True)
    for s in scores[1:]:
        m = jnp.maximum(m, jnp.max(s, axis=-1, keepdims=True))
    denom = None
    out = None
    for s, (_, v) in zip(scores, kvs):
        p = jnp.exp(s - m)
        part = jnp.sum(p, axis=-1, keepdims=True)
        pv = jnp.dot(p.astype(BF16), v, preferred_element_type=F32)
        denom = part if denom is None else denom + part
        out = pv if out is None else out + pv
    return out * (1.0 / denom)


def _gqa_kernel(*refs, n_kv, group, tq, has_cache):
    if has_cache:
        q_ref, k_ref, v_ref, kc_ref, vc_ref, o_ref = refs
    else:
        q_ref, k_ref, v_ref, o_ref = refs
    d = HEAD_DIM_A
    for g in range(n_kv):
        heads = [g * group + j for j in range(group)]
        q4 = jnp.concatenate([q_ref[:, h * d:(h + 1) * d] for h in heads], axis=0)
        kvs = [(k_ref[:, g * d:(g + 1) * d], v_ref[:, g * d:(g + 1) * d])]
        if has_cache:
            kvs.append((kc_ref[:, g * d:(g + 1) * d], vc_ref[:, g * d:(g + 1) * d]))
        o = _attend(q4, kvs)
        for j, h in enumerate(heads):
            o_ref[:, h * d:(h + 1) * d] = o[j * tq:(j + 1) * tq].astype(o_ref.dtype)


def _gqa_ctx(q, k, v):
    nq, nk = N_HEADS_A * HEAD_DIM_A, N_KV_A * HEAD_DIM_A
    return pl.pallas_call(
        functools.partial(_gqa_kernel, n_kv=N_KV_A, group=N_HEADS_A // N_KV_A, tq=SEQ, has_cache=False),
        grid=(BATCH,),
        in_specs=[_row_spec(SEQ, nq), _row_spec(SEQ, nk), _row_spec(SEQ, nk)],
        out_specs=_row_spec(SEQ, nq),
        out_shape=jax.ShapeDtypeStruct((N_CTX, nq), BF16),
        compiler_params=_params("arbitrary"),
        name="gqa_ctx",
    )(q, k, v)


def _gqa_lat(q, k, v, kc, vc):
    nq, nk = N_HEADS_A * HEAD_DIM_A, N_KV_A * HEAD_DIM_A
    tiles = DEC_SEQ // TQ_A
    kv_spec = pl.BlockSpec((DEC_SEQ, nk), lambda b, t: (b, 0))
    cache_spec = pl.BlockSpec((None, PAST_LEN, nk), lambda b, t: (b, 0, 0))
    q_spec = pl.BlockSpec((TQ_A, nq), lambda b, t: (b * tiles + t, 0))
    return pl.pallas_call(
        functools.partial(_gqa_kernel, n_kv=N_KV_A, group=N_HEADS_A // N_KV_A, tq=TQ_A, has_cache=True),
        grid=(DEC_BATCH, tiles),
        in_specs=[q_spec, kv_spec, kv_spec, cache_spec, cache_spec],
        out_specs=q_spec,
        out_shape=jax.ShapeDtypeStruct((N_LAT, nq), BF16),
        compiler_params=_params("arbitrary", "arbitrary"),
        name="gqa_lat",
    )(q, k, v, kc, vc)


def _split_pair(qp):
    lo = lax.broadcasted_iota(jnp.int32, qp.shape, 1) < HEAD_DIM_B
    zero = jnp.zeros_like(qp)
    return jnp.concatenate([jnp.where(lo, qp, zero), jnp.where(lo, zero, qp)], axis=0)


def _merge_pair(o2, rows):
    lo = lax.broadcasted_iota(jnp.int32, (rows, LANES), 1) < HEAD_DIM_B
    return jnp.where(lo, o2[:rows], o2[rows:])


def _mha_ctx_kernel(q_ref, k_ref, v_ref, o_ref):
    for p in range(N_HEADS_B // 2):
        sl = slice(p * LANES, (p + 1) * LANES)
        o2 = _attend(_split_pair(q_ref[:, sl]), [(k_ref[:, sl], v_ref[:, sl])])
        o_ref[:, sl] = _merge_pair(o2, SEQ).astype(o_ref.dtype)


def _mha_ctx(q, k, v):
    return pl.pallas_call(
        _mha_ctx_kernel,
        grid=(BATCH,),
        in_specs=[_row_spec(SEQ, D_MODEL)] * 3,
        out_specs=_row_spec(SEQ, D_MODEL),
        out_shape=jax.ShapeDtypeStruct((N_CTX, D_MODEL), BF16),
        compiler_params=_params("arbitrary"),
        name="mha_ctx",
    )(q, k, v)


NAT_HALF = D_MODEL // 2
NAT_PAIRS = NAT_HALF // LANES
N_DR = 2 * WIN_H - 2


def _natten_kernel(q_ref, k_ref, v_ref, kc_ref, vc_ref, tbl_ref, o_ref):
    r = pl.program_id(2)
    start = jnp.clip(r - WIN_H // 2, 0, GRID_H - WIN_H)
    row0 = pl.multiple_of(start * GRID_W, GRID_W)
    dr0 = start - r + (WIN_H - 1)
    n_loc = WIN_H * GRID_W
    for p in range(NAT_PAIRS):
        sl = slice(p * LANES, (p + 1) * LANES)
        q2 = _split_pair(q_ref[:, sl])
        bias = jnp.concatenate([tbl_ref[dr0 + 2 * w, p] for w in range(WIN_H // 2)], axis=1)
        kvs = [(k_ref[pl.ds(row0, n_loc), sl], v_ref[pl.ds(row0, n_loc), sl]), (kc_ref[:, sl], vc_ref[:, sl])]
        o2 = _attend(q2, kvs, biases=[bias, None])
        o_ref[:, sl] = _merge_pair(o2, GRID_W).astype(o_ref.dtype)


def _natten(q, k, v, kc, vc, tbl):
    n_half = D_MODEL // NAT_HALF
    q_spec = pl.BlockSpec((GRID_W, NAT_HALF), lambda b, hh, r: (b * GRID_H + r, hh))
    kv_spec = pl.BlockSpec((DEC_SEQ, NAT_HALF), lambda b, hh, r: (b, hh))
    cache_spec = pl.BlockSpec((None, PAST_LEN, NAT_HALF), lambda b, hh, r: (b, 0, hh))
    tbl_spec = pl.BlockSpec((N_DR, NAT_PAIRS, LANES, LANES), lambda b, hh, r: (0, hh, 0, 0))
    return pl.pallas_call(
        _natten_kernel,
        grid=(DEC_BATCH, n_half, GRID_H),
        in_specs=[q_spec, kv_spec, kv_spec, cache_spec, cache_spec, tbl_spec],
        out_specs=q_spec,
        out_shape=jax.ShapeDtypeStruct((N_LAT, D_MODEL), BF16),
        compiler_params=_params("arbitrary", "arbitrary", "arbitrary"),
        name="natten",
    )(q, k, v, kc, vc, tbl)


def _bias_table_kernel(rpb_ref, o_ref):
    dr = pl.program_id(0)
    pr = pl.program_id(1)
    row = lax.broadcasted_iota(jnp.int32, (LANES, LANES), 0)
    col = lax.broadcasted_iota(jnp.int32, (LANES, LANES), 1)
    sub = row >= GRID_W
    half = col >= GRID_W
    qc = row % GRID_W
    kc = col % GRID_W
    dc = jnp.clip(kc - qc, -(WIN_W - 1), WIN_W - 1) + (WIN_W - 1)
    col_start = jnp.clip(qc - WIN_W // 2, 0, GRID_W - WIN_W)
    in_win = (kc >= col_start) & (kc < col_start + WIN_W)
    n_dc = 2 * WIN_W - 1
    n_drow = 2 * WIN_H - 1

    def base(s, hf):
        return ((2 * pr + s) * n_drow + dr + hf) * n_dc

    b00, b01, b10, b11 = base(0, 0), base(0, 1), base(1, 0), base(1, 1)

    def body(j, acc):
        val = jnp.where(sub, jnp.where(half, rpb_ref[b11 + j], rpb_ref[b10 + j]),
                        jnp.where(half, rpb_ref[b01 + j], rpb_ref[b00 + j]))
        return jnp.where(dc == j, val, acc)

    acc = lax.fori_loop(0, n_dc, body, jnp.zeros((LANES, LANES), F32))
    o_ref[...] = jnp.where(in_win, acc, NEG_INF)


def _bias_table(rpb):
    return pl.pallas_call(
        _bias_table_kernel,
        grid=(N_DR, N_HEADS_B // 2),
        in_specs=[pl.BlockSpec(memory_space=pltpu.SMEM)],
        out_specs=pl.BlockSpec((None, None, LANES, LANES), lambda d, p: (d, p, 0, 0)),
        out_shape=jax.ShapeDtypeStruct((N_DR, N_HEADS_B // 2, LANES, LANES), F32),
        compiler_params=_params("arbitrary", "arbitrary"),
        name="natten_bias_table",
    )(rpb.reshape(-1))


def _wo_ln_kernel(a_ref, w_ref, x_ref, gate_ref, g_ref, b_ref, o_ref):
    o = jnp.dot(a_ref[...], w_ref[...], preferred_element_type=F32)
    y = ALPHA * x_ref[...] + gate_ref[...] * o
    o_ref[...] = _layer_norm(y, g_ref[...], b_ref[...])


def _wo_ln(stream, a, w, x, gate, ln_g, ln_b):
    vec = _full_spec((1, D_MODEL))
    return pl.pallas_call(
        _wo_ln_kernel,
        grid=(stream.n // TM,),
        in_specs=[_row_spec(TM, a.shape[1]), _full_spec(w.shape), _row_spec(TM, D_MODEL),
                  _mod_spec(stream, TM), vec, vec],
        out_specs=_row_spec(TM, D_MODEL),
        out_shape=jax.ShapeDtypeStruct((stream.n, D_MODEL), F32),
        compiler_params=_params("arbitrary"),
        name="wo_postnorm",
    )(a, w, x, gate, ln_g, ln_b)


def _swiglu_step(h_scr, acc_scr, wg_ref, wu_ref, wd_ref):
    h = h_scr[...]
    g = jnp.dot(h, wg_ref[...], preferred_element_type=F32)
    u = jnp.dot(h, wu_ref[...], preferred_element_type=F32)
    a = (g * jax.nn.sigmoid(g) * u).astype(BF16)
    acc_scr[...] += jnp.dot(a, wd_ref[...], preferred_element_type=F32)


def _ffn_dense_kernel(x_ref, sh_ref, sc_ref, wg_ref, wu_ref, wd_ref, gate_ref, g_ref, b_ref, o_ref,
                      h_scr, acc_scr, *, nj):
    j = pl.program_id(1)

    @pl.when(j == 0)
    def _():
        h_scr[...] = (x_ref[...] * (1.0 + sc_ref[...]) + sh_ref[...]).astype(BF16)
        acc_scr[...] = jnp.zeros_like(acc_scr)

    _swiglu_step(h_scr, acc_scr, wg_ref, wu_ref, wd_ref)

    @pl.when(j == nj - 1)
    def _():
        y = ALPHA * x_ref[...] + gate_ref[...] * acc_scr[...]
        o_ref[...] = _layer_norm(y, g_ref[...], b_ref[...])


def _ffn_dense(stream, x, shift, scale, w_gu, w_down, gate, ln_g, ln_b):
    nj = D_FF // TF
    vec = pl.BlockSpec((1, D_MODEL), lambda i, j: (0, 0))
    mod = pl.BlockSpec((None, 1, D_MODEL), stream.group_map(TM))
    row = pl.BlockSpec((TM, D_MODEL), lambda i, j: (i, 0))
    return pl.pallas_call(
        functools.partial(_ffn_dense_kernel, nj=nj),
        grid=(stream.n // TM, nj),
        in_specs=[row, mod, mod,
                  pl.BlockSpec((D_MODEL, TF), lambda i, j: (0, j)),
                  pl.BlockSpec((D_MODEL, TF), lambda i, j: (0, nj + j)),
                  pl.BlockSpec((TF, D_MODEL), lambda i, j: (j, 0)),
                  mod, vec, vec],
        out_specs=row,
        out_shape=jax.ShapeDtypeStruct((stream.n, D_MODEL), F32),
        scratch_shapes=[pltpu.VMEM((TM, D_MODEL), BF16), pltpu.VMEM((TM, D_MODEL), F32)],
        compiler_params=_params("arbitrary", "arbitrary"),
        name="ffn_dense",
    )(x, shift, scale, w_gu, w_gu, w_down, gate, ln_g, ln_b)


def _ffn_grouped_kernel(te_ref, tv_ref, x_ref, wg_ref, wu_ref, wd_ref, o_ref, h_scr, acc_scr, *, nj):
    i = pl.program_id(0)
    j = pl.program_id(1)

    @pl.when(tv_ref[i] != 0)
    def _():
        @pl.when(j == 0)
        def _():
            h_scr[...] = x_ref[...].astype(BF16)
            acc_scr[...] = jnp.zeros_like(acc_scr)

        _swiglu_step(h_scr, acc_scr, wg_ref, wu_ref, wd_ref)

        @pl.when(j == nj - 1)
        def _():
            o_ref[...] = acc_scr[...]

    @pl.when((tv_ref[i] == 0) & (j == nj - 1))
    def _():
        o_ref[...] = jnp.zeros_like(o_ref)


def _ffn_grouped(tile_expert, tile_valid, xs, w_gu, w_down):
    nj = D_FF // TF
    row = pl.BlockSpec((TMG, D_MODEL), lambda i, j, te, tv: (i, 0))
    grid_spec = pltpu.PrefetchScalarGridSpec(
        num_scalar_prefetch=2,
        grid=(NT_MOE, nj),
        in_specs=[row,
                  pl.BlockSpec((None, D_MODEL, TF), lambda i, j, te, tv: (te[i], 0, j)),
                  pl.BlockSpec((None, D_MODEL, TF), lambda i, j, te, tv: (te[i], 0, nj + j)),
                  pl.BlockSpec((None, TF, D_MODEL), lambda i, j, te, tv: (te[i], j, 0))],
        out_specs=row,
        scratch_shapes=[pltpu.VMEM((TMG, D_MODEL), BF16), pltpu.VMEM((TMG, D_MODEL), F32)],
    )
    return pl.pallas_call(
        functools.partial(_ffn_grouped_kernel, nj=nj),
        grid_spec=grid_spec,
        out_shape=jax.ShapeDtypeStruct((NT_MOE * TMG, D_MODEL), F32),
        compiler_params=_params("arbitrary", "arbitrary"),
        name="ffn_grouped",
    )(tile_expert, tile_valid, xs, w_gu, w_gu, w_down)


META_E0, META_E1, META_W0, META_W1, META_R0, META_R1 = range(6)


def _router_kernel(x_ref, sh_ref, sc_ref, rw_ref, cnt_in_ref, h_ref, meta_ref, cnt_ref, carry_scr):
    i = pl.program_id(0)

    @pl.when(i == 0)
    def _():
        carry_scr[...] = cnt_in_ref[...]

    h = x_ref[...] * (1.0 + sc_ref[...]) + sh_ref[...]
    h_ref[...] = h
    logits = jnp.dot(h, rw_ref[...], preferred_element_type=F32, precision=HIGHEST)
    lane = lax.broadcasted_iota(jnp.int32, logits.shape, 1).astype(F32)
    lg = jnp.where(lane < N_EXPERTS, logits, -jnp.inf)
    m1 = jnp.max(lg, axis=-1, keepdims=True)
    i1 = jnp.min(jnp.where(lg == m1, lane, float(LANES)), axis=-1, keepdims=True)
    lg2 = jnp.where(lane == i1, -jnp.inf, lg)
    m2 = jnp.max(lg2, axis=-1, keepdims=True)
    i2 = jnp.min(jnp.where(lg2 == m2, lane, float(LANES)), axis=-1, keepdims=True)
    e = jnp.exp(m2 - m1)
    w1 = 1.0 / (1.0 + e)
    w2 = e / (1.0 + e)

    sel1 = lane == i1
    sel2 = lane == i2
    onehot = jnp.where(sel1 | sel2, 1.0, 0.0)
    rr = lax.broadcasted_iota(jnp.int32, (TM, TM), 0)
    cc = lax.broadcasted_iota(jnp.int32, (TM, TM), 1)
    lower = jnp.where(cc < rr, 1.0, 0.0).astype(BF16)
    before = jnp.dot(lower, onehot.astype(BF16), preferred_element_type=F32) + carry_scr[0:1, :]
    r1 = jnp.sum(jnp.where(sel1, before, 0.0), axis=-1, keepdims=True)
    r2 = jnp.sum(jnp.where(sel2, before, 0.0), axis=-1, keepdims=True)
    carry_scr[...] = carry_scr[...] + jnp.sum(onehot, axis=0, keepdims=True)
    cnt_ref[...] = carry_scr[...]

    cols = [i1, i2, w1, w2, r1, r2]
    meta = jnp.zeros(logits.shape, F32)
    for c, val in enumerate(cols):
        meta = jnp.where(lane == c, val, meta)
    meta_ref[...] = meta


def _router(stream, x, shift, scale, rw_pad, cnt_in):
    cnt_spec = _full_spec((SUBLANES, LANES))
    return pl.pallas_call(
        _router_kernel,
        grid=(stream.n // TM,),
        in_specs=[_row_spec(TM, D_MODEL), _mod_spec(stream, TM), _mod_spec(stream, TM),
                  _full_spec((D_MODEL, LANES)), cnt_spec],
        out_specs=[_row_spec(TM, D_MODEL), _row_spec(TM, LANES), cnt_spec],
        out_shape=[jax.ShapeDtypeStruct((stream.n, D_MODEL), F32), jax.ShapeDtypeStruct((stream.n, LANES), F32),
                   jax.ShapeDtypeStruct((SUBLANES, LANES), F32)],
        scratch_shapes=[pltpu.VMEM((SUBLANES, LANES), F32)],
        compiler_params=_params("arbitrary"),
        name="moe_router",
    )(x, shift, scale, rw_pad, cnt_in)


def _row_copy(src_ref, src_row, dst_ref, dst_row, sem):
    return pltpu.make_async_copy(src_ref.at[pl.ds(src_row, 1)], dst_ref.at[pl.ds(dst_row, 1)], sem)


def _scatter_kernel(d0_ref, d1_ref, h_ref, buf_in_ref, buf_ref, sem):
    del buf_in_ref
    base = pl.program_id(0) * TM

    def issue(r, c):
        _row_copy(h_ref, r, buf_ref, d0_ref[base + r], sem).start()
        _row_copy(h_ref, r, buf_ref, d1_ref[base + r], sem).start()
        return c

    lax.fori_loop(0, TM, issue, 0)

    def drain(r, c):
        _row_copy(h_ref, 0, buf_ref, 0, sem).wait()
        _row_copy(h_ref, 0, buf_ref, 0, sem).wait()
        return c

    lax.fori_loop(0, TM, drain, 0)


def _scatter(stream, d0, d1, h, buf):
    grid_spec = pltpu.PrefetchScalarGridSpec(
        num_scalar_prefetch=2,
        grid=(stream.n // TM,),
        in_specs=[pl.BlockSpec((TM, D_MODEL), lambda i, a, b: (i, 0)), pl.BlockSpec(memory_space=pl.ANY)],
        out_specs=pl.BlockSpec(memory_space=pl.ANY),
        scratch_shapes=[pltpu.SemaphoreType.DMA],
    )
    return pl.pallas_call(
        _scatter_kernel,
        grid_spec=grid_spec,
        out_shape=jax.ShapeDtypeStruct(buf.shape, buf.dtype),
        input_output_aliases={3: 0},
        compiler_params=_params("arbitrary"),
        name="moe_scatter",
    )(d0, d1, h, buf)


def _combine_kernel(d0_ref, d1_ref, ys_ref, meta_ref, x_ref, gate_ref, g_ref, b_ref, o_ref, rows_scr, sem):
    base = pl.program_id(0) * TM

    def issue(r, c):
        _row_copy(ys_ref, d0_ref[base + r], rows_scr.at[0], r, sem).start()
        _row_copy(ys_ref, d1_ref[base + r], rows_scr.at[1], r, sem).start()
        return c

    lax.fori_loop(0, TM, issue, 0)

    def drain(r, c):
        _row_copy(ys_ref, 0, rows_scr.at[0], 0, sem).wait()
        _row_copy(ys_ref, 0, rows_scr.at[1], 0, sem).wait()
        return c

    lax.fori_loop(0, TM, drain, 0)

    meta = meta_ref[...]
    w0 = meta[:, META_W0:META_W0 + 1]
    w1 = meta[:, META_W1:META_W1 + 1]
    f = w0 * rows_scr[0] + w1 * rows_scr[1]
    y = ALPHA * x_ref[...] + gate_ref[...] * f
    o_ref[...] = _layer_norm(y, g_ref[...], b_ref[...])


def _combine(stream, d0, d1, ys, meta, x, gate, ln_g, ln_b):
    vec = pl.BlockSpec((1, D_MODEL), lambda i, a, b: (0, 0))
    grid_spec = pltpu.PrefetchScalarGridSpec(
        num_scalar_prefetch=2,
        grid=(stream.n // TM,),
        in_specs=[pl.BlockSpec(memory_space=pl.ANY), _row_spec(TM, LANES), _row_spec(TM, D_MODEL),
                  _mod_spec(stream, TM), vec, vec],
        out_specs=_row_spec(TM, D_MODEL),
        scratch_shapes=[pltpu.VMEM((2, TM, D_MODEL), F32), pltpu.SemaphoreType.DMA],
    )
    return pl.pallas_call(
        _combine_kernel,
        grid_spec=grid_spec,
        out_shape=jax.ShapeDtypeStruct((stream.n, D_MODEL), F32),
        compiler_params=_params("arbitrary"),
        name="moe_combine",
    )(d0, d1, ys, meta, x, gate, ln_g, ln_b)


def _rope_tables():
    t = jnp.arange(DEC_SEQ)
    row = (t // GRID_W).astype(F32)
    col = (t % GRID_W).astype(F32)
    freqs = ROPE_THETA ** (-jnp.arange(ROT_FREQS, dtype=F32) / ROT_FREQS)
    ar = row[:, None] * freqs
    ac = col[:, None] * freqs
    cos = jnp.concatenate([jnp.cos(ar), jnp.cos(ar), jnp.cos(ac), jnp.cos(ac)], axis=1)
    sin = jnp.concatenate([-jnp.sin(ar), jnp.sin(ar), -jnp.sin(ac), jnp.sin(ac)], axis=1)
    return cos, sin


def _routing_plan(metas, counts):
    cnt = counts[0, :N_EXPERTS].astype(jnp.int32)
    tiles_e = (cnt + TMG - 1) // TMG
    tile_end = jnp.cumsum(tiles_e)
    tile_start = tile_end - tiles_e
    offs = tile_start * TMG
    dests = []
    for meta in metas:
        pair = []
        for ecol, rcol in ((META_E0, META_R0), (META_E1, META_R1)):
            e = meta[:, ecol].astype(jnp.int32)
            d = meta[:, rcol].astype(jnp.int32)
            for k in range(N_EXPERTS):
                d = d + jnp.where(e == k, offs[k], 0)
            pair.append(d)
        dests.append(tuple(pair))
    tid = jnp.arange(NT_MOE, dtype=jnp.int32)
    te = jnp.minimum(jnp.sum((tid[:, None] >= tile_end[None, :]).astype(jnp.int32), axis=1), N_EXPERTS - 1)
    total = tile_end[-1]
    valid = tid < total
    te_last = jnp.max(jnp.where(valid, te, 0))
    te = jnp.where(valid, te, te_last)
    return dests, te, valid.astype(jnp.int32)


def kernel(x_prompt, x_sample, cache_k_a, cache_v_a, cache_k_b, cache_v_b, c, c_ctx, ada_w, ada_b, ln_attn_g, ln_attn_b, ln_ffn_g, ln_ffn_b, wqkv_a, qnorm_a, knorm_a, wo_a, wqkv_b, rpb_b, wo_b, ffn_w_gu, ffn_w_down, router_w, moe_w_gu, moe_w_down):
    streams = (CTX, LAT)
    xs = [x_prompt.reshape(N_CTX, D_MODEL), x_sample.reshape(N_LAT, D_MODEL)]

    cond = jnp.zeros((GROUP_PAD, D_MODEL), F32).at[0].set(c_ctx).at[1:N_GROUPS].set(c)
    mods = _ada_mods(cond, ada_w, ada_b)
    vec = lambda a, l: a[l].reshape(1, D_MODEL)

    m = mods[0]
    w_qkv = wqkv_a[0].astype(BF16)
    w_o = wo_a[0].astype(BF16)
    w_gu = ffn_w_gu[0].astype(BF16)
    w_dn = ffn_w_down[0].astype(BF16)
    gains = (qnorm_a[0].reshape(1, HEAD_DIM_A), knorm_a[0].reshape(1, HEAD_DIM_A))
    nq, nk = N_HEADS_A * HEAD_DIM_A, N_KV_A * HEAD_DIM_A
    qscale = HEAD_DIM_A ** -0.5

    qp, kp, vp, kp32, vp32 = _qkv(CTX, xs[0], m[0], m[1], w_qkv, nq=nq, nk=nk, qscale=qscale, gains=gains,
                                  emit_f32=True)
    new_k_a = kp32.reshape(BATCH, 1, SEQ, N_KV_A, HEAD_DIM_A)
    new_v_a = vp32.reshape(BATCH, 1, SEQ, N_KV_A, HEAD_DIM_A)
    ql, kl, vl = _qkv(LAT, xs[1], m[0], m[1], w_qkv, nq=nq, nk=nk, qscale=qscale, gains=gains,
                      rope_tables=_rope_tables())
    attn = [_gqa_ctx(qp, kp, vp),
            _gqa_lat(ql, kl, vl, cache_k_a[:, 0].reshape(DEC_BATCH, PAST_LEN, nk).astype(BF16),
                     cache_v_a[:, 0].reshape(DEC_BATCH, PAST_LEN, nk).astype(BF16))]
    xs = [_wo_ln(s, a, w_o, x, m[2], vec(ln_attn_g, 0), vec(ln_attn_b, 0)) for s, a, x in zip(streams, attn, xs)]
    xs = [_ffn_dense(s, x, m[3], m[4], w_gu, w_dn, m[5], vec(ln_ffn_g, 0), vec(ln_ffn_b, 0))
          for s, x in zip(streams, xs)]

    m = mods[1]
    w_qkv = wqkv_b[0].astype(BF16)
    w_o = wo_b[0].astype(BF16)
    qscale = HEAD_DIM_B ** -0.5
    qp, kp, vp, kp32, vp32 = _qkv(CTX, xs[0], m[0], m[1], w_qkv, nq=D_MODEL, nk=D_MODEL, qscale=qscale,
                                  emit_f32=True)
    new_k_b = kp32.reshape(BATCH, 1, SEQ, N_HEADS_B, HEAD_DIM_B)
    new_v_b = vp32.reshape(BATCH, 1, SEQ, N_HEADS_B, HEAD_DIM_B)
    ql, kl, vl = _qkv(LAT, xs[1], m[0], m[1], w_qkv, nq=D_MODEL, nk=D_MODEL, qscale=qscale)
    attn = [_mha_ctx(qp, kp, vp),
            _natten(ql, kl, vl, cache_k_b[:, 0].reshape(DEC_BATCH, PAST_LEN, D_MODEL).astype(BF16),
                    cache_v_b[:, 0].reshape(DEC_BATCH, PAST_LEN, D_MODEL).astype(BF16), _bias_table(rpb_b[0]))]
    xs = [_wo_ln(s, a, w_o, x, m[2], vec(ln_attn_g, 1), vec(ln_attn_b, 1)) for s, a, x in zip(streams, attn, xs)]

    rw_pad = jnp.zeros((D_MODEL, LANES), F32).at[:, :N_EXPERTS].set(router_w[0])
    counts = jnp.zeros((SUBLANES, LANES), F32)
    hs, metas = [], []
    for s, x in zip(streams, xs):
        h, meta, counts = _router(s, x, m[3], m[4], rw_pad, counts)
        hs.append(h)
        metas.append(meta)
    dests, tile_expert, tile_valid = _routing_plan(metas, counts)
    sorted_rows = jnp.zeros((NT_MOE * TMG, D_MODEL), F32)
    for s, (d0, d1), h in zip(streams, dests, hs):
        sorted_rows = _scatter(s, d0, d1, h, sorted_rows)
    ys = _ffn_grouped(tile_expert, tile_valid, sorted_rows, moe_w_gu[0].astype(BF16), moe_w_down[0].astype(BF16))
    outs = [_combine(s, d0, d1, ys, meta, x, m[5], vec(ln_ffn_g, 1), vec(ln_ffn_b, 1))
            for s, (d0, d1), meta, x in zip(streams, dests, metas, xs)]

    y_prompt = outs[0].reshape(BATCH, SEQ, D_MODEL)
    y_sample = outs[1].reshape(DEC_BATCH, DEC_SEQ, D_MODEL)
    return (y_prompt, y_sample, new_k_a, new_v_a, new_k_b, new_v_b)
```

```python
import functools

import jax
import jax.numpy as jnp
from jax import lax
from jax.experimental import pallas as pl
from jax.experimental.pallas import tpu as pltpu

F32 = jnp.float32
BF16 = jnp.bfloat16
HIGHEST = lax.Precision.HIGHEST

D_MODEL = 1024
BATCH, SEQ = 32, 256
DEC_BATCH, DEC_SEQ = 4, 4096
PAST_LEN = 256
DEPTH = 2
GRID_W = 64
GRID_H = DEC_SEQ // GRID_W
N_HEADS_A, N_KV_A, HEAD_DIM_A = 8, 2, 128
ROT_FREQS = HEAD_DIM_A // 4
ROPE_THETA = 10000.0
N_HEADS_B, HEAD_DIM_B = 16, 64
WIN_H, WIN_W = 8, 16
D_FF = 2816
N_EXPERTS = 8
EPS = 1e-6
NEG_INF = -1e30
ALPHA = (2.0 * DEPTH) ** 0.25

N_CTX = BATCH * SEQ
N_LAT = DEC_BATCH * DEC_SEQ
N_GROUPS = 1 + DEC_BATCH
GROUP_PAD = 8

LANES = 128
SUBLANES = 8
VMEM_LIMIT = 56 * 2**20

TM = 512
TF = 1408
TMG = 512
TQ_A = 128
UNIT_A = 4
TN_ADA = 1536
ROW_UNROLL = 8
N_PAIRS = 2 * (N_CTX + N_LAT)
NT_MOE = N_PAIRS // TMG + N_EXPERTS


class Stream:
    def __init__(self, n_rows, group_offset, rows_per_group):
        self.n = n_rows
        self.goff = group_offset
        self.rpg = rows_per_group

    def group_map(self, tm):
        tiles_per_group = self.rpg // tm
        goff = self.goff
        return lambda i, *_: (goff + i // tiles_per_group, 0, 0)


CTX = Stream(N_CTX, 0, N_CTX)
LAT = Stream(N_LAT, 1, DEC_SEQ)


def _params(*sem):
    return pltpu.CompilerParams(dimension_semantics=sem, vmem_limit_bytes=VMEM_LIMIT)


def _mod_spec(stream, tm):
    return pl.BlockSpec((None, 1, D_MODEL), stream.group_map(tm))


def _row_spec(tm, width):
    return pl.BlockSpec((tm, width), lambda i, *_: (i, 0))


def _full_spec(shape):
    nd = len(shape)
    return pl.BlockSpec(shape, lambda *_: (0,) * nd)


def _layer_norm(y, g, b):
    mu = jnp.mean(y, axis=-1, keepdims=True)
    d = y - mu
    var = jnp.mean(d * d, axis=-1, keepdims=True)
    return d * lax.rsqrt(var + EPS) * g + b


def _ada_kernel(c_ref, w_ref, b_ref, o_ref):
    c = c_ref[...]
    s = c * jax.nn.sigmoid(c)
    o_ref[...] = jnp.dot(s, w_ref[...], preferred_element_type=F32, precision=HIGHEST) + b_ref[...]


def _ada_mods(cond, ada_w, ada_b):
    n_out = 6 * D_MODEL
    out = pl.pallas_call(
        _ada_kernel,
        grid=(DEPTH, n_out // TN_ADA),
        in_specs=[
            pl.BlockSpec((GROUP_PAD, D_MODEL), lambda l, n: (0, 0)),
            pl.BlockSpec((None, D_MODEL, TN_ADA), lambda l, n: (l, 0, n)),
            pl.BlockSpec((None, 1, TN_ADA), lambda l, n: (l, 0, n)),
        ],
        out_specs=pl.BlockSpec((None, GROUP_PAD, TN_ADA), lambda l, n: (l, 0, n)),
        out_shape=jax.ShapeDtypeStruct((DEPTH, GROUP_PAD, n_out), F32),
        compiler_params=_params("arbitrary", "arbitrary"),
        name="ada_mods",
    )(cond, ada_w, ada_b.reshape(DEPTH, 1, n_out))
    out = out.reshape(DEPTH, GROUP_PAD, 6, D_MODEL).transpose(0, 2, 1, 3)
    return out[:, :, :, None, :]


def _swap_halves(t):
    lane = lax.broadcasted_iota(jnp.int32, t.shape, 1)
    fwd = pltpu.roll(t, LANES - ROT_FREQS, 1)
    bwd = pltpu.roll(t, ROT_FREQS, 1)
    return jnp.where((lane % (2 * ROT_FREQS)) < ROT_FREQS, fwd, bwd)


def _qkv_kernel(*refs, nq, nk, norm, rope, emit_f32, qscale):
    refs = list(refs)
    x_ref, sh_ref, sc_ref, w_ref = refs[:4]
    pos = 4
    if norm:
        qg_ref, kg_ref = refs[pos:pos + 2]
        pos += 2
    if rope:
        cos_ref, sin_ref = refs[pos:pos + 2]
        pos += 2
    q_ref, k_ref, v_ref = refs[pos:pos + 3]
    pos += 3
    if emit_f32:
        kf_ref, vf_ref = refs[pos:pos + 2]

    h = (x_ref[...] * (1.0 + sc_ref[...]) + sh_ref[...]).astype(BF16)
    qkv = jnp.dot(h, w_ref[...], preferred_element_type=F32)
    if norm:
        n_heads = (nq + nk) // HEAD_DIM_A
        for hd in range(n_heads):
            lo = hd * HEAD_DIM_A
            t = qkv[:, lo:lo + HEAD_DIM_A]
            ms = jnp.mean(t * t, axis=-1, keepdims=True)
            gain = qg_ref[...] if lo < nq else kg_ref[...]
            t = t * lax.rsqrt(ms + EPS) * gain
            if rope:
                t = t * cos_ref[...] + _swap_halves(t) * sin_ref[...]
            if lo < nq:
                q_ref[:, lo:lo + HEAD_DIM_A] = (t * qscale).astype(BF16)
            else:
                k_ref[:, lo - nq:lo - nq + HEAD_DIM_A] = t.astype(BF16)
                if emit_f32:
                    kf_ref[:, lo - nq:lo - nq + HEAD_DIM_A] = t
    else:
        q_ref[...] = (qkv[:, :nq] * qscale).astype(BF16)
        k = qkv[:, nq:nq + nk]
        k_ref[...] = k.astype(BF16)
        if emit_f32:
            kf_ref[...] = k
    v = qkv[:, nq + nk:]
    v_ref[...] = v.astype(BF16)
    if emit_f32:
        vf_ref[...] = v


def _qkv(stream, x, shift, scale, w, *, nq, nk, qscale, gains=None, rope_tables=None, emit_f32=False):
    n = stream.n
    nw = w.shape[1]
    norm = gains is not None
    rope = rope_tables is not None
    in_specs = [_row_spec(TM, D_MODEL), _mod_spec(stream, TM), _mod_spec(stream, TM), _full_spec((D_MODEL, nw))]
    args = [x, shift, scale, w]
    if norm:
        in_specs += [_full_spec((1, HEAD_DIM_A))] * 2
        args += list(gains)
    if rope:
        tiles_per_seq = DEC_SEQ // TM
        tbl_spec = pl.BlockSpec((TM, HEAD_DIM_A), lambda i: (i % tiles_per_seq, 0))
        in_specs += [tbl_spec, tbl_spec]
        args += list(rope_tables)
    out_specs = [_row_spec(TM, nq), _row_spec(TM, nk), _row_spec(TM, nk)]
    out_shape = [jax.ShapeDtypeStruct((n, nq), BF16), jax.ShapeDtypeStruct((n, nk), BF16),
                 jax.ShapeDtypeStruct((n, nk), BF16)]
    if emit_f32:
        out_specs += [_row_spec(TM, nk), _row_spec(TM, nk)]
        out_shape += [jax.ShapeDtypeStruct((n, nk), F32)] * 2
    return pl.pallas_call(
        functools.partial(_qkv_kernel, nq=nq, nk=nk, norm=norm, rope=rope, emit_f32=emit_f32, qscale=qscale),
        grid=(n // TM,),
        in_specs=in_specs,
        out_specs=out_specs,
        out_shape=out_shape,
        compiler_params=_params("arbitrary"),
        name="qkv_norm_rope" if norm else "qkv",
    )(*args)


def _attend(q, kvs, biases=None):
    scores = []
    for idx, (k, _) in enumerate(kvs):
        s = lax.dot_general(q, k, (((1,), (1,)), ((), ())), preferred_element_type=F32)
        if biases is not None and biases[idx] is not None:
            s = s + biases[idx]
        scores.append(s)
    m = jnp.max(scores[0], axis=-1, keepdims=True)
    for s in scores[1:]:
        m = jnp.maximum(m, jnp.max(s, axis=-1, keepdims=True))
    denom = None
    out = None
    for s, (_, v) in zip(scores, kvs):
        p = jnp.exp(s - m)
        part = jnp.sum(p, axis=-1, keepdims=True)
        pv = jnp.dot(p.astype(BF16), v, preferred_element_type=F32)
        denom = part if denom is None else denom + part
        out = pv if out is None else out + pv
    return out * (1.0 / denom)


def _gqa_kernel(*refs, n_kv, group, unit, tq, has_cache):
    if has_cache:
        q_ref, k_ref, v_ref, kc_ref, vc_ref, o_ref = refs
    else:
        q_ref, k_ref, v_ref, o_ref = refs
    d = HEAD_DIM_A
    for g in range(n_kv):
        for u0 in range(0, group, unit):
            heads = [g * group + u0 + j for j in range(unit)]
            qs = jnp.concatenate([q_ref[:, h * d:(h + 1) * d] for h in heads], axis=0)
            kvs = [(k_ref[:, g * d:(g + 1) * d], v_ref[:, g * d:(g + 1) * d])]
            if has_cache:
                kvs.append((kc_ref[:, g * d:(g + 1) * d], vc_ref[:, g * d:(g + 1) * d]))
            o = _attend(qs, kvs)
            for j, h in enumerate(heads):
                o_ref[:, h * d:(h + 1) * d] = o[j * tq:(j + 1) * tq].astype(o_ref.dtype)


def _gqa_ctx(q, k, v):
    nq, nk = N_HEADS_A * HEAD_DIM_A, N_KV_A * HEAD_DIM_A
    return pl.pallas_call(
        functools.partial(_gqa_kernel, n_kv=N_KV_A, group=N_HEADS_A // N_KV_A, unit=N_HEADS_A // N_KV_A, tq=SEQ,
                          has_cache=False),
        grid=(BATCH,),
        in_specs=[_row_spec(SEQ, nq), _row_spec(SEQ, nk), _row_spec(SEQ, nk)],
        out_specs=_row_spec(SEQ, nq),
        out_shape=jax.ShapeDtypeStruct((N_CTX, nq), BF16),
        compiler_params=_params("arbitrary"),
        name="gqa_ctx",
    )(q, k, v)


def _gqa_lat(q, k, v, kc, vc):
    nq, nk = N_HEADS_A * HEAD_DIM_A, N_KV_A * HEAD_DIM_A
    tiles = DEC_SEQ // TQ_A
    kv_spec = pl.BlockSpec((DEC_SEQ, nk), lambda b, t: (b, 0))
    cache_spec = pl.BlockSpec((None, PAST_LEN, nk), lambda b, t: (b, 0, 0))
    q_spec = pl.BlockSpec((TQ_A, nq), lambda b, t: (b * tiles + t, 0))
    return pl.pallas_call(
        functools.partial(_gqa_kernel, n_kv=N_KV_A, group=N_HEADS_A // N_KV_A, unit=UNIT_A, tq=TQ_A, has_cache=True),
        grid=(DEC_BATCH, tiles),
        in_specs=[q_spec, kv_spec, kv_spec, cache_spec, cache_spec],
        out_specs=q_spec,
        out_shape=jax.ShapeDtypeStruct((N_LAT, nq), BF16),
        compiler_params=_params("arbitrary", "arbitrary"),
        name="gqa_lat",
    )(q, k, v, kc, vc)


def _split_pair(qp):
    lo = lax.broadcasted_iota(jnp.int32, qp.shape, 1) < HEAD_DIM_B
    zero = jnp.zeros_like(qp)
    return jnp.concatenate([jnp.where(lo, qp, zero), jnp.where(lo, zero, qp)], axis=0)


def _merge_pair(o2, rows):
    lo = lax.broadcasted_iota(jnp.int32, (rows, LANES), 1) < HEAD_DIM_B
    return jnp.where(lo, o2[:rows], o2[rows:])


def _mha_ctx_kernel(q_ref, k_ref, v_ref, o_ref):
    for p in range(N_HEADS_B // 2):
        sl = slice(p * LANES, (p + 1) * LANES)
        o2 = _attend(_split_pair(q_ref[:, sl]), [(k_ref[:, sl], v_ref[:, sl])])
        o_ref[:, sl] = _merge_pair(o2, SEQ).astype(o_ref.dtype)


def _mha_ctx(q, k, v):
    return pl.pallas_call(
        _mha_ctx_kernel,
        grid=(BATCH,),
        in_specs=[_row_spec(SEQ, D_MODEL)] * 3,
        out_specs=_row_spec(SEQ, D_MODEL),
        out_shape=jax.ShapeDtypeStruct((N_CTX, D_MODEL), BF16),
        compiler_params=_params("arbitrary"),
        name="mha_ctx",
    )(q, k, v)


NAT_HALF = D_MODEL // 2
NAT_PAIRS = NAT_HALF // LANES
NAT_R = 4
NAT_WROWS = 12
N_DROW = 2 * WIN_H - 1
MASKED_TILE = N_DROW


def _natten_kernel(q_ref, k_ref, v_ref, kc_ref, vc_ref, tbl_ref, o_ref):
    r0 = pl.program_id(2) * NAT_R
    ws = jnp.clip(r0 - WIN_H // 2, 0, GRID_H - NAT_WROWS)
    row0 = pl.multiple_of(ws * GRID_W, GRID_W)
    n_loc = NAT_WROWS * GRID_W
    lo = lax.broadcasted_iota(jnp.int32, (LANES, LANES), 1) < GRID_W

    def tile_index(a, w):
        r = r0 + a
        start = jnp.clip(r - WIN_H // 2, 0, GRID_H - WIN_H)
        wr = ws + w
        valid = (wr >= start) & (wr < start + WIN_H)
        return jnp.where(valid, wr - r + (WIN_H - 1), MASKED_TILE)

    idx = [[tile_index(a, w) for w in range(NAT_WROWS)] for a in range(NAT_R)]
    for p in range(NAT_PAIRS):
        sl = slice(p * LANES, (p + 1) * LANES)
        q2 = jnp.concatenate([_split_pair(q_ref[a * GRID_W:(a + 1) * GRID_W, sl]) for a in range(NAT_R)], axis=0)
        bias = jnp.concatenate(
            [jnp.concatenate([jnp.where(lo, tbl_ref[idx[a][w], p], tbl_ref[idx[a][w + 1], p])
                              for w in range(0, NAT_WROWS, 2)], axis=1) for a in range(NAT_R)], axis=0)
        kvs = [(k_ref[pl.ds(row0, n_loc), sl], v_ref[pl.ds(row0, n_loc), sl]), (kc_ref[:, sl], vc_ref[:, sl])]
        o2 = _attend(q2, kvs, biases=[bias, None])
        for a in range(NAT_R):
            o_ref[a * GRID_W:(a + 1) * GRID_W, sl] = _merge_pair(
                o2[a * LANES:(a + 1) * LANES], GRID_W).astype(o_ref.dtype)


def _natten(q, k, v, kc, vc, tbl):
    n_half = D_MODEL // NAT_HALF
    blocks = GRID_H // NAT_R
    q_spec = pl.BlockSpec((NAT_R * GRID_W, NAT_HALF), lambda b, hh, r: (b * blocks + r, hh))
    kv_spec = pl.BlockSpec((DEC_SEQ, NAT_HALF), lambda b, hh, r: (b, hh))
    cache_spec = pl.BlockSpec((None, PAST_LEN, NAT_HALF), lambda b, hh, r: (b, 0, hh))
    tbl_spec = pl.BlockSpec((N_DROW + 1, NAT_PAIRS, LANES, LANES), lambda b, hh, r: (0, hh, 0, 0))
    return pl.pallas_call(
        _natten_kernel,
        grid=(DEC_BATCH, n_half, blocks),
        in_specs=[q_spec, kv_spec, kv_spec, cache_spec, cache_spec, tbl_spec],
        out_specs=q_spec,
        out_shape=jax.ShapeDtypeStruct((N_LAT, D_MODEL), BF16),
        compiler_params=_params("arbitrary", "arbitrary", "arbitrary"),
        name="natten",
    )(q, k, v, kc, vc, tbl)


def _bias_table_kernel(rpb_ref, o_ref):
    dr = pl.program_id(0)
    pr = pl.program_id(1)
    row = lax.broadcasted_iota(jnp.int32, (LANES, LANES), 0)
    col = lax.broadcasted_iota(jnp.int32, (LANES, LANES), 1)
    sub = row >= GRID_W
    qc = row % GRID_W
    kc = col % GRID_W
    dc = jnp.clip(kc - qc, -(WIN_W - 1), WIN_W - 1) + (WIN_W - 1)
    col_start = jnp.clip(qc - WIN_W // 2, 0, GRID_W - WIN_W)
    in_win = (kc >= col_start) & (kc < col_start + WIN_W) & (dr < N_DROW)
    n_dc = 2 * WIN_W - 1
    drc = jnp.minimum(dr, N_DROW - 1)
    b0 = ((2 * pr) * N_DROW + drc) * n_dc
    b1 = ((2 * pr + 1) * N_DROW + drc) * n_dc

    def body(j, acc):
        return jnp.where(dc == j, jnp.where(sub, rpb_ref[b1 + j], rpb_ref[b0 + j]), acc)

    acc = lax.fori_loop(0, n_dc, body, jnp.zeros((LANES, LANES), F32))
    o_ref[...] = jnp.where(in_win, acc, NEG_INF)


def _bias_table(rpb):
    return pl.pallas_call(
        _bias_table_kernel,
        grid=(N_DROW + 1, N_HEADS_B // 2),
        in_specs=[pl.BlockSpec(memory_space=pltpu.SMEM)],
        out_specs=pl.BlockSpec((None, None, LANES, LANES), lambda d, p: (d, p, 0, 0)),
        out_shape=jax.ShapeDtypeStruct((N_DROW + 1, N_HEADS_B // 2, LANES, LANES), F32),
        compiler_params=_params("arbitrary", "arbitrary"),
        name="natten_bias_table",
    )(rpb.reshape(-1))


def _wo_ln_kernel(a_ref, w_ref, x_ref, gate_ref, g_ref, b_ref, o_ref):
    o = jnp.dot(a_ref[...], w_ref[...], preferred_element_type=F32)
    y = ALPHA * x_ref[...] + gate_ref[...] * o
    o_ref[...] = _layer_norm(y, g_ref[...], b_ref[...])


def _wo_ln(stream, a, w, x, gate, ln_g, ln_b):
    vec = _full_spec((1, D_MODEL))
    return pl.pallas_call(
        _wo_ln_kernel,
        grid=(stream.n // TM,),
        in_specs=[_row_spec(TM, a.shape[1]), _full_spec(w.shape), _row_spec(TM, D_MODEL),
                  _mod_spec(stream, TM), vec, vec],
        out_specs=_row_spec(TM, D_MODEL),
        out_shape=jax.ShapeDtypeStruct((stream.n, D_MODEL), F32),
        compiler_params=_params("arbitrary"),
        name="wo_postnorm",
    )(a, w, x, gate, ln_g, ln_b)


def _swiglu_step(h_scr, acc_scr, wg_ref, wu_ref, wd_ref):
    h = h_scr[...]
    g = jnp.dot(h, wg_ref[...], preferred_element_type=F32)
    u = jnp.dot(h, wu_ref[...], preferred_element_type=F32)
    a = (g * jax.nn.sigmoid(g) * u).astype(BF16)
    acc_scr[...] += jnp.dot(a, wd_ref[...], preferred_element_type=F32)


def _ffn_dense_kernel(x_ref, sh_ref, sc_ref, wg_ref, wu_ref, wd_ref, gate_ref, g_ref, b_ref, o_ref,
                      h_scr, acc_scr, *, nj):
    j = pl.program_id(1)

    @pl.when(j == 0)
    def _():
        h_scr[...] = (x_ref[...] * (1.0 + sc_ref[...]) + sh_ref[...]).astype(BF16)
        acc_scr[...] = jnp.zeros_like(acc_scr)

    _swiglu_step(h_scr, acc_scr, wg_ref, wu_ref, wd_ref)

    @pl.when(j == nj - 1)
    def _():
        y = ALPHA * x_ref[...] + gate_ref[...] * acc_scr[...]
        o_ref[...] = _layer_norm(y, g_ref[...], b_ref[...])


def _ffn_dense(stream, x, shift, scale, w_gu, w_down, gate, ln_g, ln_b):
    nj = D_FF // TF
    vec = pl.BlockSpec((1, D_MODEL), lambda i, j: (0, 0))
    mod = pl.BlockSpec((None, 1, D_MODEL), stream.group_map(TM))
    row = pl.BlockSpec((TM, D_MODEL), lambda i, j: (i, 0))
    return pl.pallas_call(
        functools.partial(_ffn_dense_kernel, nj=nj),
        grid=(stream.n // TM, nj),
        in_specs=[row, mod, mod,
                  pl.BlockSpec((D_MODEL, TF), lambda i, j: (0, j)),
                  pl.BlockSpec((D_MODEL, TF), lambda i, j: (0, nj + j)),
                  pl.BlockSpec((TF, D_MODEL), lambda i, j: (j, 0)),
                  mod, vec, vec],
        out_specs=row,
        out_shape=jax.ShapeDtypeStruct((stream.n, D_MODEL), F32),
        scratch_shapes=[pltpu.VMEM((TM, D_MODEL), BF16), pltpu.VMEM((TM, D_MODEL), F32)],
        compiler_params=_params("arbitrary", "arbitrary"),
        name="ffn_dense",
    )(x, shift, scale, w_gu, w_gu, w_down, gate, ln_g, ln_b)


def _ffn_grouped_kernel(te_ref, tv_ref, x_ref, wg_ref, wu_ref, wd_ref, o_ref, h_scr, acc_scr, *, nj):
    i = pl.program_id(0)
    j = pl.program_id(1)

    @pl.when(tv_ref[i] != 0)
    def _():
        @pl.when(j == 0)
        def _():
            h_scr[...] = x_ref[...].astype(BF16)
            acc_scr[...] = jnp.zeros_like(acc_scr)

        _swiglu_step(h_scr, acc_scr, wg_ref, wu_ref, wd_ref)

        @pl.when(j == nj - 1)
        def _():
            o_ref[...] = acc_scr[...]

    @pl.when((tv_ref[i] == 0) & (j == nj - 1))
    def _():
        o_ref[...] = jnp.zeros_like(o_ref)


def _ffn_grouped(tile_expert, tile_valid, xs, w_gu, w_down):
    nj = D_FF // TF
    row = pl.BlockSpec((TMG, D_MODEL), lambda i, j, te, tv: (i, 0))
    grid_spec = pltpu.PrefetchScalarGridSpec(
        num_scalar_prefetch=2,
        grid=(NT_MOE, nj),
        in_specs=[row,
                  pl.BlockSpec((None, D_MODEL, TF), lambda i, j, te, tv: (te[i], 0, j)),
                  pl.BlockSpec((None, D_MODEL, TF), lambda i, j, te, tv: (te[i], 0, nj + j)),
                  pl.BlockSpec((None, TF, D_MODEL), lambda i, j, te, tv: (te[i], j, 0))],
        out_specs=row,
        scratch_shapes=[pltpu.VMEM((TMG, D_MODEL), BF16), pltpu.VMEM((TMG, D_MODEL), F32)],
    )
    return pl.pallas_call(
        functools.partial(_ffn_grouped_kernel, nj=nj),
        grid_spec=grid_spec,
        out_shape=jax.ShapeDtypeStruct((NT_MOE * TMG, D_MODEL), F32),
        compiler_params=_params("arbitrary", "arbitrary"),
        name="ffn_grouped",
    )(tile_expert, tile_valid, xs, w_gu, w_gu, w_down)


META_E0, META_E1, META_W0, META_W1, META_R0, META_R1 = range(6)


def _router_kernel(x_ref, sh_ref, sc_ref, rw_ref, cnt_in_ref, h_ref, meta_ref, cnt_ref, carry_scr):
    i = pl.program_id(0)

    @pl.when(i == 0)
    def _():
        carry_scr[...] = cnt_in_ref[...]

    h = x_ref[...] * (1.0 + sc_ref[...]) + sh_ref[...]
    h_ref[...] = h
    logits = jnp.dot(h, rw_ref[...], preferred_element_type=F32, precision=HIGHEST)
    lane = lax.broadcasted_iota(jnp.int32, logits.shape, 1).astype(F32)
    lg = jnp.where(lane < N_EXPERTS, logits, -jnp.inf)
    m1 = jnp.max(lg, axis=-1, keepdims=True)
    i1 = jnp.min(jnp.where(lg == m1, lane, float(LANES)), axis=-1, keepdims=True)
    lg2 = jnp.where(lane == i1, -jnp.inf, lg)
    m2 = jnp.max(lg2, axis=-1, keepdims=True)
    i2 = jnp.min(jnp.where(lg2 == m2, lane, float(LANES)), axis=-1, keepdims=True)
    e = jnp.exp(m2 - m1)
    w1 = 1.0 / (1.0 + e)
    w2 = e / (1.0 + e)

    sel1 = lane == i1
    sel2 = lane == i2
    onehot = jnp.where(sel1 | sel2, 1.0, 0.0)
    rr = lax.broadcasted_iota(jnp.int32, (TM, TM), 0)
    cc = lax.broadcasted_iota(jnp.int32, (TM, TM), 1)
    lower = jnp.where(cc < rr, 1.0, 0.0).astype(BF16)
    before = jnp.dot(lower, onehot.astype(BF16), preferred_element_type=F32) + carry_scr[0:1, :]
    r1 = jnp.sum(jnp.where(sel1, before, 0.0), axis=-1, keepdims=True)
    r2 = jnp.sum(jnp.where(sel2, before, 0.0), axis=-1, keepdims=True)
    carry_scr[...] = carry_scr[...] + jnp.sum(onehot, axis=0, keepdims=True)
    cnt_ref[...] = carry_scr[...]

    cols = [i1, i2, w1, w2, r1, r2]
    meta = jnp.zeros(logits.shape, F32)
    for c, val in enumerate(cols):
        meta = jnp.where(lane == c, val, meta)
    meta_ref[...] = meta


def _router(stream, x, shift, scale, rw_pad, cnt_in):
    cnt_spec = _full_spec((SUBLANES, LANES))
    return pl.pallas_call(
        _router_kernel,
        grid=(stream.n // TM,),
        in_specs=[_row_spec(TM, D_MODEL), _mod_spec(stream, TM), _mod_spec(stream, TM),
                  _full_spec((D_MODEL, LANES)), cnt_spec],
        out_specs=[_row_spec(TM, D_MODEL), _row_spec(TM, LANES), cnt_spec],
        out_shape=[jax.ShapeDtypeStruct((stream.n, D_MODEL), F32), jax.ShapeDtypeStruct((stream.n, LANES), F32),
                   jax.ShapeDtypeStruct((SUBLANES, LANES), F32)],
        scratch_shapes=[pltpu.VMEM((SUBLANES, LANES), F32)],
        compiler_params=_params("arbitrary"),
        name="moe_router",
    )(x, shift, scale, rw_pad, cnt_in)


def _row_copy(src_ref, src_row, dst_ref, dst_row, sem):
    return pltpu.make_async_copy(src_ref.at[pl.ds(src_row, 1)], dst_ref.at[pl.ds(dst_row, 1)], sem)


def _scatter_kernel(d0_ref, d1_ref, h_ref, buf_in_ref, buf_ref, sem):
    del buf_in_ref
    base = pl.program_id(0) * TM

    def issue(c, carry):
        for u in range(ROW_UNROLL):
            r = c * ROW_UNROLL + u
            _row_copy(h_ref, r, buf_ref, d0_ref[base + r], sem).start(priority=0)
            _row_copy(h_ref, r, buf_ref, d1_ref[base + r], sem).start(priority=1)
        return carry

    lax.fori_loop(0, TM // ROW_UNROLL, issue, 0)
    for _ in range(2):
        pltpu.make_async_copy(h_ref, buf_ref.at[pl.ds(0, TM)], sem).wait()


def _scatter(stream, d0, d1, h, buf):
    grid_spec = pltpu.PrefetchScalarGridSpec(
        num_scalar_prefetch=2,
        grid=(stream.n // TM,),
        in_specs=[pl.BlockSpec((TM, D_MODEL), lambda i, a, b: (i, 0)), pl.BlockSpec(memory_space=pl.ANY)],
        out_specs=pl.BlockSpec(memory_space=pl.ANY),
        scratch_shapes=[pltpu.SemaphoreType.DMA],
    )
    return pl.pallas_call(
        _scatter_kernel,
        grid_spec=grid_spec,
        out_shape=jax.ShapeDtypeStruct(buf.shape, buf.dtype),
        input_output_aliases={3: 0},
        compiler_params=_params("arbitrary"),
        name="moe_scatter",
    )(d0, d1, h, buf)


def _combine_kernel(d0_ref, d1_ref, ys_ref, meta_ref, x_ref, gate_ref, g_ref, b_ref, o_ref, rows_scr, sem):
    base = pl.program_id(0) * TM

    def issue(c, carry):
        for u in range(ROW_UNROLL):
            r = c * ROW_UNROLL + u
            _row_copy(ys_ref, d0_ref[base + r], rows_scr.at[0], r, sem).start(priority=0)
            _row_copy(ys_ref, d1_ref[base + r], rows_scr.at[1], r, sem).start(priority=1)
        return carry

    lax.fori_loop(0, TM // ROW_UNROLL, issue, 0)
    for s in range(2):
        pltpu.make_async_copy(ys_ref.at[pl.ds(0, TM)], rows_scr.at[s], sem).wait()

    meta = meta_ref[...]
    w0 = meta[:, META_W0:META_W0 + 1]
    w1 = meta[:, META_W1:META_W1 + 1]
    f = w0 * rows_scr[0] + w1 * rows_scr[1]
    y = ALPHA * x_ref[...] + gate_ref[...] * f
    o_ref[...] = _layer_norm(y, g_ref[...], b_ref[...])


def _combine(stream, d0, d1, ys, meta, x, gate, ln_g, ln_b):
    vec = pl.BlockSpec((1, D_MODEL), lambda i, a, b: (0, 0))
    grid_spec = pltpu.PrefetchScalarGridSpec(
        num_scalar_prefetch=2,
        grid=(stream.n // TM,),
        in_specs=[pl.BlockSpec(memory_space=pl.ANY), _row_spec(TM, LANES), _row_spec(TM, D_MODEL),
                  _mod_spec(stream, TM), vec, vec],
        out_specs=_row_spec(TM, D_MODEL),
        scratch_shapes=[pltpu.VMEM((2, TM, D_MODEL), F32), pltpu.SemaphoreType.DMA],
    )
    return pl.pallas_call(
        _combine_kernel,
        grid_spec=grid_spec,
        out_shape=jax.ShapeDtypeStruct((stream.n, D_MODEL), F32),
        compiler_params=_params("arbitrary"),
        name="moe_combine",
    )(d0, d1, ys, meta, x, gate, ln_g, ln_b)


def _rope_tables():
    t = jnp.arange(DEC_SEQ)
    row = (t // GRID_W).astype(F32)
    col = (t % GRID_W).astype(F32)
    freqs = ROPE_THETA ** (-jnp.arange(ROT_FREQS, dtype=F32) / ROT_FREQS)
    ar = row[:, None] * freqs
    ac = col[:, None] * freqs
    cos = jnp.concatenate([jnp.cos(ar), jnp.cos(ar), jnp.cos(ac), jnp.cos(ac)], axis=1)
    sin = jnp.concatenate([-jnp.sin(ar), jnp.sin(ar), -jnp.sin(ac), jnp.sin(ac)], axis=1)
    return cos, sin


def _routing_plan(metas, counts):
    cnt = counts[0, :N_EXPERTS].astype(jnp.int32)
    tiles_e = (cnt + TMG - 1) // TMG
    tile_end = jnp.cumsum(tiles_e)
    tile_start = tile_end - tiles_e
    offs = tile_start * TMG
    dests = []
    for meta in metas:
        pair = []
        for ecol, rcol in ((META_E0, META_R0), (META_E1, META_R1)):
            e = meta[:, ecol].astype(jnp.int32)
            d = meta[:, rcol].astype(jnp.int32)
            for k in range(N_EXPERTS):
                d = d + jnp.where(e == k, offs[k], 0)
            pair.append(d)
        dests.append(tuple(pair))
    tid = jnp.arange(NT_MOE, dtype=jnp.int32)
    te = jnp.minimum(jnp.sum((tid[:, None] >= tile_end[None, :]).astype(jnp.int32), axis=1), N_EXPERTS - 1)
    total = tile_end[-1]
    valid = tid < total
    te_last = jnp.max(jnp.where(valid, te, 0))
    te = jnp.where(valid, te, te_last)
    return dests, te, valid.astype(jnp.int32)


def kernel(x_prompt, x_sample, cache_k_a, cache_v_a, cache_k_b, cache_v_b, c, c_ctx, ada_w, ada_b, ln_attn_g, ln_attn_b, ln_ffn_g, ln_ffn_b, wqkv_a, qnorm_a, knorm_a, wo_a, wqkv_b, rpb_b, wo_b, ffn_w_gu, ffn_w_down, router_w, moe_w_gu, moe_w_down):
    streams = (CTX, LAT)
    xs = [x_prompt.reshape(N_CTX, D_MODEL), x_sample.reshape(N_LAT, D_MODEL)]

    cond = jnp.zeros((GROUP_PAD, D_MODEL), F32).at[0].set(c_ctx).at[1:N_GROUPS].set(c)
    mods = _ada_mods(cond, ada_w, ada_b)
    vec = lambda a, l: a[l].reshape(1, D_MODEL)

    m = mods[0]
    w_qkv = wqkv_a[0].astype(BF16)
    w_o = wo_a[0].astype(BF16)
    w_gu = ffn_w_gu[0].astype(BF16)
    w_dn = ffn_w_down[0].astype(BF16)
    gains = (qnorm_a[0].reshape(1, HEAD_DIM_A), knorm_a[0].reshape(1, HEAD_DIM_A))
    nq, nk = N_HEADS_A * HEAD_DIM_A, N_KV_A * HEAD_DIM_A
    qscale = HEAD_DIM_A ** -0.5

    qp, kp, vp, kp32, vp32 = _qkv(CTX, xs[0], m[0], m[1], w_qkv, nq=nq, nk=nk, qscale=qscale, gains=gains,
                                  emit_f32=True)
    new_k_a = kp32.reshape(BATCH, 1, SEQ, N_KV_A, HEAD_DIM_A)
    new_v_a = vp32.reshape(BATCH, 1, SEQ, N_KV_A, HEAD_DIM_A)
    ql, kl, vl = _qkv(LAT, xs[1], m[0], m[1], w_qkv, nq=nq, nk=nk, qscale=qscale, gains=gains,
                      rope_tables=_rope_tables())
    attn = [_gqa_ctx(qp, kp, vp),
            _gqa_lat(ql, kl, vl, cache_k_a[:, 0].reshape(DEC_BATCH, PAST_LEN, nk).astype(BF16),
                     cache_v_a[:, 0].reshape(DEC_BATCH, PAST_LEN, nk).astype(BF16))]
    xs = [_wo_ln(s, a, w_o, x, m[2], vec(ln_attn_g, 0), vec(ln_attn_b, 0)) for s, a, x in zip(streams, attn, xs)]
    xs = [_ffn_dense(s, x, m[3], m[4], w_gu, w_dn, m[5], vec(ln_ffn_g, 0), vec(ln_ffn_b, 0))
          for s, x in zip(streams, xs)]

    m = mods[1]
    w_qkv = wqkv_b[0].astype(BF16)
    w_o = wo_b[0].astype(BF16)
    qscale = HEAD_DIM_B ** -0.5
    qp, kp, vp, kp32, vp32 = _qkv(CTX, xs[0], m[0], m[1], w_qkv, nq=D_MODEL, nk=D_MODEL, qscale=qscale,
                                  emit_f32=True)
    new_k_b = kp32.reshape(BATCH, 1, SEQ, N_HEADS_B, HEAD_DIM_B)
    new_v_b = vp32.reshape(BATCH, 1, SEQ, N_HEADS_B, HEAD_DIM_B)
    ql, kl, vl = _qkv(LAT, xs[1], m[0], m[1], w_qkv, nq=D_MODEL, nk=D_MODEL, qscale=qscale)
    attn = [_mha_ctx(qp, kp, vp),
            _natten(ql, kl, vl, cache_k_b[:, 0].reshape(DEC_BATCH, PAST_LEN, D_MODEL).astype(BF16),
                    cache_v_b[:, 0].reshape(DEC_BATCH, PAST_LEN, D_MODEL).astype(BF16), _bias_table(rpb_b[0]))]
    xs = [_wo_ln(s, a, w_o, x, m[2], vec(ln_attn_g, 1), vec(ln_attn_b, 1)) for s, a, x in zip(streams, attn, xs)]

    rw_pad = jnp.zeros((D_MODEL, LANES), F32).at[:, :N_EXPERTS].set(router_w[0])
    counts = jnp.zeros((SUBLANES, LANES), F32)
    hs, metas = [], []
    for s, x in zip(streams, xs):
        h, meta, counts = _router(s, x, m[3], m[4], rw_pad, counts)
        hs.append(h)
        metas.append(meta)
    dests, tile_expert, tile_valid = _routing_plan(metas, counts)
    sorted_rows = jnp.zeros((NT_MOE * TMG, D_MODEL), F32)
    for s, (d0, d1), h in zip(streams, dests, hs):
        sorted_rows = _scatter(s, d0, d1, h, sorted_rows)
    ys = _ffn_grouped(tile_expert, tile_valid, sorted_rows, moe_w_gu[0].astype(BF16), moe_w_down[0].astype(BF16))
    outs = [_combine(s, d0, d1, ys, meta, x, m[5], vec(ln_ffn_g, 1), vec(ln_ffn_b, 1))
            for s, (d0, d1), meta, x in zip(streams, dests, metas, xs)]

    y_prompt = outs[0].reshape(BATCH, SEQ, D_MODEL)
    y_sample = outs[1].reshape(DEC_BATCH, DEC_SEQ, D_MODEL)
    return (y_prompt, y_sample, new_k_a, new_v_a, new_k_b, new_v_b)
```

```python
import functools
import math

import jax
import jax.numpy as jnp
from jax import lax
from jax.experimental import pallas as pl
from jax.experimental.pallas import tpu as pltpu

F32 = jnp.float32
BF16 = jnp.bfloat16
HIGHEST = lax.Precision.HIGHEST

D_MODEL = 1024
BATCH, SEQ = 32, 256
DEC_BATCH, DEC_SEQ = 4, 4096
PAST_LEN = 256
DEPTH = 2
GRID_W = 64
GRID_H = DEC_SEQ // GRID_W
N_HEADS_A, N_KV_A, HEAD_DIM_A = 8, 2, 128
ROT_FREQS = HEAD_DIM_A // 4
ROPE_THETA = 10000.0
N_HEADS_B, HEAD_DIM_B = 16, 64
WIN_H, WIN_W = 8, 16
D_FF = 2816
N_EXPERTS = 8
EPS = 1e-6
NEG_INF = -1e30
ALPHA = (2.0 * DEPTH) ** 0.25
LOG2_E = math.log2(math.e)

N_CTX = BATCH * SEQ
N_LAT = DEC_BATCH * DEC_SEQ
N_GROUPS = 1 + DEC_BATCH
GROUP_PAD = 8

LANES = 128
SUBLANES = 8
VMEM_LIMIT = 56 * 2**20

TM = 512
MXU_DIM = 256
FF_CHUNKS = ((0, 1024), (1024, 2048), (2048, D_FF))
TM_FFN = 512
TMG = 512
TQ_A = 128
UNIT_A = 4
TN_ADA = 1536
ROW_UNROLL = 8
N_PAIRS = 2 * (N_CTX + N_LAT)
NT_MOE = N_PAIRS // TMG + N_EXPERTS


class Stream:
    def __init__(self, n_rows, group_offset, rows_per_group):
        self.n = n_rows
        self.goff = group_offset
        self.rpg = rows_per_group

    def group_map(self, tm):
        tiles_per_group = self.rpg // tm
        goff = self.goff
        return lambda i, *_: (goff + i // tiles_per_group, 0, 0)


CTX = Stream(N_CTX, 0, N_CTX)
LAT = Stream(N_LAT, 1, DEC_SEQ)


def _params(*sem):
    return pltpu.CompilerParams(dimension_semantics=sem, vmem_limit_bytes=VMEM_LIMIT)


def _mod_spec(stream, tm):
    return pl.BlockSpec((None, 1, D_MODEL), stream.group_map(tm))


def _row_spec(tm, width):
    return pl.BlockSpec((tm, width), lambda i, *_: (i, 0))


def _full_spec(shape):
    nd = len(shape)
    return pl.BlockSpec(shape, lambda *_: (0,) * nd)


def _layer_norm(y, g, b):
    mu = jnp.mean(y, axis=-1, keepdims=True)
    d = y - mu
    var = jnp.mean(d * d, axis=-1, keepdims=True)
    return d * lax.rsqrt(var + EPS) * g + b


def _ada_kernel(c_ref, w_ref, b_ref, o_ref):
    c = c_ref[...]
    s = c * jax.nn.sigmoid(c)
    o_ref[...] = jnp.dot(s, w_ref[...], preferred_element_type=F32, precision=HIGHEST) + b_ref[...]


def _ada_mods(cond, ada_w, ada_b):
    n_out = 6 * D_MODEL
    out = pl.pallas_call(
        _ada_kernel,
        grid=(DEPTH, n_out // TN_ADA),
        in_specs=[
            pl.BlockSpec((GROUP_PAD, D_MODEL), lambda l, n: (0, 0)),
            pl.BlockSpec((None, D_MODEL, TN_ADA), lambda l, n: (l, 0, n)),
            pl.BlockSpec((None, 1, TN_ADA), lambda l, n: (l, 0, n)),
        ],
        out_specs=pl.BlockSpec((None, GROUP_PAD, TN_ADA), lambda l, n: (l, 0, n)),
        out_shape=jax.ShapeDtypeStruct((DEPTH, GROUP_PAD, n_out), F32),
        compiler_params=_params("arbitrary", "arbitrary"),
        name="ada_mods",
    )(cond, ada_w, ada_b.reshape(DEPTH, 1, n_out))
    out = out.reshape(DEPTH, GROUP_PAD, 6, D_MODEL).transpose(0, 2, 1, 3)
    return out[:, :, :, None, :]


def _swap_halves(t):
    lane = lax.broadcasted_iota(jnp.int32, t.shape, 1)
    fwd = pltpu.roll(t, LANES - ROT_FREQS, 1)
    bwd = pltpu.roll(t, ROT_FREQS, 1)
    return jnp.where((lane % (2 * ROT_FREQS)) < ROT_FREQS, fwd, bwd)


def _qkv_kernel(*refs, nq, nk, norm, rope, emit_f32, v_transposed, qscale):
    refs = list(refs)
    x_ref, sh_ref, sc_ref, w_ref = refs[:4]
    pos = 4
    if norm:
        qg_ref, kg_ref = refs[pos:pos + 2]
        pos += 2
    if rope:
        cos_ref, sin_ref = refs[pos:pos + 2]
        pos += 2
    q_ref, k_ref, v_ref = refs[pos:pos + 3]
    pos += 3
    if emit_f32:
        kf_ref, vf_ref = refs[pos:pos + 2]

    h = (x_ref[...] * (1.0 + sc_ref[...]) + sh_ref[...]).astype(BF16)
    qkv = jnp.dot(h, w_ref[...], preferred_element_type=F32)
    if norm:
        n_heads = (nq + nk) // HEAD_DIM_A
        for hd in range(n_heads):
            lo = hd * HEAD_DIM_A
            t = qkv[:, lo:lo + HEAD_DIM_A]
            ms = jnp.mean(t * t, axis=-1, keepdims=True)
            gain = qg_ref[...] if lo < nq else kg_ref[...]
            t = t * lax.rsqrt(ms + EPS) * gain
            if rope:
                t = t * cos_ref[...] + _swap_halves(t) * sin_ref[...]
            if lo < nq:
                q_ref[:, lo:lo + HEAD_DIM_A] = (t * qscale).astype(BF16)
            else:
                k_ref[:, lo - nq:lo - nq + HEAD_DIM_A] = t.astype(BF16)
                if emit_f32:
                    kf_ref[:, lo - nq:lo - nq + HEAD_DIM_A] = t
    else:
        q_ref[...] = (qkv[:, :nq] * qscale).astype(BF16)
        k = qkv[:, nq:nq + nk]
        k_ref[...] = k.astype(BF16)
        if emit_f32:
            kf_ref[...] = k
    v = qkv[:, nq + nk:]
    v_ref[...] = (v.T if v_transposed else v).astype(BF16)
    if emit_f32:
        vf_ref[...] = v


def _qkv(stream, x, shift, scale, w, *, nq, nk, qscale, gains=None, rope_tables=None, emit_f32=False,
         v_transposed=False):
    n = stream.n
    nw = w.shape[1]
    norm = gains is not None
    rope = rope_tables is not None
    in_specs = [_row_spec(TM, D_MODEL), _mod_spec(stream, TM), _mod_spec(stream, TM), _full_spec((D_MODEL, nw))]
    args = [x, shift, scale, w]
    if norm:
        in_specs += [_full_spec((1, HEAD_DIM_A))] * 2
        args += list(gains)
    if rope:
        tiles_per_seq = DEC_SEQ // TM
        tbl_spec = pl.BlockSpec((TM, HEAD_DIM_A), lambda i: (i % tiles_per_seq, 0))
        in_specs += [tbl_spec, tbl_spec]
        args += list(rope_tables)
    out_specs = [_row_spec(TM, nq), _row_spec(TM, nk), _row_spec(TM, nk)]
    out_shape = [jax.ShapeDtypeStruct((n, nq), BF16), jax.ShapeDtypeStruct((n, nk), BF16),
                 jax.ShapeDtypeStruct((n, nk), BF16)]
    if v_transposed:
        out_specs[2] = pl.BlockSpec((nk, TM), lambda i: (0, i))
        out_shape[2] = jax.ShapeDtypeStruct((nk, n), BF16)
    if emit_f32:
        out_specs += [_row_spec(TM, nk), _row_spec(TM, nk)]
        out_shape += [jax.ShapeDtypeStruct((n, nk), F32)] * 2
    return pl.pallas_call(
        functools.partial(_qkv_kernel, nq=nq, nk=nk, norm=norm, rope=rope, emit_f32=emit_f32,
                          v_transposed=v_transposed, qscale=qscale),
        grid=(n // TM,),
        in_specs=in_specs,
        out_specs=out_specs,
        out_shape=out_shape,
        compiler_params=_params("arbitrary"),
        name="qkv_norm_rope" if norm else "qkv",
    )(*args)


def _attend(q, kvs, biases=None):
    scores = []
    for idx, (k, _) in enumerate(kvs):
        s = lax.dot_general(q, k, (((1,), (1,)), ((), ())), preferred_element_type=F32)
        if biases is not None and biases[idx] is not None:
            s = s + biases[idx]
        scores.append(s)
    m = jnp.max(scores[0], axis=-1, keepdims=True)
    for s in scores[1:]:
        m = jnp.maximum(m, jnp.max(s, axis=-1, keepdims=True))
    denom = None
    out = None
    for s, (_, v) in zip(scores, kvs):
        p = jnp.exp(s - m)
        part = jnp.sum(p, axis=-1, keepdims=True)
        pv = jnp.dot(p.astype(BF16), v, preferred_element_type=F32)
        denom = part if denom is None else denom + part
        out = pv if out is None else out + pv
    return out * (1.0 / denom)


def _gqa_ctx_kernel(q_ref, k_ref, v_ref, o_ref):
    d = HEAD_DIM_A
    group = N_HEADS_A // N_KV_A
    for g in range(N_KV_A):
        heads = [g * group + j for j in range(group)]
        qs = jnp.concatenate([q_ref[:, h * d:(h + 1) * d] for h in heads], axis=0)
        o = _attend(qs, [(k_ref[:, g * d:(g + 1) * d], v_ref[:, g * d:(g + 1) * d])])
        for j, h in enumerate(heads):
            o_ref[:, h * d:(h + 1) * d] = o[j * SEQ:(j + 1) * SEQ].astype(o_ref.dtype)


def _gqa_ctx(q, k, v):
    nq, nk = N_HEADS_A * HEAD_DIM_A, N_KV_A * HEAD_DIM_A
    return pl.pallas_call(
        _gqa_ctx_kernel,
        grid=(BATCH,),
        in_specs=[_row_spec(SEQ, nq), _row_spec(SEQ, nk), _row_spec(SEQ, nk)],
        out_specs=_row_spec(SEQ, nq),
        out_shape=jax.ShapeDtypeStruct((N_CTX, nq), BF16),
        compiler_params=_params("arbitrary"),
        name="gqa_ctx",
    )(q, k, v)


ONES_ROWS = 16


def _scores_t(q, ks):
    return [lax.dot_general(k, q, (((1,), (1,)), ((), ())), preferred_element_type=F32) for k in ks]


def _softmax_pv_t(scores, vts):
    m = jnp.max(scores[0], axis=0, keepdims=True)
    for s in scores[1:]:
        m = jnp.maximum(m, jnp.max(s, axis=0, keepdims=True))
    acc = None
    for s, vt in zip(scores, vts):
        p = jnp.exp2(s - m).astype(BF16)
        vt_ones = jnp.concatenate([vt, jnp.ones((ONES_ROWS, vt.shape[1]), BF16)], axis=0)
        pv = jnp.dot(vt_ones, p, preferred_element_type=F32)
        acc = pv if acc is None else acc + pv
    d = HEAD_DIM_A
    return (acc[:d] * (1.0 / acc[d:d + 1])).T


def _gqa_t_kernel(q_ref, k_ref, vt_ref, kc_ref, vct_ref, o_ref, *, n_kv, group, unit, tq):
    d = HEAD_DIM_A
    units = [(g, [g * group + u0 + j for j in range(unit)]) for g in range(n_kv) for u0 in range(0, group, unit)]
    scores = []
    for g, heads in units:
        ds = slice(g * d, (g + 1) * d)
        qs = jnp.concatenate([q_ref[:, h * d:(h + 1) * d] for h in heads], axis=0)
        scores.append(_scores_t(qs, [k_ref[:, ds], kc_ref[:, ds]]))
    for (g, heads), sc in zip(units, scores):
        ds = slice(g * d, (g + 1) * d)
        o = _softmax_pv_t(sc, [vt_ref[ds, :], vct_ref[ds, :]])
        for j, h in enumerate(heads):
            o_ref[:, h * d:(h + 1) * d] = o[j * tq:(j + 1) * tq].astype(o_ref.dtype)


def _gqa_lat_t(q, k, vt, kc, vct):
    nq, nk = N_HEADS_A * HEAD_DIM_A, N_KV_A * HEAD_DIM_A
    tiles = DEC_SEQ // TQ_A
    q_spec = pl.BlockSpec((TQ_A, nq), lambda b, t: (b * tiles + t, 0))
    return pl.pallas_call(
        functools.partial(_gqa_t_kernel, n_kv=N_KV_A, group=N_HEADS_A // N_KV_A, unit=UNIT_A, tq=TQ_A),
        grid=(DEC_BATCH, tiles),
        in_specs=[q_spec,
                  pl.BlockSpec((DEC_SEQ, nk), lambda b, t: (b, 0)),
                  pl.BlockSpec((nk, DEC_SEQ), lambda b, t: (0, b)),
                  pl.BlockSpec((None, PAST_LEN, nk), lambda b, t: (b, 0, 0)),
                  pl.BlockSpec((None, nk, PAST_LEN), lambda b, t: (b, 0, 0))],
        out_specs=q_spec,
        out_shape=jax.ShapeDtypeStruct((N_LAT, nq), BF16),
        compiler_params=_params("arbitrary", "arbitrary"),
        name="gqa_lat",
    )(q, k, vt, kc, vct)


def _split_pair(qp):
    lo = lax.broadcasted_iota(jnp.int32, qp.shape, 1) < HEAD_DIM_B
    zero = jnp.zeros_like(qp)
    return jnp.concatenate([jnp.where(lo, qp, zero), jnp.where(lo, zero, qp)], axis=0)


def _merge_pair(o2, rows):
    lo = lax.broadcasted_iota(jnp.int32, (rows, LANES), 1) < HEAD_DIM_B
    return jnp.where(lo, o2[:rows], o2[rows:])


def _mha_ctx_kernel(q_ref, k_ref, v_ref, o_ref):
    for p in range(N_HEADS_B // 2):
        sl = slice(p * LANES, (p + 1) * LANES)
        o2 = _attend(_split_pair(q_ref[:, sl]), [(k_ref[:, sl], v_ref[:, sl])])
        o_ref[:, sl] = _merge_pair(o2, SEQ).astype(o_ref.dtype)


def _mha_ctx(q, k, v):
    return pl.pallas_call(
        _mha_ctx_kernel,
        grid=(BATCH,),
        in_specs=[_row_spec(SEQ, D_MODEL)] * 3,
        out_specs=_row_spec(SEQ, D_MODEL),
        out_shape=jax.ShapeDtypeStruct((N_CTX, D_MODEL), BF16),
        compiler_params=_params("arbitrary"),
        name="mha_ctx",
    )(q, k, v)


NAT_HALF = D_MODEL // 2
NAT_PAIRS = NAT_HALF // LANES
NAT_R = 4
NAT_WROWS = 12
N_DROW = 2 * WIN_H - 1
MASKED_TILE = N_DROW


def _natten_kernel(q_ref, k_ref, v_ref, kc_ref, vc_ref, tbl_ref, o_ref):
    r0 = pl.program_id(2) * NAT_R
    ws = jnp.clip(r0 - WIN_H // 2, 0, GRID_H - NAT_WROWS)
    row0 = pl.multiple_of(ws * GRID_W, GRID_W)
    n_loc = NAT_WROWS * GRID_W
    lo = lax.broadcasted_iota(jnp.int32, (LANES, LANES), 1) < GRID_W

    def tile_index(a, w):
        r = r0 + a
        start = jnp.clip(r - WIN_H // 2, 0, GRID_H - WIN_H)
        wr = ws + w
        valid = (wr >= start) & (wr < start + WIN_H)
        return jnp.where(valid, wr - r + (WIN_H - 1), MASKED_TILE)

    idx = [[tile_index(a, w) for w in range(NAT_WROWS)] for a in range(NAT_R)]
    for p in range(NAT_PAIRS):
        sl = slice(p * LANES, (p + 1) * LANES)
        q2 = jnp.concatenate([_split_pair(q_ref[a * GRID_W:(a + 1) * GRID_W, sl]) for a in range(NAT_R)], axis=0)
        bias = jnp.concatenate(
            [jnp.concatenate([jnp.where(lo, tbl_ref[idx[a][w], p], tbl_ref[idx[a][w + 1], p])
                              for w in range(0, NAT_WROWS, 2)], axis=1) for a in range(NAT_R)], axis=0)
        kvs = [(k_ref[pl.ds(row0, n_loc), sl], v_ref[pl.ds(row0, n_loc), sl]), (kc_ref[:, sl], vc_ref[:, sl])]
        o2 = _attend(q2, kvs, biases=[bias, None])
        for a in range(NAT_R):
            o_ref[a * GRID_W:(a + 1) * GRID_W, sl] = _merge_pair(
                o2[a * LANES:(a + 1) * LANES], GRID_W).astype(o_ref.dtype)


def _natten(q, k, v, kc, vc, tbl):
    n_half = D_MODEL // NAT_HALF
    blocks = GRID_H // NAT_R
    q_spec = pl.BlockSpec((NAT_R * GRID_W, NAT_HALF), lambda b, hh, r: (b * blocks + r, hh))
    kv_spec = pl.BlockSpec((DEC_SEQ, NAT_HALF), lambda b, hh, r: (b, hh))
    cache_spec = pl.BlockSpec((None, PAST_LEN, NAT_HALF), lambda b, hh, r: (b, 0, hh))
    tbl_spec = pl.BlockSpec((N_DROW + 1, NAT_PAIRS, LANES, LANES), lambda b, hh, r: (0, hh, 0, 0))
    return pl.pallas_call(
        _natten_kernel,
        grid=(DEC_BATCH, n_half, blocks),
        in_specs=[q_spec, kv_spec, kv_spec, cache_spec, cache_spec, tbl_spec],
        out_specs=q_spec,
        out_shape=jax.ShapeDtypeStruct((N_LAT, D_MODEL), BF16),
        compiler_params=_params("arbitrary", "arbitrary", "arbitrary"),
        name="natten",
    )(q, k, v, kc, vc, tbl)


def _bias_table_kernel(rpb_ref, o_ref):
    dr = pl.program_id(0)
    row = lax.broadcasted_iota(jnp.int32, (LANES, LANES), 0)
    col = lax.broadcasted_iota(jnp.int32, (LANES, LANES), 1)
    sub = row >= GRID_W
    qc = row % GRID_W
    kc = col % GRID_W
    dc = jnp.clip(kc - qc, -(WIN_W - 1), WIN_W - 1) + (WIN_W - 1)
    col_start = jnp.clip(qc - WIN_W // 2, 0, GRID_W - WIN_W)
    in_win = (kc >= col_start) & (kc < col_start + WIN_W) & (dr < N_DROW)
    n_dc = 2 * WIN_W - 1
    drc = jnp.minimum(dr, N_DROW - 1)
    for pr in range(N_HEADS_B // 2):
        b0 = ((2 * pr) * N_DROW + drc) * n_dc
        b1 = ((2 * pr + 1) * N_DROW + drc) * n_dc

        def body(j, acc, b0=b0, b1=b1):
            return jnp.where(dc == j, jnp.where(sub, rpb_ref[b1 + j], rpb_ref[b0 + j]), acc)

        acc = lax.fori_loop(0, n_dc, body, jnp.zeros((LANES, LANES), F32))
        o_ref[pr] = jnp.where(in_win, acc, NEG_INF)


def _bias_table(rpb):
    return pl.pallas_call(
        _bias_table_kernel,
        grid=(N_DROW + 1,),
        in_specs=[pl.BlockSpec(memory_space=pltpu.SMEM)],
        out_specs=pl.BlockSpec((None, N_HEADS_B // 2, LANES, LANES), lambda d: (d, 0, 0, 0)),
        out_shape=jax.ShapeDtypeStruct((N_DROW + 1, N_HEADS_B // 2, LANES, LANES), F32),
        compiler_params=_params("arbitrary"),
        name="natten_bias_table",
    )(rpb.reshape(-1))


def _wo_ln_kernel(a_ref, w_ref, x_ref, gate_ref, g_ref, b_ref, o_ref):
    o = jnp.dot(a_ref[...], w_ref[...], preferred_element_type=F32)
    y = ALPHA * x_ref[...] + gate_ref[...] * o
    o_ref[...] = _layer_norm(y, g_ref[...], b_ref[...])


def _wo_ln(stream, a, w, x, gate, ln_g, ln_b):
    vec = _full_spec((1, D_MODEL))
    return pl.pallas_call(
        _wo_ln_kernel,
        grid=(stream.n // TM,),
        in_specs=[_row_spec(TM, a.shape[1]), _full_spec(w.shape), _row_spec(TM, D_MODEL),
                  _mod_spec(stream, TM), vec, vec],
        out_specs=_row_spec(TM, D_MODEL),
        out_shape=jax.ShapeDtypeStruct((stream.n, D_MODEL), F32),
        compiler_params=_params("arbitrary"),
        name="wo_postnorm",
    )(a, w, x, gate, ln_g, ln_b)


def _swiglu(h, wg_ref, wu_ref, wd_ref):
    acc = None
    for lo, hi in FF_CHUNKS:
        g = jnp.dot(h, wg_ref[:, lo:hi], preferred_element_type=F32)
        u = jnp.dot(h, wu_ref[:, lo:hi], preferred_element_type=F32)
        a = (g * jax.nn.sigmoid(g) * u).astype(BF16)
        part = jnp.dot(a, wd_ref[lo:hi, :], preferred_element_type=F32)
        acc = part if acc is None else acc + part
    return acc


def _ffn_dense_kernel(x_ref, sh_ref, sc_ref, wg_ref, wu_ref, wd_ref, gate_ref, g_ref, b_ref, o_ref):
    x = x_ref[...]
    h = (x * (1.0 + sc_ref[...]) + sh_ref[...]).astype(BF16)
    y = ALPHA * x + gate_ref[...] * _swiglu(h, wg_ref, wu_ref, wd_ref)
    o_ref[...] = _layer_norm(y, g_ref[...], b_ref[...])


def _ffn_dense(stream, x, shift, scale, w_gu, w_down, gate, ln_g, ln_b):
    vec = _full_spec((1, D_MODEL))
    mod = _mod_spec(stream, TM_FFN)
    row = _row_spec(TM_FFN, D_MODEL)
    return pl.pallas_call(
        _ffn_dense_kernel,
        grid=(stream.n // TM_FFN,),
        in_specs=[row, mod, mod,
                  pl.BlockSpec((D_MODEL, D_FF), lambda i: (0, 0)),
                  pl.BlockSpec((D_MODEL, D_FF), lambda i: (0, 1)),
                  _full_spec((D_FF, D_MODEL)),
                  mod, vec, vec],
        out_specs=row,
        out_shape=jax.ShapeDtypeStruct((stream.n, D_MODEL), F32),
        compiler_params=_params("arbitrary"),
        name="ffn_dense",
    )(x, shift, scale, w_gu, w_gu, w_down, gate, ln_g, ln_b)


def _ffn_grouped_kernel(te_ref, tv_ref, x_ref, wg_ref, wu_ref, wd_ref, o_ref):
    i = pl.program_id(0)

    @pl.when(tv_ref[i] != 0)
    def _():
        o_ref[...] = _swiglu(x_ref[...].astype(BF16), wg_ref, wu_ref, wd_ref)

    @pl.when(tv_ref[i] == 0)
    def _():
        o_ref[...] = jnp.zeros_like(o_ref)


def _ffn_grouped(tile_expert, tile_valid, xs, w_gu, w_down):
    row = pl.BlockSpec((TMG, D_MODEL), lambda i, te, tv: (i, 0))
    grid_spec = pltpu.PrefetchScalarGridSpec(
        num_scalar_prefetch=2,
        grid=(NT_MOE,),
        in_specs=[row,
                  pl.BlockSpec((None, D_MODEL, D_FF), lambda i, te, tv: (te[i], 0, 0)),
                  pl.BlockSpec((None, D_MODEL, D_FF), lambda i, te, tv: (te[i], 0, 1)),
                  pl.BlockSpec((None, D_FF, D_MODEL), lambda i, te, tv: (te[i], 0, 0))],
        out_specs=row,
    )
    return pl.pallas_call(
        _ffn_grouped_kernel,
        grid_spec=grid_spec,
        out_shape=jax.ShapeDtypeStruct((NT_MOE * TMG, D_MODEL), F32),
        compiler_params=_params("arbitrary"),
        name="ffn_grouped",
    )(tile_expert, tile_valid, xs, w_gu, w_gu, w_down)


META_E0, META_E1, META_W0, META_W1, META_R0, META_R1 = range(6)


def _router_kernel(x_ref, sh_ref, sc_ref, rw_ref, cnt_in_ref, h_ref, meta_ref, cnt_ref, carry_scr):
    i = pl.program_id(0)

    @pl.when(i == 0)
    def _():
        carry_scr[...] = cnt_in_ref[...]

    h = x_ref[...] * (1.0 + sc_ref[...]) + sh_ref[...]
    h_ref[...] = h
    logits = jnp.dot(h, rw_ref[...], preferred_element_type=F32, precision=HIGHEST)
    lane = lax.broadcasted_iota(jnp.int32, logits.shape, 1).astype(F32)
    lg = jnp.where(lane < N_EXPERTS, logits, -jnp.inf)
    m1 = jnp.max(lg, axis=-1, keepdims=True)
    i1 = jnp.min(jnp.where(lg == m1, lane, float(LANES)), axis=-1, keepdims=True)
    lg2 = jnp.where(lane == i1, -jnp.inf, lg)
    m2 = jnp.max(lg2, axis=-1, keepdims=True)
    i2 = jnp.min(jnp.where(lg2 == m2, lane, float(LANES)), axis=-1, keepdims=True)
    e = jnp.exp(m2 - m1)
    w1 = 1.0 / (1.0 + e)
    w2 = e / (1.0 + e)

    sel1 = lane == i1
    sel2 = lane == i2
    onehot = jnp.where(sel1 | sel2, 1.0, 0.0)
    rr = lax.broadcasted_iota(jnp.int32, (TM, TM), 0)
    cc = lax.broadcasted_iota(jnp.int32, (TM, TM), 1)
    lower = jnp.where(cc < rr, 1.0, 0.0).astype(BF16)
    before = jnp.dot(lower, onehot.astype(BF16), preferred_element_type=F32) + carry_scr[0:1, :]
    r1 = jnp.sum(jnp.where(sel1, before, 0.0), axis=-1, keepdims=True)
    r2 = jnp.sum(jnp.where(sel2, before, 0.0), axis=-1, keepdims=True)
    carry_scr[...] = carry_scr[...] + jnp.sum(onehot, axis=0, keepdims=True)
    cnt_ref[...] = carry_scr[...]

    cols = [i1, i2, w1, w2, r1, r2]
    meta = jnp.zeros(logits.shape, F32)
    for c, val in enumerate(cols):
        meta = jnp.where(lane == c, val, meta)
    meta_ref[...] = meta


def _router(stream, x, shift, scale, rw_pad, cnt_in):
    cnt_spec = _full_spec((SUBLANES, LANES))
    return pl.pallas_call(
        _router_kernel,
        grid=(stream.n // TM,),
        in_specs=[_row_spec(TM, D_MODEL), _mod_spec(stream, TM), _mod_spec(stream, TM),
                  _full_spec((D_MODEL, LANES)), cnt_spec],
        out_specs=[_row_spec(TM, D_MODEL), _row_spec(TM, LANES), cnt_spec],
        out_shape=[jax.ShapeDtypeStruct((stream.n, D_MODEL), F32), jax.ShapeDtypeStruct((stream.n, LANES), F32),
                   jax.ShapeDtypeStruct((SUBLANES, LANES), F32)],
        scratch_shapes=[pltpu.VMEM((SUBLANES, LANES), F32)],
        compiler_params=_params("arbitrary"),
        name="moe_router",
    )(x, shift, scale, rw_pad, cnt_in)


def _row_copy(src_ref, src_row, dst_ref, dst_row, sem):
    return pltpu.make_async_copy(src_ref.at[pl.ds(src_row, 1)], dst_ref.at[pl.ds(dst_row, 1)], sem)


def _scatter_kernel(d0_ref, d1_ref, h_ref, buf_in_ref, buf_ref, sem):
    del buf_in_ref
    base = pl.program_id(0) * TM

    def issue(c, carry):
        for u in range(ROW_UNROLL):
            r = c * ROW_UNROLL + u
            _row_copy(h_ref, r, buf_ref, d0_ref[base + r], sem).start(priority=0)
            _row_copy(h_ref, r, buf_ref, d1_ref[base + r], sem).start(priority=1)
        return carry

    lax.fori_loop(0, TM // ROW_UNROLL, issue, 0)
    for _ in range(2):
        pltpu.make_async_copy(h_ref, buf_ref.at[pl.ds(0, TM)], sem).wait()


def _scatter(stream, d0, d1, h, buf):
    grid_spec = pltpu.PrefetchScalarGridSpec(
        num_scalar_prefetch=2,
        grid=(stream.n // TM,),
        in_specs=[pl.BlockSpec((TM, D_MODEL), lambda i, a, b: (i, 0)), pl.BlockSpec(memory_space=pl.ANY)],
        out_specs=pl.BlockSpec(memory_space=pl.ANY),
        scratch_shapes=[pltpu.SemaphoreType.DMA],
    )
    return pl.pallas_call(
        _scatter_kernel,
        grid_spec=grid_spec,
        out_shape=jax.ShapeDtypeStruct(buf.shape, buf.dtype),
        input_output_aliases={3: 0},
        compiler_params=_params("arbitrary"),
        name="moe_scatter",
    )(d0, d1, h, buf)


def _combine_kernel(d0_ref, d1_ref, ys_ref, meta_ref, x_ref, gate_ref, g_ref, b_ref, o_ref, rows_scr, sem):
    base = pl.program_id(0) * TM

    def issue(c, carry):
        for u in range(ROW_UNROLL):
            r = c * ROW_UNROLL + u
            _row_copy(ys_ref, d0_ref[base + r], rows_scr.at[0], r, sem).start(priority=0)
            _row_copy(ys_ref, d1_ref[base + r], rows_scr.at[1], r, sem).start(priority=1)
        return carry

    lax.fori_loop(0, TM // ROW_UNROLL, issue, 0)
    for s in range(2):
        pltpu.make_async_copy(ys_ref.at[pl.ds(0, TM)], rows_scr.at[s], sem).wait()

    meta = meta_ref[...]
    w0 = meta[:, META_W0:META_W0 + 1]
    w1 = meta[:, META_W1:META_W1 + 1]
    f = w0 * rows_scr[0] + w1 * rows_scr[1]
    y = ALPHA * x_ref[...] + gate_ref[...] * f
    o_ref[...] = _layer_norm(y, g_ref[...], b_ref[...])


def _combine(stream, d0, d1, ys, meta, x, gate, ln_g, ln_b):
    vec = pl.BlockSpec((1, D_MODEL), lambda i, a, b: (0, 0))
    grid_spec = pltpu.PrefetchScalarGridSpec(
        num_scalar_prefetch=2,
        grid=(stream.n // TM,),
        in_specs=[pl.BlockSpec(memory_space=pl.ANY), _row_spec(TM, LANES), _row_spec(TM, D_MODEL),
                  _mod_spec(stream, TM), vec, vec],
        out_specs=_row_spec(TM, D_MODEL),
        scratch_shapes=[pltpu.VMEM((2, TM, D_MODEL), F32), pltpu.SemaphoreType.DMA],
    )
    return pl.pallas_call(
        _combine_kernel,
        grid_spec=grid_spec,
        out_shape=jax.ShapeDtypeStruct((stream.n, D_MODEL), F32),
        compiler_params=_params("arbitrary"),
        name="moe_combine",
    )(d0, d1, ys, meta, x, gate, ln_g, ln_b)


def _rope_tables():
    t = jnp.arange(DEC_SEQ)
    row = (t // GRID_W).astype(F32)
    col = (t % GRID_W).astype(F32)
    freqs = ROPE_THETA ** (-jnp.arange(ROT_FREQS, dtype=F32) / ROT_FREQS)
    ar = row[:, None] * freqs
    ac = col[:, None] * freqs
    cos = jnp.concatenate([jnp.cos(ar), jnp.cos(ar), jnp.cos(ac), jnp.cos(ac)], axis=1)
    sin = jnp.concatenate([-jnp.sin(ar), jnp.sin(ar), -jnp.sin(ac), jnp.sin(ac)], axis=1)
    return cos, sin


def _routing_plan(metas, counts):
    cnt = counts[0, :N_EXPERTS].astype(jnp.int32)
    tiles_e = (cnt + TMG - 1) // TMG
    tile_end = jnp.cumsum(tiles_e)
    tile_start = tile_end - tiles_e
    offs = tile_start * TMG
    dests = []
    for meta in metas:
        pair = []
        for ecol, rcol in ((META_E0, META_R0), (META_E1, META_R1)):
            e = meta[:, ecol].astype(jnp.int32)
            d = meta[:, rcol].astype(jnp.int32)
            for k in range(N_EXPERTS):
                d = d + jnp.where(e == k, offs[k], 0)
            pair.append(d)
        dests.append(tuple(pair))
    tid = jnp.arange(NT_MOE, dtype=jnp.int32)
    te = jnp.minimum(jnp.sum((tid[:, None] >= tile_end[None, :]).astype(jnp.int32), axis=1), N_EXPERTS - 1)
    total = tile_end[-1]
    valid = tid < total
    te_last = jnp.max(jnp.where(valid, te, 0))
    te = jnp.where(valid, te, te_last)
    return dests, te, valid.astype(jnp.int32)


def kernel(x_prompt, x_sample, cache_k_a, cache_v_a, cache_k_b, cache_v_b, c, c_ctx, ada_w, ada_b, ln_attn_g, ln_attn_b, ln_ffn_g, ln_ffn_b, wqkv_a, qnorm_a, knorm_a, wo_a, wqkv_b, rpb_b, wo_b, ffn_w_gu, ffn_w_down, router_w, moe_w_gu, moe_w_down):
    streams = (CTX, LAT)
    xs = [x_prompt.reshape(N_CTX, D_MODEL), x_sample.reshape(N_LAT, D_MODEL)]

    cond = jnp.zeros((GROUP_PAD, D_MODEL), F32).at[0].set(c_ctx).at[1:N_GROUPS].set(c)
    mods = _ada_mods(cond, ada_w, ada_b)
    vec = lambda a, l: a[l].reshape(1, D_MODEL)

    m = mods[0]
    w_qkv = wqkv_a[0].astype(BF16)
    w_o = wo_a[0].astype(BF16)
    w_gu = ffn_w_gu[0].astype(BF16)
    w_dn = ffn_w_down[0].astype(BF16)
    gains = (qnorm_a[0].reshape(1, HEAD_DIM_A), knorm_a[0].reshape(1, HEAD_DIM_A))
    nq, nk = N_HEADS_A * HEAD_DIM_A, N_KV_A * HEAD_DIM_A
    qscale = HEAD_DIM_A ** -0.5

    qp, kp, vp, kp32, vp32 = _qkv(CTX, xs[0], m[0], m[1], w_qkv, nq=nq, nk=nk, qscale=qscale, gains=gains,
                                  emit_f32=True)
    new_k_a = kp32.reshape(BATCH, 1, SEQ, N_KV_A, HEAD_DIM_A)
    new_v_a = vp32.reshape(BATCH, 1, SEQ, N_KV_A, HEAD_DIM_A)
    ql, kl, vlt = _qkv(LAT, xs[1], m[0], m[1], w_qkv, nq=nq, nk=nk, qscale=qscale * LOG2_E, gains=gains,
                       rope_tables=_rope_tables(), v_transposed=True)
    cache_k = cache_k_a[:, 0].reshape(DEC_BATCH, PAST_LEN, nk).astype(BF16)
    cache_vt = jnp.swapaxes(cache_v_a[:, 0].reshape(DEC_BATCH, PAST_LEN, nk), 1, 2).astype(BF16)
    attn = [_gqa_ctx(qp, kp, vp), _gqa_lat_t(ql, kl, vlt, cache_k, cache_vt)]
    xs = [_wo_ln(s, a, w_o, x, m[2], vec(ln_attn_g, 0), vec(ln_attn_b, 0)) for s, a, x in zip(streams, attn, xs)]
    xs = [_ffn_dense(s, x, m[3], m[4], w_gu, w_dn, m[5], vec(ln_ffn_g, 0), vec(ln_ffn_b, 0))
          for s, x in zip(streams, xs)]

    m = mods[1]
    w_qkv = wqkv_b[0].astype(BF16)
    w_o = wo_b[0].astype(BF16)
    qscale = HEAD_DIM_B ** -0.5
    qp, kp, vp, kp32, vp32 = _qkv(CTX, xs[0], m[0], m[1], w_qkv, nq=D_MODEL, nk=D_MODEL, qscale=qscale,
                                  emit_f32=True)
    new_k_b = kp32.reshape(BATCH, 1, SEQ, N_HEADS_B, HEAD_DIM_B)
    new_v_b = vp32.reshape(BATCH, 1, SEQ, N_HEADS_B, HEAD_DIM_B)
    ql, kl, vl = _qkv(LAT, xs[1], m[0], m[1], w_qkv, nq=D_MODEL, nk=D_MODEL, qscale=qscale)
    attn = [_mha_ctx(qp, kp, vp),
            _natten(ql, kl, vl, cache_k_b[:, 0].reshape(DEC_BATCH, PAST_LEN, D_MODEL).astype(BF16),
                    cache_v_b[:, 0].reshape(DEC_BATCH, PAST_LEN, D_MODEL).astype(BF16), _bias_table(rpb_b[0]))]
    xs = [_wo_ln(s, a, w_o, x, m[2], vec(ln_attn_g, 1), vec(ln_attn_b, 1)) for s, a, x in zip(streams, attn, xs)]

    rw_pad = jnp.zeros((D_MODEL, LANES), F32).at[:, :N_EXPERTS].set(router_w[0])
    counts = jnp.zeros((SUBLANES, LANES), F32)
    hs, metas = [], []
    for s, x in zip(streams, xs):
        h, meta, counts = _router(s, x, m[3], m[4], rw_pad, counts)
        hs.append(h)
        metas.append(meta)
    dests, tile_expert, tile_valid = _routing_plan(metas, counts)
    sorted_rows = jnp.zeros((NT_MOE * TMG, D_MODEL), F32)
    for s, (d0, d1), h in zip(streams, dests, hs):
        sorted_rows = _scatter(s, d0, d1, h, sorted_rows)
    ys = _ffn_grouped(tile_expert, tile_valid, sorted_rows, moe_w_gu[0].astype(BF16), moe_w_down[0].astype(BF16))
    outs = [_combine(s, d0, d1, ys, meta, x, m[5], vec(ln_ffn_g, 1), vec(ln_ffn_b, 1))
            for s, (d0, d1), meta, x in zip(streams, dests, metas, xs)]

    y_prompt = outs[0].reshape(BATCH, SEQ, D_MODEL)
    y_sample = outs[1].reshape(DEC_BATCH, DEC_SEQ, D_MODEL)
    return (y_prompt, y_sample, new_k_a, new_v_a, new_k_b, new_v_b)
```

```python
import functools
import math

import jax
import jax.numpy as jnp
from jax import lax
from jax.experimental import pallas as pl
from jax.experimental.pallas import tpu as pltpu

F32 = jnp.float32
BF16 = jnp.bfloat16
HIGHEST = lax.Precision.HIGHEST

D_MODEL = 1024
BATCH, SEQ = 32, 256
DEC_BATCH, DEC_SEQ = 4, 4096
PAST_LEN = 256
DEPTH = 2
GRID_W = 64
GRID_H = DEC_SEQ // GRID_W
N_HEADS_A, N_KV_A, HEAD_DIM_A = 8, 2, 128
ROT_FREQS = HEAD_DIM_A // 4
ROPE_THETA = 10000.0
N_HEADS_B, HEAD_DIM_B = 16, 64
WIN_H, WIN_W = 8, 16
D_FF = 2816
N_EXPERTS = 8
EPS = 1e-6
NEG_INF = -1e30
ALPHA = (2.0 * DEPTH) ** 0.25
LOG2_E = math.log2(math.e)

N_CTX = BATCH * SEQ
N_LAT = DEC_BATCH * DEC_SEQ
N_GROUPS = 1 + DEC_BATCH
GROUP_PAD = 8

LANES = 128
SUBLANES = 8
VMEM_LIMIT = 56 * 2**20

TM = 512
MXU_DIM = 256
FF_CHUNKS = ((0, 1024), (1024, 2048), (2048, D_FF))
TM_FFN = 512
TMG = 512
TQ_A = 256
UNIT_A = 2
TN_ADA = 1536
ROW_UNROLL = 8
N_PAIRS = 2 * (N_CTX + N_LAT)
NT_MOE = N_PAIRS // TMG + N_EXPERTS


class Stream:
    def __init__(self, n_rows, group_offset, rows_per_group):
        self.n = n_rows
        self.goff = group_offset
        self.rpg = rows_per_group

    def group_map(self, tm):
        tiles_per_group = self.rpg // tm
        goff = self.goff
        return lambda i, *_: (goff + i // tiles_per_group, 0, 0)


CTX = Stream(N_CTX, 0, N_CTX)
LAT = Stream(N_LAT, 1, DEC_SEQ)


def _params(*sem):
    return pltpu.CompilerParams(dimension_semantics=sem, vmem_limit_bytes=VMEM_LIMIT)


def _mod_spec(stream, tm):
    return pl.BlockSpec((None, 1, D_MODEL), stream.group_map(tm))


def _row_spec(tm, width):
    return pl.BlockSpec((tm, width), lambda i, *_: (i, 0))


def _full_spec(shape):
    nd = len(shape)
    return pl.BlockSpec(shape, lambda *_: (0,) * nd)


def _layer_norm(y, g, b):
    mu = jnp.mean(y, axis=-1, keepdims=True)
    d = y - mu
    var = jnp.mean(d * d, axis=-1, keepdims=True)
    return d * lax.rsqrt(var + EPS) * g + b


def _ada_kernel(c_ref, w_ref, b_ref, o_ref):
    c = c_ref[...]
    s = c * jax.nn.sigmoid(c)
    o_ref[...] = jnp.dot(s, w_ref[...], preferred_element_type=F32, precision=HIGHEST) + b_ref[...]


def _ada_mods(cond, ada_w, ada_b):
    n_out = 6 * D_MODEL
    out = pl.pallas_call(
        _ada_kernel,
        grid=(DEPTH, n_out // TN_ADA),
        in_specs=[
            pl.BlockSpec((GROUP_PAD, D_MODEL), lambda l, n: (0, 0)),
            pl.BlockSpec((None, D_MODEL, TN_ADA), lambda l, n: (l, 0, n)),
            pl.BlockSpec((None, 1, TN_ADA), lambda l, n: (l, 0, n)),
        ],
        out_specs=pl.BlockSpec((None, GROUP_PAD, TN_ADA), lambda l, n: (l, 0, n)),
        out_shape=jax.ShapeDtypeStruct((DEPTH, GROUP_PAD, n_out), F32),
        compiler_params=_params("arbitrary", "arbitrary"),
        name="ada_mods",
    )(cond, ada_w, ada_b.reshape(DEPTH, 1, n_out))
    out = out.reshape(DEPTH, GROUP_PAD, 6, D_MODEL).transpose(0, 2, 1, 3)
    return out[:, :, :, None, :]


def _swap_halves(t):
    lane = lax.broadcasted_iota(jnp.int32, t.shape, 1)
    fwd = pltpu.roll(t, LANES - ROT_FREQS, 1)
    bwd = pltpu.roll(t, ROT_FREQS, 1)
    return jnp.where((lane % (2 * ROT_FREQS)) < ROT_FREQS, fwd, bwd)


def _qkv_kernel(*refs, nq, nk, norm, rope, emit_f32, v_layout, qscale):
    refs = list(refs)
    x_ref, sh_ref, sc_ref, w_ref = refs[:4]
    pos = 4
    if norm:
        qg_ref, kg_ref = refs[pos:pos + 2]
        pos += 2
    if rope:
        cos_ref, sin_ref = refs[pos:pos + 2]
        pos += 2
    q_ref, k_ref, v_ref = refs[pos:pos + 3]
    pos += 3
    if emit_f32:
        kf_ref, vf_ref = refs[pos:pos + 2]

    h = (x_ref[...] * (1.0 + sc_ref[...]) + sh_ref[...]).astype(BF16)
    qkv = jnp.dot(h, w_ref[...], preferred_element_type=F32)
    if norm:
        n_heads = (nq + nk) // HEAD_DIM_A
        for hd in range(n_heads):
            lo = hd * HEAD_DIM_A
            t = qkv[:, lo:lo + HEAD_DIM_A]
            ms = jnp.mean(t * t, axis=-1, keepdims=True)
            gain = qg_ref[...] if lo < nq else kg_ref[...]
            t = t * lax.rsqrt(ms + EPS) * gain
            if rope:
                t = t * cos_ref[...] + _swap_halves(t) * sin_ref[...]
            if lo < nq:
                q_ref[:, lo:lo + HEAD_DIM_A] = (t * qscale).astype(BF16)
            else:
                k_ref[:, lo - nq:lo - nq + HEAD_DIM_A] = t.astype(BF16)
                if emit_f32:
                    kf_ref[:, lo - nq:lo - nq + HEAD_DIM_A] = t
    else:
        q_ref[...] = (qkv[:, :nq] * qscale).astype(BF16)
        k = qkv[:, nq:nq + nk]
        k_ref[...] = k.astype(BF16)
        if emit_f32:
            kf_ref[...] = k
    v = qkv[:, nq + nk:]
    if v_layout == "rows":
        v_ref[...] = v.astype(BF16)
    elif v_layout == "t":
        v_ref[...] = v.T.astype(BF16)
    else:
        vt = v.T.astype(BF16)
        for j in range(vt.shape[1] // LANES):
            v_ref[j] = vt[:, j * LANES:(j + 1) * LANES]
    if emit_f32:
        vf_ref[...] = v


def _qkv(stream, x, shift, scale, w, *, nq, nk, qscale, gains=None, rope_tables=None, emit_f32=False,
         v_layout="rows"):
    n = stream.n
    nw = w.shape[1]
    norm = gains is not None
    rope = rope_tables is not None
    in_specs = [_row_spec(TM, D_MODEL), _mod_spec(stream, TM), _mod_spec(stream, TM), _full_spec((D_MODEL, nw))]
    args = [x, shift, scale, w]
    if norm:
        in_specs += [_full_spec((1, HEAD_DIM_A))] * 2
        args += list(gains)
    if rope:
        tiles_per_seq = DEC_SEQ // TM
        tbl_spec = pl.BlockSpec((TM, HEAD_DIM_A), lambda i: (i % tiles_per_seq, 0))
        in_specs += [tbl_spec, tbl_spec]
        args += list(rope_tables)
    out_specs = [_row_spec(TM, nq), _row_spec(TM, nk), _row_spec(TM, nk)]
    out_shape = [jax.ShapeDtypeStruct((n, nq), BF16), jax.ShapeDtypeStruct((n, nk), BF16),
                 jax.ShapeDtypeStruct((n, nk), BF16)]
    if v_layout == "t":
        out_specs[2] = pl.BlockSpec((nk, TM), lambda i: (0, i))
        out_shape[2] = jax.ShapeDtypeStruct((nk, n), BF16)
    elif v_layout == "t_blocked":
        out_specs[2] = pl.BlockSpec((TM // LANES, nk, LANES), lambda i: (i, 0, 0))
        out_shape[2] = jax.ShapeDtypeStruct((n // LANES, nk, LANES), BF16)
    if emit_f32:
        out_specs += [_row_spec(TM, nk), _row_spec(TM, nk)]
        out_shape += [jax.ShapeDtypeStruct((n, nk), F32)] * 2
    return pl.pallas_call(
        functools.partial(_qkv_kernel, nq=nq, nk=nk, norm=norm, rope=rope, emit_f32=emit_f32,
                          v_layout=v_layout, qscale=qscale),
        grid=(n // TM,),
        in_specs=in_specs,
        out_specs=out_specs,
        out_shape=out_shape,
        compiler_params=_params("arbitrary"),
        name="qkv_norm_rope" if norm else "qkv",
    )(*args)


def _attend(q, kvs, biases=None):
    scores = []
    for idx, (k, _) in enumerate(kvs):
        s = lax.dot_general(q, k, (((1,), (1,)), ((), ())), preferred_element_type=F32)
        if biases is not None and biases[idx] is not None:
            s = s + biases[idx]
        scores.append(s)
    m = jnp.max(scores[0], axis=-1, keepdims=True)
    for s in scores[1:]:
        m = jnp.maximum(m, jnp.max(s, axis=-1, keepdims=True))
    denom = None
    out = None
    for s, (_, v) in zip(scores, kvs):
        p = jnp.exp(s - m)
        part = jnp.sum(p, axis=-1, keepdims=True)
        pv = jnp.dot(p.astype(BF16), v, preferred_element_type=F32)
        denom = part if denom is None else denom + part
        out = pv if out is None else out + pv
    return out * (1.0 / denom)


def _gqa_ctx_kernel(q_ref, k_ref, v_ref, o_ref):
    d = HEAD_DIM_A
    group = N_HEADS_A // N_KV_A
    for g in range(N_KV_A):
        heads = [g * group + j for j in range(group)]
        qs = jnp.concatenate([q_ref[:, h * d:(h + 1) * d] for h in heads], axis=0)
        o = _attend(qs, [(k_ref[:, g * d:(g + 1) * d], v_ref[:, g * d:(g + 1) * d])])
        for j, h in enumerate(heads):
            o_ref[:, h * d:(h + 1) * d] = o[j * SEQ:(j + 1) * SEQ].astype(o_ref.dtype)


def _gqa_ctx(q, k, v):
    nq, nk = N_HEADS_A * HEAD_DIM_A, N_KV_A * HEAD_DIM_A
    return pl.pallas_call(
        _gqa_ctx_kernel,
        grid=(BATCH,),
        in_specs=[_row_spec(SEQ, nq), _row_spec(SEQ, nk), _row_spec(SEQ, nk)],
        out_specs=_row_spec(SEQ, nq),
        out_shape=jax.ShapeDtypeStruct((N_CTX, nq), BF16),
        compiler_params=_params("arbitrary"),
        name="gqa_ctx",
    )(q, k, v)


ONES_ROWS = 16


def _scores_t(q, ks):
    return [lax.dot_general(k, q, (((1,), (1,)), ((), ())), preferred_element_type=F32) for k in ks]


def _softmax_pv_t(scores, vts):
    m = jnp.max(scores[0], axis=0, keepdims=True)
    for s in scores[1:]:
        m = jnp.maximum(m, jnp.max(s, axis=0, keepdims=True))
    acc = None
    for s, vt in zip(scores, vts):
        p = jnp.exp2(s - m).astype(BF16)
        vt_ones = jnp.concatenate([vt, jnp.ones((ONES_ROWS, vt.shape[1]), BF16)], axis=0)
        pv = jnp.dot(vt_ones, p, preferred_element_type=F32)
        acc = pv if acc is None else acc + pv
    d = HEAD_DIM_A
    return (acc[:d] * (1.0 / acc[d:d + 1])).T


def _gqa_t_kernel(q_ref, k_ref, vt_ref, kc_ref, vct_ref, o_ref, *, n_kv, group, unit, tq):
    d = HEAD_DIM_A
    units = [(g, [g * group + u0 + j for j in range(unit)]) for g in range(n_kv) for u0 in range(0, group, unit)]
    def unit_scores(g, heads):
        ds = slice(g * d, (g + 1) * d)
        qs = jnp.concatenate([q_ref[:, h * d:(h + 1) * d] for h in heads], axis=0)
        return _scores_t(qs, [k_ref[:, ds], kc_ref[:, ds]])

    nxt = unit_scores(*units[0])
    for idx, (g, heads) in enumerate(units):
        sc = nxt
        if idx + 1 < len(units):
            nxt = unit_scores(*units[idx + 1])
        ds = slice(g * d, (g + 1) * d)
        o = _softmax_pv_t(sc, [vt_ref[ds, :], vct_ref[ds, :]])
        for j, h in enumerate(heads):
            o_ref[:, h * d:(h + 1) * d] = o[j * tq:(j + 1) * tq].astype(o_ref.dtype)


def _gqa_lat_t(q, k, vt, kc, vct):
    nq, nk = N_HEADS_A * HEAD_DIM_A, N_KV_A * HEAD_DIM_A
    tiles = DEC_SEQ // TQ_A
    q_spec = pl.BlockSpec((TQ_A, nq), lambda b, t: (b * tiles + t, 0))
    return pl.pallas_call(
        functools.partial(_gqa_t_kernel, n_kv=N_KV_A, group=N_HEADS_A // N_KV_A, unit=UNIT_A, tq=TQ_A),
        grid=(DEC_BATCH, tiles),
        in_specs=[q_spec,
                  pl.BlockSpec((DEC_SEQ, nk), lambda b, t: (b, 0)),
                  pl.BlockSpec((nk, DEC_SEQ), lambda b, t: (0, b)),
                  pl.BlockSpec((None, PAST_LEN, nk), lambda b, t: (b, 0, 0)),
                  pl.BlockSpec((None, nk, PAST_LEN), lambda b, t: (b, 0, 0))],
        out_specs=q_spec,
        out_shape=jax.ShapeDtypeStruct((N_LAT, nq), BF16),
        compiler_params=_params("arbitrary", "arbitrary"),
        name="gqa_lat",
    )(q, k, vt, kc, vct)


def _split_pair(qp):
    lo = lax.broadcasted_iota(jnp.int32, qp.shape, 1) < HEAD_DIM_B
    zero = jnp.zeros_like(qp)
    return jnp.concatenate([jnp.where(lo, qp, zero), jnp.where(lo, zero, qp)], axis=0)


def _merge_pair(o2, rows):
    lo = lax.broadcasted_iota(jnp.int32, (rows, LANES), 1) < HEAD_DIM_B
    return jnp.where(lo, o2[:rows], o2[rows:])


def _mha_ctx_kernel(q_ref, k_ref, v_ref, o_ref):
    for p in range(N_HEADS_B // 2):
        sl = slice(p * LANES, (p + 1) * LANES)
        o2 = _attend(_split_pair(q_ref[:, sl]), [(k_ref[:, sl], v_ref[:, sl])])
        o_ref[:, sl] = _merge_pair(o2, SEQ).astype(o_ref.dtype)


def _mha_ctx(q, k, v):
    return pl.pallas_call(
        _mha_ctx_kernel,
        grid=(BATCH,),
        in_specs=[_row_spec(SEQ, D_MODEL)] * 3,
        out_specs=_row_spec(SEQ, D_MODEL),
        out_shape=jax.ShapeDtypeStruct((N_CTX, D_MODEL), BF16),
        compiler_params=_params("arbitrary"),
        name="mha_ctx",
    )(q, k, v)


NAT_HALF = D_MODEL // 2
NAT_PAIRS = NAT_HALF // LANES
NAT_R = 4
NAT_WROWS = 12
N_DROW = 2 * WIN_H - 1
MASKED_TILE = N_DROW


def _natten_kernel(q_ref, k_ref, vt_ref, kc_ref, vct_ref, tbl_ref, o_ref):
    r0 = pl.program_id(2) * NAT_R
    ws = jnp.clip(r0 - WIN_H // 2, 0, GRID_H - NAT_WROWS)
    row0 = pl.multiple_of(ws * GRID_W, LANES)
    blk0 = ws * GRID_W // LANES
    n_loc = NAT_WROWS * GRID_W

    def tile_index(a, w):
        r = r0 + a
        start = jnp.clip(r - WIN_H // 2, 0, GRID_H - WIN_H)
        wr = ws + w
        valid = (wr >= start) & (wr < start + WIN_H)
        return jnp.where(valid, wr - r + (WIN_H - 1), MASKED_TILE)

    idx = [[tile_index(a, w) for w in range(NAT_WROWS)] for a in range(NAT_R)]

    def pair_scores(p):
        sl = slice(p * LANES, (p + 1) * LANES)
        q2 = jnp.concatenate([_split_pair(q_ref[a * GRID_W:(a + 1) * GRID_W, sl]) for a in range(NAT_R)], axis=0)
        bias_t = jnp.concatenate(
            [jnp.concatenate([tbl_ref[idx[a][w], p] for w in range(NAT_WROWS)], axis=0) for a in range(NAT_R)],
            axis=1)
        s_loc, s_ctx = _scores_t(q2, [k_ref[pl.ds(row0, n_loc), sl], kc_ref[:, sl]])
        return [s_loc + bias_t, s_ctx]

    nxt = pair_scores(0)
    for p in range(NAT_PAIRS):
        sl = slice(p * LANES, (p + 1) * LANES)
        sc = nxt
        if p + 1 < NAT_PAIRS:
            nxt = pair_scores(p + 1)
        vt_win = jnp.concatenate([vt_ref[blk0 + j, sl, :] for j in range(n_loc // LANES)], axis=1)
        o2 = _softmax_pv_t(sc, [vt_win, vct_ref[sl, :]])
        for a in range(NAT_R):
            o_ref[a * GRID_W:(a + 1) * GRID_W, sl] = _merge_pair(
                o2[a * LANES:(a + 1) * LANES], GRID_W).astype(o_ref.dtype)


def _natten(q, k, vt, kc, vct, tbl):
    n_half = D_MODEL // NAT_HALF
    blocks = GRID_H // NAT_R
    seq_blocks = DEC_SEQ // LANES
    q_spec = pl.BlockSpec((NAT_R * GRID_W, NAT_HALF), lambda b, hh, r: (b * blocks + r, hh))
    return pl.pallas_call(
        _natten_kernel,
        grid=(DEC_BATCH, n_half, blocks),
        in_specs=[q_spec,
                  pl.BlockSpec((DEC_SEQ, NAT_HALF), lambda b, hh, r: (b, hh)),
                  pl.BlockSpec((seq_blocks, NAT_HALF, LANES), lambda b, hh, r: (b, hh, 0)),
                  pl.BlockSpec((None, PAST_LEN, NAT_HALF), lambda b, hh, r: (b, 0, hh)),
                  pl.BlockSpec((None, NAT_HALF, PAST_LEN), lambda b, hh, r: (b, hh, 0)),
                  pl.BlockSpec((N_DROW + 1, NAT_PAIRS, GRID_W, LANES), lambda b, hh, r: (0, hh, 0, 0))],
        out_specs=q_spec,
        out_shape=jax.ShapeDtypeStruct((N_LAT, D_MODEL), BF16),
        compiler_params=_params("arbitrary", "arbitrary", "arbitrary"),
        name="natten",
    )(q, k, vt, kc, vct, tbl)


def _bias_table_kernel(rpb_ref, o_ref):
    dr = pl.program_id(0)
    row = lax.broadcasted_iota(jnp.int32, (GRID_W, LANES), 0)
    col = lax.broadcasted_iota(jnp.int32, (GRID_W, LANES), 1)
    sub = col >= GRID_W
    qc = col % GRID_W
    kc = row
    dc = jnp.clip(kc - qc, -(WIN_W - 1), WIN_W - 1) + (WIN_W - 1)
    col_start = jnp.clip(qc - WIN_W // 2, 0, GRID_W - WIN_W)
    in_win = (kc >= col_start) & (kc < col_start + WIN_W) & (dr < N_DROW)
    n_dc = 2 * WIN_W - 1
    drc = jnp.minimum(dr, N_DROW - 1)
    for pr in range(N_HEADS_B // 2):
        b0 = ((2 * pr) * N_DROW + drc) * n_dc
        b1 = ((2 * pr + 1) * N_DROW + drc) * n_dc

        def body(j, acc, b0=b0, b1=b1):
            return jnp.where(dc == j, jnp.where(sub, rpb_ref[b1 + j], rpb_ref[b0 + j]), acc)

        acc = lax.fori_loop(0, n_dc, body, jnp.zeros((GRID_W, LANES), F32))
        o_ref[pr] = jnp.where(in_win, acc * LOG2_E, NEG_INF)


def _bias_table(rpb):
    return pl.pallas_call(
        _bias_table_kernel,
        grid=(N_DROW + 1,),
        in_specs=[pl.BlockSpec(memory_space=pltpu.SMEM)],
        out_specs=pl.BlockSpec((None, N_HEADS_B // 2, GRID_W, LANES), lambda d: (d, 0, 0, 0)),
        out_shape=jax.ShapeDtypeStruct((N_DROW + 1, N_HEADS_B // 2, GRID_W, LANES), F32),
        compiler_params=_params("arbitrary"),
        name="natten_bias_table",
    )(rpb.reshape(-1))


def _wo_ln_kernel(a_ref, w_ref, x_ref, gate_ref, g_ref, b_ref, o_ref):
    o = jnp.dot(a_ref[...], w_ref[...], preferred_element_type=F32)
    y = ALPHA * x_ref[...] + gate_ref[...] * o
    o_ref[...] = _layer_norm(y, g_ref[...], b_ref[...])


def _wo_ln(stream, a, w, x, gate, ln_g, ln_b):
    vec = _full_spec((1, D_MODEL))
    return pl.pallas_call(
        _wo_ln_kernel,
        grid=(stream.n // TM,),
        in_specs=[_row_spec(TM, a.shape[1]), _full_spec(w.shape), _row_spec(TM, D_MODEL),
                  _mod_spec(stream, TM), vec, vec],
        out_specs=_row_spec(TM, D_MODEL),
        out_shape=jax.ShapeDtypeStruct((stream.n, D_MODEL), F32),
        compiler_params=_params("arbitrary"),
        name="wo_postnorm",
    )(a, w, x, gate, ln_g, ln_b)


def _swiglu(h, wg_ref, wu_ref, wd_ref):
    acc = None
    for lo, hi in FF_CHUNKS:
        g = jnp.dot(h, wg_ref[:, lo:hi], preferred_element_type=F32)
        u = jnp.dot(h, wu_ref[:, lo:hi], preferred_element_type=F32)
        a = (g * jax.nn.sigmoid(g) * u).astype(BF16)
        part = jnp.dot(a, wd_ref[lo:hi, :], preferred_element_type=F32)
        acc = part if acc is None else acc + part
    return acc


def _ffn_dense_kernel(x_ref, sh_ref, sc_ref, wg_ref, wu_ref, wd_ref, gate_ref, g_ref, b_ref, o_ref):
    x = x_ref[...]
    h = (x * (1.0 + sc_ref[...]) + sh_ref[...]).astype(BF16)
    y = ALPHA * x + gate_ref[...] * _swiglu(h, wg_ref, wu_ref, wd_ref)
    o_ref[...] = _layer_norm(y, g_ref[...], b_ref[...])


def _ffn_dense(stream, x, shift, scale, w_gu, w_down, gate, ln_g, ln_b):
    vec = _full_spec((1, D_MODEL))
    mod = _mod_spec(stream, TM_FFN)
    row = _row_spec(TM_FFN, D_MODEL)
    return pl.pallas_call(
        _ffn_dense_kernel,
        grid=(stream.n // TM_FFN,),
        in_specs=[row, mod, mod,
                  pl.BlockSpec((D_MODEL, D_FF), lambda i: (0, 0)),
                  pl.BlockSpec((D_MODEL, D_FF), lambda i: (0, 1)),
                  _full_spec((D_FF, D_MODEL)),
                  mod, vec, vec],
        out_specs=row,
        out_shape=jax.ShapeDtypeStruct((stream.n, D_MODEL), F32),
        compiler_params=_params("arbitrary"),
        name="ffn_dense",
    )(x, shift, scale, w_gu, w_gu, w_down, gate, ln_g, ln_b)


def _ffn_grouped_kernel(te_ref, tv_ref, x_ref, wg_ref, wu_ref, wd_ref, o_ref):
    i = pl.program_id(0)

    @pl.when(tv_ref[i] != 0)
    def _():
        o_ref[...] = _swiglu(x_ref[...].astype(BF16), wg_ref, wu_ref, wd_ref)

    @pl.when(tv_ref[i] == 0)
    def _():
        o_ref[...] = jnp.zeros_like(o_ref)


def _ffn_grouped(tile_expert, tile_valid, xs, w_gu, w_down):
    row = pl.BlockSpec((TMG, D_MODEL), lambda i, te, tv: (i, 0))
    grid_spec = pltpu.PrefetchScalarGridSpec(
        num_scalar_prefetch=2,
        grid=(NT_MOE,),
        in_specs=[row,
                  pl.BlockSpec((None, D_MODEL, D_FF), lambda i, te, tv: (te[i], 0, 0)),
                  pl.BlockSpec((None, D_MODEL, D_FF), lambda i, te, tv: (te[i], 0, 1)),
                  pl.BlockSpec((None, D_FF, D_MODEL), lambda i, te, tv: (te[i], 0, 0))],
        out_specs=row,
    )
    return pl.pallas_call(
        _ffn_grouped_kernel,
        grid_spec=grid_spec,
        out_shape=jax.ShapeDtypeStruct((NT_MOE * TMG, D_MODEL), F32),
        compiler_params=_params("arbitrary"),
        name="ffn_grouped",
    )(tile_expert, tile_valid, xs, w_gu, w_gu, w_down)


META_E0, META_E1, META_W0, META_W1, META_R0, META_R1 = range(6)


def _router_kernel(x_ref, sh_ref, sc_ref, rw_ref, cnt_in_ref, h_ref, meta_ref, cnt_ref, carry_scr):
    i = pl.program_id(0)

    @pl.when(i == 0)
    def _():
        carry_scr[...] = cnt_in_ref[...]

    h = x_ref[...] * (1.0 + sc_ref[...]) + sh_ref[...]
    h_ref[...] = h
    logits = jnp.dot(h, rw_ref[...], preferred_element_type=F32, precision=HIGHEST)
    lane = lax.broadcasted_iota(jnp.int32, logits.shape, 1).astype(F32)
    lg = jnp.where(lane < N_EXPERTS, logits, -jnp.inf)
    m1 = jnp.max(lg, axis=-1, keepdims=True)
    i1 = jnp.min(jnp.where(lg == m1, lane, float(LANES)), axis=-1, keepdims=True)
    lg2 = jnp.where(lane == i1, -jnp.inf, lg)
    m2 = jnp.max(lg2, axis=-1, keepdims=True)
    i2 = jnp.min(jnp.where(lg2 == m2, lane, float(LANES)), axis=-1, keepdims=True)
    e = jnp.exp(m2 - m1)
    w1 = 1.0 / (1.0 + e)
    w2 = e / (1.0 + e)

    sel1 = lane == i1
    sel2 = lane == i2
    onehot = jnp.where(sel1 | sel2, 1.0, 0.0)
    rr = lax.broadcasted_iota(jnp.int32, (TM, TM), 0)
    cc = lax.broadcasted_iota(jnp.int32, (TM, TM), 1)
    lower = jnp.where(cc < rr, 1.0, 0.0).astype(BF16)
    before = jnp.dot(lower, onehot.astype(BF16), preferred_element_type=F32) + carry_scr[0:1, :]
    r1 = jnp.sum(jnp.where(sel1, before, 0.0), axis=-1, keepdims=True)
    r2 = jnp.sum(jnp.where(sel2, before, 0.0), axis=-1, keepdims=True)
    carry_scr[...] = carry_scr[...] + jnp.sum(onehot, axis=0, keepdims=True)
    cnt_ref[...] = carry_scr[...]

    cols = [i1, i2, w1, w2, r1, r2]
    meta = jnp.zeros(logits.shape, F32)
    for c, val in enumerate(cols):
        meta = jnp.where(lane == c, val, meta)
    meta_ref[...] = meta


def _router(stream, x, shift, scale, rw_pad, cnt_in):
    cnt_spec = _full_spec((SUBLANES, LANES))
    return pl.pallas_call(
        _router_kernel,
        grid=(stream.n // TM,),
        in_specs=[_row_spec(TM, D_MODEL), _mod_spec(stream, TM), _mod_spec(stream, TM),
                  _full_spec((D_MODEL, LANES)), cnt_spec],
        out_specs=[_row_spec(TM, D_MODEL), _row_spec(TM, LANES), cnt_spec],
        out_shape=[jax.ShapeDtypeStruct((stream.n, D_MODEL), F32), jax.ShapeDtypeStruct((stream.n, LANES), F32),
                   jax.ShapeDtypeStruct((SUBLANES, LANES), F32)],
        scratch_shapes=[pltpu.VMEM((SUBLANES, LANES), F32)],
        compiler_params=_params("arbitrary"),
        name="moe_router",
    )(x, shift, scale, rw_pad, cnt_in)


def _row_copy(src_ref, src_row, dst_ref, dst_row, sem):
    return pltpu.make_async_copy(src_ref.at[pl.ds(src_row, 1)], dst_ref.at[pl.ds(dst_row, 1)], sem)


def _scatter_kernel(d0_ref, d1_ref, h_ref, buf_in_ref, buf_ref, sem):
    del buf_in_ref
    base = pl.program_id(0) * TM

    def issue(c, carry):
        for u in range(ROW_UNROLL):
            r = c * ROW_UNROLL + u
            _row_copy(h_ref, r, buf_ref, d0_ref[base + r], sem).start(priority=0)
            _row_copy(h_ref, r, buf_ref, d1_ref[base + r], sem).start(priority=1)
        return carry

    lax.fori_loop(0, TM // ROW_UNROLL, issue, 0)
    for _ in range(2):
        pltpu.make_async_copy(h_ref, buf_ref.at[pl.ds(0, TM)], sem).wait()


def _scatter(stream, d0, d1, h, buf):
    grid_spec = pltpu.PrefetchScalarGridSpec(
        num_scalar_prefetch=2,
        grid=(stream.n // TM,),
        in_specs=[pl.BlockSpec((TM, D_MODEL), lambda i, a, b: (i, 0)), pl.BlockSpec(memory_space=pl.ANY)],
        out_specs=pl.BlockSpec(memory_space=pl.ANY),
        scratch_shapes=[pltpu.SemaphoreType.DMA],
    )
    return pl.pallas_call(
        _scatter_kernel,
        grid_spec=grid_spec,
        out_shape=jax.ShapeDtypeStruct(buf.shape, buf.dtype),
        input_output_aliases={3: 0},
        compiler_params=_params("arbitrary"),
        name="moe_scatter",
    )(d0, d1, h, buf)


def _combine_kernel(d0_ref, d1_ref, ys_ref, meta_ref, x_ref, gate_ref, g_ref, b_ref, o_ref, rows_scr, sem):
    base = pl.program_id(0) * TM

    def issue(c, carry):
        for u in range(ROW_UNROLL):
            r = c * ROW_UNROLL + u
            _row_copy(ys_ref, d0_ref[base + r], rows_scr.at[0], r, sem).start(priority=0)
            _row_copy(ys_ref, d1_ref[base + r], rows_scr.at[1], r, sem).start(priority=1)
        return carry

    lax.fori_loop(0, TM // ROW_UNROLL, issue, 0)
    for s in range(2):
        pltpu.make_async_copy(ys_ref.at[pl.ds(0, TM)], rows_scr.at[s], sem).wait()

    meta = meta_ref[...]
    w0 = meta[:, META_W0:META_W0 + 1]
    w1 = meta[:, META_W1:META_W1 + 1]
    f = w0 * rows_scr[0] + w1 * rows_scr[1]
    y = ALPHA * x_ref[...] + gate_ref[...] * f
    o_ref[...] = _layer_norm(y, g_ref[...], b_ref[...])


def _combine(stream, d0, d1, ys, meta, x, gate, ln_g, ln_b):
    vec = pl.BlockSpec((1, D_MODEL), lambda i, a, b: (0, 0))
    grid_spec = pltpu.PrefetchScalarGridSpec(
        num_scalar_prefetch=2,
        grid=(stream.n // TM,),
        in_specs=[pl.BlockSpec(memory_space=pl.ANY), _row_spec(TM, LANES), _row_spec(TM, D_MODEL),
                  _mod_spec(stream, TM), vec, vec],
        out_specs=_row_spec(TM, D_MODEL),
        scratch_shapes=[pltpu.VMEM((2, TM, D_MODEL), F32), pltpu.SemaphoreType.DMA],
    )
    return pl.pallas_call(
        _combine_kernel,
        grid_spec=grid_spec,
        out_shape=jax.ShapeDtypeStruct((stream.n, D_MODEL), F32),
        compiler_params=_params("arbitrary"),
        name="moe_combine",
    )(d0, d1, ys, meta, x, gate, ln_g, ln_b)


def _rope_tables():
    t = jnp.arange(DEC_SEQ)
    row = (t // GRID_W).astype(F32)
    col = (t % GRID_W).astype(F32)
    freqs = ROPE_THETA ** (-jnp.arange(ROT_FREQS, dtype=F32) / ROT_FREQS)
    ar = row[:, None] * freqs
    ac = col[:, None] * freqs
    cos = jnp.concatenate([jnp.cos(ar), jnp.cos(ar), jnp.cos(ac), jnp.cos(ac)], axis=1)
    sin = jnp.concatenate([-jnp.sin(ar), jnp.sin(ar), -jnp.sin(ac), jnp.sin(ac)], axis=1)
    return cos, sin


def _routing_plan(metas, counts):
    cnt = counts[0, :N_EXPERTS].astype(jnp.int32)
    tiles_e = (cnt + TMG - 1) // TMG
    tile_end = jnp.cumsum(tiles_e)
    tile_start = tile_end - tiles_e
    offs = tile_start * TMG
    dests = []
    for meta in metas:
        pair = []
        for ecol, rcol in ((META_E0, META_R0), (META_E1, META_R1)):
            e = meta[:, ecol].astype(jnp.int32)
            d = meta[:, rcol].astype(jnp.int32)
            for k in range(N_EXPERTS):
                d = d + jnp.where(e == k, offs[k], 0)
            pair.append(d)
        dests.append(tuple(pair))
    tid = jnp.arange(NT_MOE, dtype=jnp.int32)
    te = jnp.minimum(jnp.sum((tid[:, None] >= tile_end[None, :]).astype(jnp.int32), axis=1), N_EXPERTS - 1)
    total = tile_end[-1]
    valid = tid < total
    te_last = jnp.max(jnp.where(valid, te, 0))
    te = jnp.where(valid, te, te_last)
    return dests, te, valid.astype(jnp.int32)


def kernel(x_prompt, x_sample, cache_k_a, cache_v_a, cache_k_b, cache_v_b, c, c_ctx, ada_w, ada_b, ln_attn_g, ln_attn_b, ln_ffn_g, ln_ffn_b, wqkv_a, qnorm_a, knorm_a, wo_a, wqkv_b, rpb_b, wo_b, ffn_w_gu, ffn_w_down, router_w, moe_w_gu, moe_w_down):
    streams = (CTX, LAT)
    xs = [x_prompt.reshape(N_CTX, D_MODEL), x_sample.reshape(N_LAT, D_MODEL)]

    cond = jnp.zeros((GROUP_PAD, D_MODEL), F32).at[0].set(c_ctx).at[1:N_GROUPS].set(c)
    mods = _ada_mods(cond, ada_w, ada_b)
    vec = lambda a, l: a[l].reshape(1, D_MODEL)

    m = mods[0]
    w_qkv = wqkv_a[0].astype(BF16)
    w_o = wo_a[0].astype(BF16)
    w_gu = ffn_w_gu[0].astype(BF16)
    w_dn = ffn_w_down[0].astype(BF16)
    gains = (qnorm_a[0].reshape(1, HEAD_DIM_A), knorm_a[0].reshape(1, HEAD_DIM_A))
    nq, nk = N_HEADS_A * HEAD_DIM_A, N_KV_A * HEAD_DIM_A
    qscale = HEAD_DIM_A ** -0.5

    qp, kp, vp, kp32, vp32 = _qkv(CTX, xs[0], m[0], m[1], w_qkv, nq=nq, nk=nk, qscale=qscale, gains=gains,
                                  emit_f32=True)
    new_k_a = kp32.reshape(BATCH, 1, SEQ, N_KV_A, HEAD_DIM_A)
    new_v_a = vp32.reshape(BATCH, 1, SEQ, N_KV_A, HEAD_DIM_A)
    ql, kl, vlt = _qkv(LAT, xs[1], m[0], m[1], w_qkv, nq=nq, nk=nk, qscale=qscale * LOG2_E, gains=gains,
                       rope_tables=_rope_tables(), v_layout="t")
    cache_k = cache_k_a[:, 0].reshape(DEC_BATCH, PAST_LEN, nk).astype(BF16)
    cache_vt = jnp.swapaxes(cache_v_a[:, 0].reshape(DEC_BATCH, PAST_LEN, nk), 1, 2).astype(BF16)
    attn = [_gqa_ctx(qp, kp, vp), _gqa_lat_t(ql, kl, vlt, cache_k, cache_vt)]
    xs = [_wo_ln(s, a, w_o, x, m[2], vec(ln_attn_g, 0), vec(ln_attn_b, 0)) for s, a, x in zip(streams, attn, xs)]
    xs = [_ffn_dense(s, x, m[3], m[4], w_gu, w_dn, m[5], vec(ln_ffn_g, 0), vec(ln_ffn_b, 0))
          for s, x in zip(streams, xs)]

    m = mods[1]
    w_qkv = wqkv_b[0].astype(BF16)
    w_o = wo_b[0].astype(BF16)
    qscale = HEAD_DIM_B ** -0.5
    qp, kp, vp, kp32, vp32 = _qkv(CTX, xs[0], m[0], m[1], w_qkv, nq=D_MODEL, nk=D_MODEL, qscale=qscale,
                                  emit_f32=True)
    new_k_b = kp32.reshape(BATCH, 1, SEQ, N_HEADS_B, HEAD_DIM_B)
    new_v_b = vp32.reshape(BATCH, 1, SEQ, N_HEADS_B, HEAD_DIM_B)
    ql, kl, vlt = _qkv(LAT, xs[1], m[0], m[1], w_qkv, nq=D_MODEL, nk=D_MODEL, qscale=qscale * LOG2_E,
                       v_layout="t_blocked")
    cache_k = cache_k_b[:, 0].reshape(DEC_BATCH, PAST_LEN, D_MODEL).astype(BF16)
    cache_vt = jnp.swapaxes(cache_v_b[:, 0].reshape(DEC_BATCH, PAST_LEN, D_MODEL), 1, 2).astype(BF16)
    attn = [_mha_ctx(qp, kp, vp), _natten(ql, kl, vlt, cache_k, cache_vt, _bias_table(rpb_b[0]))]
    xs = [_wo_ln(s, a, w_o, x, m[2], vec(ln_attn_g, 1), vec(ln_attn_b, 1)) for s, a, x in zip(streams, attn, xs)]

    rw_pad = jnp.zeros((D_MODEL, LANES), F32).at[:, :N_EXPERTS].set(router_w[0])
    counts = jnp.zeros((SUBLANES, LANES), F32)
    hs, metas = [], []
    for s, x in zip(streams, xs):
        h, meta, counts = _router(s, x, m[3], m[4], rw_pad, counts)
        hs.append(h)
        metas.append(meta)
    dests, tile_expert, tile_valid = _routing_plan(metas, counts)
    sorted_rows = jnp.zeros((NT_MOE * TMG, D_MODEL), F32)
    for s, (d0, d1), h in zip(streams, dests, hs):
        sorted_rows = _scatter(s, d0, d1, h, sorted_rows)
    ys = _ffn_grouped(tile_expert, tile_valid, sorted_rows, moe_w_gu[0].astype(BF16), moe_w_down[0].astype(BF16))
    outs = [_combine(s, d0, d1, ys, meta, x, m[5], vec(ln_ffn_g, 1), vec(ln_ffn_b, 1))
            for s, (d0, d1), meta, x in zip(streams, dests, metas, xs)]

    y_prompt = outs[0].reshape(BATCH, SEQ, D_MODEL)
    y_sample = outs[1].reshape(DEC_BATCH, DEC_SEQ, D_MODEL)
    return (y_prompt, y_sample, new_k_a, new_v_a, new_k_b, new_v_b)
```

```python
import functools
import math

import jax
import jax.numpy as jnp
from jax import lax
from jax.experimental import pallas as pl
from jax.experimental.pallas import tpu as pltpu

F32 = jnp.float32
BF16 = jnp.bfloat16
HIGHEST = lax.Precision.HIGHEST

D_MODEL = 1024
BATCH, SEQ = 32, 256
DEC_BATCH, DEC_SEQ = 4, 4096
PAST_LEN = 256
DEPTH = 2
GRID_W = 64
GRID_H = DEC_SEQ // GRID_W
N_HEADS_A, N_KV_A, HEAD_DIM_A = 8, 2, 128
ROT_FREQS = HEAD_DIM_A // 4
ROPE_THETA = 10000.0
N_HEADS_B, HEAD_DIM_B = 16, 64
WIN_H, WIN_W = 8, 16
D_FF = 2816
N_EXPERTS = 8
EPS = 1e-6
NEG_INF = -1e30
ALPHA = (2.0 * DEPTH) ** 0.25
LOG2_E = math.log2(math.e)

N_CTX = BATCH * SEQ
N_LAT = DEC_BATCH * DEC_SEQ
N_GROUPS = 1 + DEC_BATCH
GROUP_PAD = 8

LANES = 128
SUBLANES = 8
VMEM_LIMIT = 56 * 2**20

TM = 512
MXU_DIM = 256
FF_CHUNKS = ((0, 1024), (1024, 2048), (2048, D_FF))
TM_FFN = 512
TMG = 512
TQ_A = 256
UNIT_A = 2
TN_ADA = 1536
N_PAIRS = 2 * (N_CTX + N_LAT)
NT_MOE = N_PAIRS // TMG + N_EXPERTS
N_PAD_ROWS = NT_MOE * TMG - N_PAIRS


class Stream:
    def __init__(self, n_rows, group_offset, rows_per_group):
        self.n = n_rows
        self.goff = group_offset
        self.rpg = rows_per_group

    def group_map(self, tm):
        tiles_per_group = self.rpg // tm
        goff = self.goff
        return lambda i, *_: (goff + i // tiles_per_group, 0, 0)


CTX = Stream(N_CTX, 0, N_CTX)
LAT = Stream(N_LAT, 1, DEC_SEQ)


def _params(*sem):
    return pltpu.CompilerParams(dimension_semantics=sem, vmem_limit_bytes=VMEM_LIMIT)


def _mod_spec(stream, tm):
    return pl.BlockSpec((None, 1, D_MODEL), stream.group_map(tm))


def _row_spec(tm, width):
    return pl.BlockSpec((tm, width), lambda i, *_: (i, 0))


def _full_spec(shape):
    nd = len(shape)
    return pl.BlockSpec(shape, lambda *_: (0,) * nd)


def _layer_norm(y, g, b):
    mu = jnp.mean(y, axis=-1, keepdims=True)
    d = y - mu
    var = jnp.mean(d * d, axis=-1, keepdims=True)
    return d * lax.rsqrt(var + EPS) * g + b


def _ada_kernel(c_ref, w_ref, b_ref, o_ref):
    c = c_ref[...]
    s = c * jax.nn.sigmoid(c)
    o_ref[...] = jnp.dot(s, w_ref[...], preferred_element_type=F32, precision=HIGHEST) + b_ref[...]


def _ada_mods(cond, ada_w, ada_b):
    n_out = 6 * D_MODEL
    out = pl.pallas_call(
        _ada_kernel,
        grid=(DEPTH, n_out // TN_ADA),
        in_specs=[
            pl.BlockSpec((GROUP_PAD, D_MODEL), lambda l, n: (0, 0)),
            pl.BlockSpec((None, D_MODEL, TN_ADA), lambda l, n: (l, 0, n)),
            pl.BlockSpec((None, 1, TN_ADA), lambda l, n: (l, 0, n)),
        ],
        out_specs=pl.BlockSpec((None, GROUP_PAD, TN_ADA), lambda l, n: (l, 0, n)),
        out_shape=jax.ShapeDtypeStruct((DEPTH, GROUP_PAD, n_out), F32),
        compiler_params=_params("arbitrary", "arbitrary"),
        name="ada_mods",
    )(cond, ada_w, ada_b.reshape(DEPTH, 1, n_out))
    out = out.reshape(DEPTH, GROUP_PAD, 6, D_MODEL).transpose(0, 2, 1, 3)
    return out[:, :, :, None, :]


def _swap_halves(t):
    lane = lax.broadcasted_iota(jnp.int32, t.shape, 1)
    fwd = pltpu.roll(t, LANES - ROT_FREQS, 1)
    bwd = pltpu.roll(t, ROT_FREQS, 1)
    return jnp.where((lane % (2 * ROT_FREQS)) < ROT_FREQS, fwd, bwd)


def _qkv_kernel(*refs, nq, nk, norm, rope, emit_f32, v_layout, qscale):
    refs = list(refs)
    x_ref, sh_ref, sc_ref, w_ref = refs[:4]
    pos = 4
    if norm:
        qg_ref, kg_ref = refs[pos:pos + 2]
        pos += 2
    if rope:
        cos_ref, sin_ref = refs[pos:pos + 2]
        pos += 2
    q_ref, k_ref, v_ref = refs[pos:pos + 3]
    pos += 3
    if emit_f32:
        kf_ref, vf_ref = refs[pos:pos + 2]

    h = (x_ref[...] * (1.0 + sc_ref[...]) + sh_ref[...]).astype(BF16)
    qkv = jnp.dot(h, w_ref[...], preferred_element_type=F32)
    if norm:
        n_heads = (nq + nk) // HEAD_DIM_A
        for hd in range(n_heads):
            lo = hd * HEAD_DIM_A
            t = qkv[:, lo:lo + HEAD_DIM_A]
            ms = jnp.mean(t * t, axis=-1, keepdims=True)
            gain = qg_ref[...] if lo < nq else kg_ref[...]
            t = t * lax.rsqrt(ms + EPS) * gain
            if rope:
                t = t * cos_ref[...] + _swap_halves(t) * sin_ref[...]
            if lo < nq:
                q_ref[:, lo:lo + HEAD_DIM_A] = (t * qscale).astype(BF16)
            else:
                k_ref[:, lo - nq:lo - nq + HEAD_DIM_A] = t.astype(BF16)
                if emit_f32:
                    kf_ref[:, lo - nq:lo - nq + HEAD_DIM_A] = t
    else:
        q_ref[...] = (qkv[:, :nq] * qscale).astype(BF16)
        k = qkv[:, nq:nq + nk]
        k_ref[...] = k.astype(BF16)
        if emit_f32:
            kf_ref[...] = k
    v = qkv[:, nq + nk:]
    if v_layout == "rows":
        v_ref[...] = v.astype(BF16)
    elif v_layout == "t":
        v_ref[...] = v.T.astype(BF16)
    else:
        vt = v.T.astype(BF16)
        for j in range(vt.shape[1] // LANES):
            v_ref[j] = vt[:, j * LANES:(j + 1) * LANES]
    if emit_f32:
        vf_ref[...] = v


def _qkv(stream, x, shift, scale, w, *, nq, nk, qscale, gains=None, rope_tables=None, emit_f32=False,
         v_layout="rows"):
    n = stream.n
    nw = w.shape[1]
    norm = gains is not None
    rope = rope_tables is not None
    in_specs = [_row_spec(TM, D_MODEL), _mod_spec(stream, TM), _mod_spec(stream, TM), _full_spec((D_MODEL, nw))]
    args = [x, shift, scale, w]
    if norm:
        in_specs += [_full_spec((1, HEAD_DIM_A))] * 2
        args += list(gains)
    if rope:
        tiles_per_seq = DEC_SEQ // TM
        tbl_spec = pl.BlockSpec((TM, HEAD_DIM_A), lambda i: (i % tiles_per_seq, 0))
        in_specs += [tbl_spec, tbl_spec]
        args += list(rope_tables)
    out_specs = [_row_spec(TM, nq), _row_spec(TM, nk), _row_spec(TM, nk)]
    out_shape = [jax.ShapeDtypeStruct((n, nq), BF16), jax.ShapeDtypeStruct((n, nk), BF16),
                 jax.ShapeDtypeStruct((n, nk), BF16)]
    if v_layout == "t":
        out_specs[2] = pl.BlockSpec((nk, TM), lambda i: (0, i))
        out_shape[2] = jax.ShapeDtypeStruct((nk, n), BF16)
    elif v_layout == "t_blocked":
        out_specs[2] = pl.BlockSpec((TM // LANES, nk, LANES), lambda i: (i, 0, 0))
        out_shape[2] = jax.ShapeDtypeStruct((n // LANES, nk, LANES), BF16)
    if emit_f32:
        out_specs += [_row_spec(TM, nk), _row_spec(TM, nk)]
        out_shape += [jax.ShapeDtypeStruct((n, nk), F32)] * 2
    return pl.pallas_call(
        functools.partial(_qkv_kernel, nq=nq, nk=nk, norm=norm, rope=rope, emit_f32=emit_f32,
                          v_layout=v_layout, qscale=qscale),
        grid=(n // TM,),
        in_specs=in_specs,
        out_specs=out_specs,
        out_shape=out_shape,
        compiler_params=_params("arbitrary"),
        name="qkv_norm_rope" if norm else "qkv",
    )(*args)


def _attend(q, kvs, biases=None):
    scores = []
    for idx, (k, _) in enumerate(kvs):
        s = lax.dot_general(q, k, (((1,), (1,)), ((), ())), preferred_element_type=F32)
        if biases is not None and biases[idx] is not None:
            s = s + biases[idx]
        scores.append(s)
    m = jnp.max(scores[0], axis=-1, keepdims=True)
    for s in scores[1:]:
        m = jnp.maximum(m, jnp.max(s, axis=-1, keepdims=True))
    denom = None
    out = None
    for s, (_, v) in zip(scores, kvs):
        p = jnp.exp(s - m)
        part = jnp.sum(p, axis=-1, keepdims=True)
        pv = jnp.dot(p.astype(BF16), v, preferred_element_type=F32)
        denom = part if denom is None else denom + part
        out = pv if out is None else out + pv
    return out * (1.0 / denom)


def _gqa_ctx_kernel(q_ref, k_ref, v_ref, o_ref):
    d = HEAD_DIM_A
    group = N_HEADS_A // N_KV_A
    for g in range(N_KV_A):
        heads = [g * group + j for j in range(group)]
        qs = jnp.concatenate([q_ref[:, h * d:(h + 1) * d] for h in heads], axis=0)
        o = _attend(qs, [(k_ref[:, g * d:(g + 1) * d], v_ref[:, g * d:(g + 1) * d])])
        for j, h in enumerate(heads):
            o_ref[:, h * d:(h + 1) * d] = o[j * SEQ:(j + 1) * SEQ].astype(o_ref.dtype)


def _gqa_ctx(q, k, v):
    nq, nk = N_HEADS_A * HEAD_DIM_A, N_KV_A * HEAD_DIM_A
    return pl.pallas_call(
        _gqa_ctx_kernel,
        grid=(BATCH,),
        in_specs=[_row_spec(SEQ, nq), _row_spec(SEQ, nk), _row_spec(SEQ, nk)],
        out_specs=_row_spec(SEQ, nq),
        out_shape=jax.ShapeDtypeStruct((N_CTX, nq), BF16),
        compiler_params=_params("arbitrary"),
        name="gqa_ctx",
    )(q, k, v)


ONES_ROWS = 16


def _scores_t(q, ks):
    return [lax.dot_general(k, q, (((1,), (1,)), ((), ())), preferred_element_type=F32) for k in ks]


def _softmax_pv_t(scores, vts):
    m = jnp.max(scores[0], axis=0, keepdims=True)
    for s in scores[1:]:
        m = jnp.maximum(m, jnp.max(s, axis=0, keepdims=True))
    acc = None
    for s, vt in zip(scores, vts):
        p = jnp.exp2(s - m).astype(BF16)
        vt_ones = jnp.concatenate([vt, jnp.ones((ONES_ROWS, vt.shape[1]), BF16)], axis=0)
        pv = jnp.dot(vt_ones, p, preferred_element_type=F32)
        acc = pv if acc is None else acc + pv
    d = LANES
    return (acc[:d] * (1.0 / acc[d:d + 1])).T


def _gqa_t_kernel(q_ref, k_ref, vt_ref, kc_ref, vct_ref, o_ref, *, n_kv, group, unit, tq):
    d = HEAD_DIM_A
    units = [(g, [g * group + u0 + j for j in range(unit)]) for g in range(n_kv) for u0 in range(0, group, unit)]
    def unit_scores(g, heads):
        ds = slice(g * d, (g + 1) * d)
        qs = jnp.concatenate([q_ref[:, h * d:(h + 1) * d] for h in heads], axis=0)
        return _scores_t(qs, [k_ref[:, ds], kc_ref[:, ds]])

    nxt = unit_scores(*units[0])
    for idx, (g, heads) in enumerate(units):
        sc = nxt
        if idx + 1 < len(units):
            nxt = unit_scores(*units[idx + 1])
        ds = slice(g * d, (g + 1) * d)
        o = _softmax_pv_t(sc, [vt_ref[ds, :], vct_ref[ds, :]])
        for j, h in enumerate(heads):
            o_ref[:, h * d:(h + 1) * d] = o[j * tq:(j + 1) * tq].astype(o_ref.dtype)


def _gqa_lat_t(q, k, vt, kc, vct):
    nq, nk = N_HEADS_A * HEAD_DIM_A, N_KV_A * HEAD_DIM_A
    tiles = DEC_SEQ // TQ_A
    q_spec = pl.BlockSpec((TQ_A, nq), lambda b, t: (b * tiles + t, 0))
    return pl.pallas_call(
        functools.partial(_gqa_t_kernel, n_kv=N_KV_A, group=N_HEADS_A // N_KV_A, unit=UNIT_A, tq=TQ_A),
        grid=(DEC_BATCH, tiles),
        in_specs=[q_spec,
                  pl.BlockSpec((DEC_SEQ, nk), lambda b, t: (b, 0)),
                  pl.BlockSpec((nk, DEC_SEQ), lambda b, t: (0, b)),
                  pl.BlockSpec((None, PAST_LEN, nk), lambda b, t: (b, 0, 0)),
                  pl.BlockSpec((None, nk, PAST_LEN), lambda b, t: (b, 0, 0))],
        out_specs=q_spec,
        out_shape=jax.ShapeDtypeStruct((N_LAT, nq), BF16),
        compiler_params=_params("arbitrary", "arbitrary"),
        name="gqa_lat",
    )(q, k, vt, kc, vct)


def _split_pair(qp):
    lo = lax.broadcasted_iota(jnp.int32, qp.shape, 1) < HEAD_DIM_B
    zero = jnp.zeros_like(qp)
    return jnp.concatenate([jnp.where(lo, qp, zero), jnp.where(lo, zero, qp)], axis=0)


def _merge_pair(o2, rows):
    lo = lax.broadcasted_iota(jnp.int32, (rows, LANES), 1) < HEAD_DIM_B
    return jnp.where(lo, o2[:rows], o2[rows:])


def _mha_ctx_kernel(q_ref, k_ref, v_ref, o_ref):
    for p in range(N_HEADS_B // 2):
        sl = slice(p * LANES, (p + 1) * LANES)
        o2 = _attend(_split_pair(q_ref[:, sl]), [(k_ref[:, sl], v_ref[:, sl])])
        o_ref[:, sl] = _merge_pair(o2, SEQ).astype(o_ref.dtype)


def _mha_ctx(q, k, v):
    return pl.pallas_call(
        _mha_ctx_kernel,
        grid=(BATCH,),
        in_specs=[_row_spec(SEQ, D_MODEL)] * 3,
        out_specs=_row_spec(SEQ, D_MODEL),
        out_shape=jax.ShapeDtypeStruct((N_CTX, D_MODEL), BF16),
        compiler_params=_params("arbitrary"),
        name="mha_ctx",
    )(q, k, v)


NAT_HALF = D_MODEL // 2
NAT_PAIRS = NAT_HALF // LANES
NAT_R = 4
NAT_WROWS = 12
N_DROW = 2 * WIN_H - 1
MASKED_TILE = N_DROW


def _natten_kernel(q_ref, k_ref, vt_ref, kc_ref, vct_ref, tbl_ref, o_ref):
    r0 = pl.program_id(2) * NAT_R
    ws = jnp.clip(r0 - WIN_H // 2, 0, GRID_H - NAT_WROWS)
    row0 = pl.multiple_of(ws * GRID_W, LANES)
    blk0 = ws * GRID_W // LANES
    n_loc = NAT_WROWS * GRID_W

    def tile_index(a, w):
        r = r0 + a
        start = jnp.clip(r - WIN_H // 2, 0, GRID_H - WIN_H)
        wr = ws + w
        valid = (wr >= start) & (wr < start + WIN_H)
        return jnp.where(valid, wr - r + (WIN_H - 1), MASKED_TILE)

    idx = [[tile_index(a, w) for w in range(NAT_WROWS)] for a in range(NAT_R)]

    def pair_scores(p):
        sl = slice(p * LANES, (p + 1) * LANES)
        q2 = jnp.concatenate([_split_pair(q_ref[a * GRID_W:(a + 1) * GRID_W, sl]) for a in range(NAT_R)], axis=0)
        bias_t = jnp.concatenate(
            [jnp.concatenate([tbl_ref[idx[a][w], p] for w in range(NAT_WROWS)], axis=0) for a in range(NAT_R)],
            axis=1)
        s_loc, s_ctx = _scores_t(q2, [k_ref[pl.ds(row0, n_loc), sl], kc_ref[:, sl]])
        return [s_loc + bias_t, s_ctx]

    nxt = pair_scores(0)
    for p in range(NAT_PAIRS):
        sl = slice(p * LANES, (p + 1) * LANES)
        sc = nxt
        if p + 1 < NAT_PAIRS:
            nxt = pair_scores(p + 1)
        vt_win = jnp.concatenate([vt_ref[blk0 + j, sl, :] for j in range(n_loc // LANES)], axis=1)
        o2 = _softmax_pv_t(sc, [vt_win, vct_ref[sl, :]])
        for a in range(NAT_R):
            o_ref[a * GRID_W:(a + 1) * GRID_W, sl] = _merge_pair(
                o2[a * LANES:(a + 1) * LANES], GRID_W).astype(o_ref.dtype)


def _natten(q, k, vt, kc, vct, tbl):
    n_half = D_MODEL // NAT_HALF
    blocks = GRID_H // NAT_R
    seq_blocks = DEC_SEQ // LANES
    q_spec = pl.BlockSpec((NAT_R * GRID_W, NAT_HALF), lambda b, hh, r: (b * blocks + r, hh))
    return pl.pallas_call(
        _natten_kernel,
        grid=(DEC_BATCH, n_half, blocks),
        in_specs=[q_spec,
                  pl.BlockSpec((DEC_SEQ, NAT_HALF), lambda b, hh, r: (b, hh)),
                  pl.BlockSpec((seq_blocks, NAT_HALF, LANES), lambda b, hh, r: (b, hh, 0)),
                  pl.BlockSpec((None, PAST_LEN, NAT_HALF), lambda b, hh, r: (b, 0, hh)),
                  pl.BlockSpec((None, NAT_HALF, PAST_LEN), lambda b, hh, r: (b, hh, 0)),
                  pl.BlockSpec((N_DROW + 1, NAT_PAIRS, GRID_W, LANES), lambda b, hh, r: (0, hh, 0, 0))],
        out_specs=q_spec,
        out_shape=jax.ShapeDtypeStruct((N_LAT, D_MODEL), BF16),
        compiler_params=_params("arbitrary", "arbitrary", "arbitrary"),
        name="natten",
    )(q, k, vt, kc, vct, tbl)


def _bias_table_kernel(rpb_ref, o_ref):
    dr = pl.program_id(0)
    row = lax.broadcasted_iota(jnp.int32, (GRID_W, LANES), 0)
    col = lax.broadcasted_iota(jnp.int32, (GRID_W, LANES), 1)
    sub = col >= GRID_W
    qc = col % GRID_W
    kc = row
    dc = jnp.clip(kc - qc, -(WIN_W - 1), WIN_W - 1) + (WIN_W - 1)
    col_start = jnp.clip(qc - WIN_W // 2, 0, GRID_W - WIN_W)
    in_win = (kc >= col_start) & (kc < col_start + WIN_W) & (dr < N_DROW)
    n_dc = 2 * WIN_W - 1
    drc = jnp.minimum(dr, N_DROW - 1)
    for pr in range(N_HEADS_B // 2):
        b0 = ((2 * pr) * N_DROW + drc) * n_dc
        b1 = ((2 * pr + 1) * N_DROW + drc) * n_dc

        def body(j, acc, b0=b0, b1=b1):
            return jnp.where(dc == j, jnp.where(sub, rpb_ref[b1 + j], rpb_ref[b0 + j]), acc)

        acc = lax.fori_loop(0, n_dc, body, jnp.zeros((GRID_W, LANES), F32))
        o_ref[pr] = jnp.where(in_win, acc * LOG2_E, NEG_INF)


def _bias_table(rpb):
    return pl.pallas_call(
        _bias_table_kernel,
        grid=(N_DROW + 1,),
        in_specs=[pl.BlockSpec(memory_space=pltpu.SMEM)],
        out_specs=pl.BlockSpec((None, N_HEADS_B // 2, GRID_W, LANES), lambda d: (d, 0, 0, 0)),
        out_shape=jax.ShapeDtypeStruct((N_DROW + 1, N_HEADS_B // 2, GRID_W, LANES), F32),
        compiler_params=_params("arbitrary"),
        name="natten_bias_table",
    )(rpb.reshape(-1))


def _wo_ln_kernel(a_ref, w_ref, x_ref, gate_ref, g_ref, b_ref, o_ref):
    o = jnp.dot(a_ref[...], w_ref[...], preferred_element_type=F32)
    y = ALPHA * x_ref[...] + gate_ref[...] * o
    o_ref[...] = _layer_norm(y, g_ref[...], b_ref[...])


def _wo_ln(stream, a, w, x, gate, ln_g, ln_b):
    vec = _full_spec((1, D_MODEL))
    return pl.pallas_call(
        _wo_ln_kernel,
        grid=(stream.n // TM,),
        in_specs=[_row_spec(TM, a.shape[1]), _full_spec(w.shape), _row_spec(TM, D_MODEL),
                  _mod_spec(stream, TM), vec, vec],
        out_specs=_row_spec(TM, D_MODEL),
        out_shape=jax.ShapeDtypeStruct((stream.n, D_MODEL), F32),
        compiler_params=_params("arbitrary"),
        name="wo_postnorm",
    )(a, w, x, gate, ln_g, ln_b)


def _swiglu(h, wg_ref, wu_ref, wd_ref):
    acc = None
    for lo, hi in FF_CHUNKS:
        g = jnp.dot(h, wg_ref[:, lo:hi], preferred_element_type=F32)
        u = jnp.dot(h, wu_ref[:, lo:hi], preferred_element_type=F32)
        a = (g * jax.nn.sigmoid(g) * u).astype(BF16)
        part = jnp.dot(a, wd_ref[lo:hi, :], preferred_element_type=F32)
        acc = part if acc is None else acc + part
    return acc


def _ffn_dense_kernel(x_ref, sh_ref, sc_ref, wg_ref, wu_ref, wd_ref, gate_ref, g_ref, b_ref, o_ref):
    x = x_ref[...]
    h = (x * (1.0 + sc_ref[...]) + sh_ref[...]).astype(BF16)
    y = ALPHA * x + gate_ref[...] * _swiglu(h, wg_ref, wu_ref, wd_ref)
    o_ref[...] = _layer_norm(y, g_ref[...], b_ref[...])


def _ffn_dense(stream, x, shift, scale, w_gu, w_down, gate, ln_g, ln_b):
    vec = _full_spec((1, D_MODEL))
    mod = _mod_spec(stream, TM_FFN)
    row = _row_spec(TM_FFN, D_MODEL)
    return pl.pallas_call(
        _ffn_dense_kernel,
        grid=(stream.n // TM_FFN,),
        in_specs=[row, mod, mod,
                  pl.BlockSpec((D_MODEL, D_FF), lambda i: (0, 0)),
                  pl.BlockSpec((D_MODEL, D_FF), lambda i: (0, 1)),
                  _full_spec((D_FF, D_MODEL)),
                  mod, vec, vec],
        out_specs=row,
        out_shape=jax.ShapeDtypeStruct((stream.n, D_MODEL), F32),
        compiler_params=_params("arbitrary"),
        name="ffn_dense",
    )(x, shift, scale, w_gu, w_gu, w_down, gate, ln_g, ln_b)


def _ffn_grouped_kernel(te_ref, tv_ref, x_ref, wg_ref, wu_ref, wd_ref, o_ref):
    i = pl.program_id(0)

    @pl.when(tv_ref[i] != 0)
    def _():
        o_ref[...] = _swiglu(x_ref[...].astype(BF16), wg_ref, wu_ref, wd_ref)

    @pl.when(tv_ref[i] == 0)
    def _():
        o_ref[...] = jnp.zeros_like(o_ref)


def _ffn_grouped(tile_expert, tile_valid, xs, w_gu, w_down):
    row = pl.BlockSpec((TMG, D_MODEL), lambda i, te, tv: (i, 0))
    grid_spec = pltpu.PrefetchScalarGridSpec(
        num_scalar_prefetch=2,
        grid=(NT_MOE,),
        in_specs=[row,
                  pl.BlockSpec((None, D_MODEL, D_FF), lambda i, te, tv: (te[i], 0, 0)),
                  pl.BlockSpec((None, D_MODEL, D_FF), lambda i, te, tv: (te[i], 0, 1)),
                  pl.BlockSpec((None, D_FF, D_MODEL), lambda i, te, tv: (te[i], 0, 0))],
        out_specs=row,
    )
    return pl.pallas_call(
        _ffn_grouped_kernel,
        grid_spec=grid_spec,
        out_shape=jax.ShapeDtypeStruct((NT_MOE * TMG, D_MODEL), F32),
        compiler_params=_params("arbitrary"),
        name="ffn_grouped",
    )(tile_expert, tile_valid, xs, w_gu, w_gu, w_down)


META_E0, META_E1, META_W0, META_W1, META_R0, META_R1 = range(6)


def _router_kernel(x_ref, sh_ref, sc_ref, rw_ref, cnt_in_ref, h_ref, meta_ref, cnt_ref, carry_scr):
    i = pl.program_id(0)

    @pl.when(i == 0)
    def _():
        carry_scr[...] = cnt_in_ref[...]

    h = x_ref[...] * (1.0 + sc_ref[...]) + sh_ref[...]
    h_ref[...] = h
    w = rw_ref[...]
    h_hi = h.astype(BF16)
    w_hi = w.astype(BF16)
    h_lo = (h - h_hi.astype(F32)).astype(BF16)
    w_lo = (w - w_hi.astype(F32)).astype(BF16)
    logits = (jnp.dot(h_hi, w_hi, preferred_element_type=F32)
              + (jnp.dot(h_lo, w_hi, preferred_element_type=F32)
                 + jnp.dot(h_hi, w_lo, preferred_element_type=F32)))
    lane = lax.broadcasted_iota(jnp.int32, logits.shape, 1).astype(F32)
    lg = jnp.where(lane < N_EXPERTS, logits, -jnp.inf)
    m1 = jnp.max(lg, axis=-1, keepdims=True)
    i1 = jnp.min(jnp.where(lg == m1, lane, float(LANES)), axis=-1, keepdims=True)
    lg2 = jnp.where(lane == i1, -jnp.inf, lg)
    m2 = jnp.max(lg2, axis=-1, keepdims=True)
    i2 = jnp.min(jnp.where(lg2 == m2, lane, float(LANES)), axis=-1, keepdims=True)
    e = jnp.exp(m2 - m1)
    w1 = 1.0 / (1.0 + e)
    w2 = e / (1.0 + e)

    sel1 = lane == i1
    sel2 = lane == i2
    onehot = jnp.where(sel1 | sel2, 1.0, 0.0)
    rr = lax.broadcasted_iota(jnp.int32, (TM, TM), 0)
    cc = lax.broadcasted_iota(jnp.int32, (TM, TM), 1)
    lower = jnp.where(cc < rr, 1.0, 0.0).astype(BF16)
    before = jnp.dot(lower, onehot.astype(BF16), preferred_element_type=F32) + carry_scr[0:1, :]
    r1 = jnp.sum(jnp.where(sel1, before, 0.0), axis=-1, keepdims=True)
    r2 = jnp.sum(jnp.where(sel2, before, 0.0), axis=-1, keepdims=True)
    carry_scr[...] = carry_scr[...] + jnp.sum(onehot, axis=0, keepdims=True)
    cnt_ref[...] = carry_scr[...]

    cols = [i1, i2, w1, w2, r1, r2]
    meta = jnp.zeros(logits.shape, F32)
    for c, val in enumerate(cols):
        meta = jnp.where(lane == c, val, meta)
    meta_ref[...] = meta


def _router(stream, x, shift, scale, rw_pad, cnt_in):
    cnt_spec = _full_spec((SUBLANES, LANES))
    return pl.pallas_call(
        _router_kernel,
        grid=(stream.n // TM,),
        in_specs=[_row_spec(TM, D_MODEL), _mod_spec(stream, TM), _mod_spec(stream, TM),
                  _full_spec((D_MODEL, LANES)), cnt_spec],
        out_specs=[_row_spec(TM, D_MODEL), _row_spec(TM, LANES), cnt_spec],
        out_shape=[jax.ShapeDtypeStruct((stream.n, D_MODEL), F32), jax.ShapeDtypeStruct((stream.n, LANES), F32),
                   jax.ShapeDtypeStruct((SUBLANES, LANES), F32)],
        scratch_shapes=[pltpu.VMEM((SUBLANES, LANES), F32)],
        compiler_params=_params("arbitrary"),
        name="moe_router",
    )(x, shift, scale, rw_pad, cnt_in)


def _row_copy(src_ref, src_row, dst_ref, dst_row, sem):
    return pltpu.make_async_copy(src_ref.at[pl.ds(src_row, 1)], dst_ref.at[pl.ds(dst_row, 1)], sem)


def _scatter_rows(d0_ref, d1_ref, h_ref, buf_ref, sem):
    base = pl.program_id(0) * TM
    for r in range(TM):
        _row_copy(h_ref, r, buf_ref, d0_ref[base + r], sem).start(priority=0)
        _row_copy(h_ref, r, buf_ref, d1_ref[base + r], sem).start(priority=1)


def _scatter_wait(h_ref, buf_ref, sem, n_rows):
    pltpu.make_async_copy(h_ref.at[pl.ds(0, n_rows)], buf_ref.at[pl.ds(0, n_rows)], sem).wait()


def _scatter_kernel(d0_ref, d1_ref, pad_ref, hc_ref, hl_ref, buf_ref, zero_scr, sem, *, ctx_steps, pads_per_step):
    i = pl.program_id(0)

    @pl.when(i < ctx_steps)
    def _():
        _scatter_rows(d0_ref, d1_ref, hc_ref, buf_ref, sem)
        zero_scr[...] = jnp.zeros_like(zero_scr)
        pbase = i * pads_per_step
        for r in range(pads_per_step):
            _row_copy(zero_scr, 0, buf_ref, pad_ref[pbase + r], sem).start(priority=r % 2)
        _scatter_wait(hc_ref, buf_ref, sem, TM)
        _scatter_wait(hc_ref, buf_ref, sem, TM)
        _scatter_wait(hc_ref, buf_ref, sem, pads_per_step)

    @pl.when(i >= ctx_steps)
    def _():
        _scatter_rows(d0_ref, d1_ref, hl_ref, buf_ref, sem)
        _scatter_wait(hl_ref, buf_ref, sem, TM)
        _scatter_wait(hl_ref, buf_ref, sem, TM)


def _scatter(d0, d1, pad_rows, h_ctx, h_lat):
    ctx_steps = CTX.n // TM
    steps = ctx_steps + LAT.n // TM
    assert N_PAD_ROWS % ctx_steps == 0
    grid_spec = pltpu.PrefetchScalarGridSpec(
        num_scalar_prefetch=3,
        grid=(steps,),
        in_specs=[pl.BlockSpec((TM, D_MODEL), lambda i, a, b, c: (jnp.minimum(i, ctx_steps - 1), 0)),
                  pl.BlockSpec((TM, D_MODEL), lambda i, a, b, c: (jnp.maximum(i - ctx_steps, 0), 0))],
        out_specs=pl.BlockSpec(memory_space=pl.ANY),
        scratch_shapes=[pltpu.VMEM((SUBLANES, D_MODEL), F32), pltpu.SemaphoreType.DMA],
    )
    return pl.pallas_call(
        functools.partial(_scatter_kernel, ctx_steps=ctx_steps, pads_per_step=N_PAD_ROWS // ctx_steps),
        grid_spec=grid_spec,
        out_shape=jax.ShapeDtypeStruct((NT_MOE * TMG, D_MODEL), F32),
        compiler_params=_params("arbitrary"),
        name="moe_scatter",
    )(d0, d1, pad_rows, h_ctx, h_lat)


def _combine_kernel(d0_ref, d1_ref, ys_ref, meta_ref, x_ref, gate_ref, g_ref, b_ref, o_ref, rows_scr, sem):
    base = pl.program_id(0) * TM

    for r in range(TM):
        _row_copy(ys_ref, d0_ref[base + r], rows_scr.at[0], r, sem).start(priority=0)
        _row_copy(ys_ref, d1_ref[base + r], rows_scr.at[1], r, sem).start(priority=1)
    for s in range(2):
        pltpu.make_async_copy(ys_ref.at[pl.ds(0, TM)], rows_scr.at[s], sem).wait()

    meta = meta_ref[...]
    w0 = meta[:, META_W0:META_W0 + 1]
    w1 = meta[:, META_W1:META_W1 + 1]
    f = w0 * rows_scr[0] + w1 * rows_scr[1]
    y = ALPHA * x_ref[...] + gate_ref[...] * f
    o_ref[...] = _layer_norm(y, g_ref[...], b_ref[...])


def _combine(stream, d0, d1, ys, meta, x, gate, ln_g, ln_b):
    vec = pl.BlockSpec((1, D_MODEL), lambda i, a, b: (0, 0))
    grid_spec = pltpu.PrefetchScalarGridSpec(
        num_scalar_prefetch=2,
        grid=(stream.n // TM,),
        in_specs=[pl.BlockSpec(memory_space=pl.ANY), _row_spec(TM, LANES), _row_spec(TM, D_MODEL),
                  _mod_spec(stream, TM), vec, vec],
        out_specs=_row_spec(TM, D_MODEL),
        scratch_shapes=[pltpu.VMEM((2, TM, D_MODEL), F32), pltpu.SemaphoreType.DMA],
    )
    return pl.pallas_call(
        _combine_kernel,
        grid_spec=grid_spec,
        out_shape=jax.ShapeDtypeStruct((stream.n, D_MODEL), F32),
        compiler_params=_params("arbitrary"),
        name="moe_combine",
    )(d0, d1, ys, meta, x, gate, ln_g, ln_b)


def _rope_tables():
    t = jnp.arange(DEC_SEQ)
    row = (t // GRID_W).astype(F32)
    col = (t % GRID_W).astype(F32)
    freqs = ROPE_THETA ** (-jnp.arange(ROT_FREQS, dtype=F32) / ROT_FREQS)
    ar = row[:, None] * freqs
    ac = col[:, None] * freqs
    cos = jnp.concatenate([jnp.cos(ar), jnp.cos(ar), jnp.cos(ac), jnp.cos(ac)], axis=1)
    sin = jnp.concatenate([-jnp.sin(ar), jnp.sin(ar), -jnp.sin(ac), jnp.sin(ac)], axis=1)
    return cos, sin


def _routing_plan(metas, counts):
    cnt = counts[0, :N_EXPERTS].astype(jnp.int32)
    tiles_e = (cnt + TMG - 1) // TMG
    tile_end = jnp.cumsum(tiles_e)
    tile_start = tile_end - tiles_e
    offs = tile_start * TMG
    dests = []
    for meta in metas:
        pair = []
        for ecol, rcol in ((META_E0, META_R0), (META_E1, META_R1)):
            e = meta[:, ecol].astype(jnp.int32)
            d = meta[:, rcol].astype(jnp.int32)
            for k in range(N_EXPERTS):
                d = d + jnp.where(e == k, offs[k], 0)
            pair.append(d)
        dests.append(tuple(pair))
    tid = jnp.arange(NT_MOE, dtype=jnp.int32)
    te = jnp.minimum(jnp.sum((tid[:, None] >= tile_end[None, :]).astype(jnp.int32), axis=1), N_EXPERTS - 1)
    total = tile_end[-1]
    valid = tid < total
    te_last = jnp.max(jnp.where(valid, te, 0))
    te = jnp.where(valid, te, te_last)
    pad_cnt = tiles_e * TMG - cnt
    pad_end = jnp.cumsum(pad_cnt)
    pad_start = pad_end - pad_cnt
    k = jnp.arange(N_PAD_ROWS, dtype=jnp.int32)
    grp = jnp.sum((k[:, None] >= pad_end[None, :]).astype(jnp.int32), axis=1)
    pad_rows = total * TMG + (k - pad_end[-1])
    for e in range(N_EXPERTS):
        pad_rows = jnp.where(grp == e, offs[e] + cnt[e] + (k - pad_start[e]), pad_rows)
    return dests, te, valid.astype(jnp.int32), pad_rows


def kernel(x_prompt, x_sample, cache_k_a, cache_v_a, cache_k_b, cache_v_b, c, c_ctx, ada_w, ada_b, ln_attn_g, ln_attn_b, ln_ffn_g, ln_ffn_b, wqkv_a, qnorm_a, knorm_a, wo_a, wqkv_b, rpb_b, wo_b, ffn_w_gu, ffn_w_down, router_w, moe_w_gu, moe_w_down):
    streams = (CTX, LAT)
    xs = [x_prompt.reshape(N_CTX, D_MODEL), x_sample.reshape(N_LAT, D_MODEL)]

    cond = jnp.zeros((GROUP_PAD, D_MODEL), F32).at[0].set(c_ctx).at[1:N_GROUPS].set(c)
    mods = _ada_mods(cond, ada_w, ada_b)
    vec = lambda a, l: a[l].reshape(1, D_MODEL)

    m = mods[0]
    w_qkv = wqkv_a[0].astype(BF16)
    w_o = wo_a[0].astype(BF16)
    w_gu = ffn_w_gu[0].astype(BF16)
    w_dn = ffn_w_down[0].astype(BF16)
    gains = (qnorm_a[0].reshape(1, HEAD_DIM_A), knorm_a[0].reshape(1, HEAD_DIM_A))
    nq, nk = N_HEADS_A * HEAD_DIM_A, N_KV_A * HEAD_DIM_A
    qscale = HEAD_DIM_A ** -0.5

    qp, kp, vp, kp32, vp32 = _qkv(CTX, xs[0], m[0], m[1], w_qkv, nq=nq, nk=nk, qscale=qscale, gains=gains,
                                  emit_f32=True)
    new_k_a = kp32.reshape(BATCH, 1, SEQ, N_KV_A, HEAD_DIM_A)
    new_v_a = vp32.reshape(BATCH, 1, SEQ, N_KV_A, HEAD_DIM_A)
    ql, kl, vlt = _qkv(LAT, xs[1], m[0], m[1], w_qkv, nq=nq, nk=nk, qscale=qscale * LOG2_E, gains=gains,
                       rope_tables=_rope_tables(), v_layout="t")
    cache_k = cache_k_a[:, 0].reshape(DEC_BATCH, PAST_LEN, nk).astype(BF16)
    cache_vt = jnp.swapaxes(cache_v_a[:, 0].reshape(DEC_BATCH, PAST_LEN, nk), 1, 2).astype(BF16)
    attn = [_gqa_ctx(qp, kp, vp), _gqa_lat_t(ql, kl, vlt, cache_k, cache_vt)]
    xs = [_wo_ln(s, a, w_o, x, m[2], vec(ln_attn_g, 0), vec(ln_attn_b, 0)) for s, a, x in zip(streams, attn, xs)]
    xs = [_ffn_dense(s, x, m[3], m[4], w_gu, w_dn, m[5], vec(ln_ffn_g, 0), vec(ln_ffn_b, 0))
          for s, x in zip(streams, xs)]

    m = mods[1]
    w_qkv = wqkv_b[0].astype(BF16)
    w_o = wo_b[0].astype(BF16)
    qscale = HEAD_DIM_B ** -0.5
    qp, kp, vp, kp32, vp32 = _qkv(CTX, xs[0], m[0], m[1], w_qkv, nq=D_MODEL, nk=D_MODEL, qscale=qscale,
                                  emit_f32=True)
    new_k_b = kp32.reshape(BATCH, 1, SEQ, N_HEADS_B, HEAD_DIM_B)
    new_v_b = vp32.reshape(BATCH, 1, SEQ, N_HEADS_B, HEAD_DIM_B)
    ql, kl, vlt = _qkv(LAT, xs[1], m[0], m[1], w_qkv, nq=D_MODEL, nk=D_MODEL, qscale=qscale * LOG2_E,
                       v_layout="t_blocked")
    cache_k = cache_k_b[:, 0].reshape(DEC_BATCH, PAST_LEN, D_MODEL).astype(BF16)
    cache_vt = jnp.swapaxes(cache_v_b[:, 0].reshape(DEC_BATCH, PAST_LEN, D_MODEL), 1, 2).astype(BF16)
    attn = [_mha_ctx(qp, kp, vp), _natten(ql, kl, vlt, cache_k, cache_vt, _bias_table(rpb_b[0]))]
    xs = [_wo_ln(s, a, w_o, x, m[2], vec(ln_attn_g, 1), vec(ln_attn_b, 1)) for s, a, x in zip(streams, attn, xs)]

    rw_pad = jnp.zeros((D_MODEL, LANES), F32).at[:, :N_EXPERTS].set(router_w[0])
    counts = jnp.zeros((SUBLANES, LANES), F32)
    hs, metas = [], []
    for s, x in zip(streams, xs):
        h, meta, counts = _router(s, x, m[3], m[4], rw_pad, counts)
        hs.append(h)
        metas.append(meta)
    dests, tile_expert, tile_valid, pad_rows = _routing_plan(metas, counts)
    d0_all = jnp.concatenate([dests[0][0], dests[1][0]])
    d1_all = jnp.concatenate([dests[0][1], dests[1][1]])
    sorted_rows = _scatter(d0_all, d1_all, pad_rows, hs[0], hs[1])
    ys = _ffn_grouped(tile_expert, tile_valid, sorted_rows, moe_w_gu[0].astype(BF16), moe_w_down[0].astype(BF16))
    outs = [_combine(s, d0, d1, ys, meta, x, m[5], vec(ln_ffn_g, 1), vec(ln_ffn_b, 1))
            for s, (d0, d1), meta, x in zip(streams, dests, metas, xs)]

    y_prompt = outs[0].reshape(BATCH, SEQ, D_MODEL)
    y_sample = outs[1].reshape(DEC_BATCH, DEC_SEQ, D_MODEL)
    return (y_prompt, y_sample, new_k_a, new_v_a, new_k_b, new_v_b)
```

```python
import functools
import math

import jax
import jax.numpy as jnp
from jax import lax
from jax.experimental import pallas as pl
from jax.experimental.pallas import tpu as pltpu

F32 = jnp.float32
BF16 = jnp.bfloat16
HIGHEST = lax.Precision.HIGHEST

D_MODEL = 1024
BATCH, SEQ = 32, 256
DEC_BATCH, DEC_SEQ = 4, 4096
PAST_LEN = 256
DEPTH = 2
GRID_W = 64
GRID_H = DEC_SEQ // GRID_W
N_HEADS_A, N_KV_A, HEAD_DIM_A = 8, 2, 128
ROT_FREQS = HEAD_DIM_A // 4
ROPE_THETA = 10000.0
N_HEADS_B, HEAD_DIM_B = 16, 64
WIN_H, WIN_W = 8, 16
D_FF = 2816
N_EXPERTS = 8
EPS = 1e-6
NEG_INF = -1e30
ALPHA = (2.0 * DEPTH) ** 0.25
LOG2_E = math.log2(math.e)

N_CTX = BATCH * SEQ
N_LAT = DEC_BATCH * DEC_SEQ
N_GROUPS = 1 + DEC_BATCH
GROUP_PAD = 8

LANES = 128
SUBLANES = 8
VMEM_LIMIT = 56 * 2**20

TM = 512
MXU_DIM = 256
FF_CHUNKS = ((0, 1024), (1024, 2048), (2048, D_FF))
TM_FFN = 512
TMG = 512
TQ_A = 256
UNIT_A = 2
TN_ADA = 1536
N_PAIRS = 2 * (N_CTX + N_LAT)
NT_MOE = N_PAIRS // TMG + N_EXPERTS
N_PAD_ROWS = NT_MOE * TMG - N_PAIRS


class Stream:
    def __init__(self, n_rows, group_offset, rows_per_group):
        self.n = n_rows
        self.goff = group_offset
        self.rpg = rows_per_group

    def group_map(self, tm):
        tiles_per_group = self.rpg // tm
        goff = self.goff
        return lambda i, *_: (goff + i // tiles_per_group, 0, 0)


CTX = Stream(N_CTX, 0, N_CTX)
LAT = Stream(N_LAT, 1, DEC_SEQ)


def _params(*sem):
    return pltpu.CompilerParams(dimension_semantics=sem, vmem_limit_bytes=VMEM_LIMIT)


def _mod_spec(stream, tm):
    return pl.BlockSpec((None, 1, D_MODEL), stream.group_map(tm))


def _row_spec(tm, width):
    return pl.BlockSpec((tm, width), lambda i, *_: (i, 0))


def _full_spec(shape):
    nd = len(shape)
    return pl.BlockSpec(shape, lambda *_: (0,) * nd)


def _layer_norm(y, g, b):
    mu = jnp.mean(y, axis=-1, keepdims=True)
    d = y - mu
    var = jnp.mean(d * d, axis=-1, keepdims=True)
    return d * lax.rsqrt(var + EPS) * g + b


def _ada_kernel(c_ref, w_ref, b_ref, o_ref):
    c = c_ref[...]
    s = c * jax.nn.sigmoid(c)
    o_ref[...] = jnp.dot(s, w_ref[...], preferred_element_type=F32, precision=HIGHEST) + b_ref[...]


def _ada_mods(cond, ada_w, ada_b):
    n_out = 6 * D_MODEL
    out = pl.pallas_call(
        _ada_kernel,
        grid=(DEPTH, n_out // TN_ADA),
        in_specs=[
            pl.BlockSpec((GROUP_PAD, D_MODEL), lambda l, n: (0, 0)),
            pl.BlockSpec((None, D_MODEL, TN_ADA), lambda l, n: (l, 0, n)),
            pl.BlockSpec((None, 1, TN_ADA), lambda l, n: (l, 0, n)),
        ],
        out_specs=pl.BlockSpec((None, GROUP_PAD, TN_ADA), lambda l, n: (l, 0, n)),
        out_shape=jax.ShapeDtypeStruct((DEPTH, GROUP_PAD, n_out), F32),
        compiler_params=_params("arbitrary", "arbitrary"),
        name="ada_mods",
    )(cond, ada_w, ada_b.reshape(DEPTH, 1, n_out))
    out = out.reshape(DEPTH, GROUP_PAD, 6, D_MODEL).transpose(0, 2, 1, 3)
    return out[:, :, :, None, :]


def _swap_halves(t):
    lane = lax.broadcasted_iota(jnp.int32, t.shape, 1)
    fwd = pltpu.roll(t, LANES - ROT_FREQS, 1)
    bwd = pltpu.roll(t, ROT_FREQS, 1)
    return jnp.where((lane % (2 * ROT_FREQS)) < ROT_FREQS, fwd, bwd)


def _qkv_kernel(*refs, nq, nk, norm, rope, emit_f32, v_layout, qscale):
    refs = list(refs)
    x_ref, sh_ref, sc_ref, w_ref = refs[:4]
    pos = 4
    if norm:
        qg_ref, kg_ref = refs[pos:pos + 2]
        pos += 2
    if rope:
        cos_ref, sin_ref = refs[pos:pos + 2]
        pos += 2
    q_ref, k_ref, v_ref = refs[pos:pos + 3]
    pos += 3
    if emit_f32:
        kf_ref, vf_ref = refs[pos:pos + 2]

    h = (x_ref[...] * (1.0 + sc_ref[...]) + sh_ref[...]).astype(BF16)
    qkv = jnp.dot(h, w_ref[...], preferred_element_type=F32)
    if norm:
        n_heads = (nq + nk) // HEAD_DIM_A
        for hd in range(n_heads):
            lo = hd * HEAD_DIM_A
            t = qkv[:, lo:lo + HEAD_DIM_A]
            ms = jnp.mean(t * t, axis=-1, keepdims=True)
            gain = qg_ref[...] if lo < nq else kg_ref[...]
            t = t * lax.rsqrt(ms + EPS) * gain
            if rope:
                t = t * cos_ref[...] + _swap_halves(t) * sin_ref[...]
            if lo < nq:
                q_ref[:, lo:lo + HEAD_DIM_A] = (t * qscale).astype(BF16)
            else:
                k_ref[:, lo - nq:lo - nq + HEAD_DIM_A] = t.astype(BF16)
                if emit_f32:
                    kf_ref[:, lo - nq:lo - nq + HEAD_DIM_A] = t
    else:
        q_ref[...] = (qkv[:, :nq] * qscale).astype(BF16)
        k = qkv[:, nq:nq + nk]
        k_ref[...] = k.astype(BF16)
        if emit_f32:
            kf_ref[...] = k
    v = qkv[:, nq + nk:]
    if v_layout == "rows":
        v_ref[...] = v.astype(BF16)
    elif v_layout == "t":
        v_ref[...] = v.T.astype(BF16)
    else:
        vt = v.T.astype(BF16)
        for j in range(vt.shape[1] // LANES):
            v_ref[j] = vt[:, j * LANES:(j + 1) * LANES]
    if emit_f32:
        vf_ref[...] = v


def _qkv(stream, x, shift, scale, w, *, nq, nk, qscale, gains=None, rope_tables=None, emit_f32=False,
         v_layout="rows"):
    n = stream.n
    nw = w.shape[1]
    norm = gains is not None
    rope = rope_tables is not None
    in_specs = [_row_spec(TM, D_MODEL), _mod_spec(stream, TM), _mod_spec(stream, TM), _full_spec((D_MODEL, nw))]
    args = [x, shift, scale, w]
    if norm:
        in_specs += [_full_spec((1, HEAD_DIM_A))] * 2
        args += list(gains)
    if rope:
        tiles_per_seq = DEC_SEQ // TM
        tbl_spec = pl.BlockSpec((TM, HEAD_DIM_A), lambda i: (i % tiles_per_seq, 0))
        in_specs += [tbl_spec, tbl_spec]
        args += list(rope_tables)
    out_specs = [_row_spec(TM, nq), _row_spec(TM, nk), _row_spec(TM, nk)]
    out_shape = [jax.ShapeDtypeStruct((n, nq), BF16), jax.ShapeDtypeStruct((n, nk), BF16),
                 jax.ShapeDtypeStruct((n, nk), BF16)]
    if v_layout == "t":
        out_specs[2] = pl.BlockSpec((nk, TM), lambda i: (0, i))
        out_shape[2] = jax.ShapeDtypeStruct((nk, n), BF16)
    elif v_layout == "t_blocked":
        out_specs[2] = pl.BlockSpec((TM // LANES, nk, LANES), lambda i: (i, 0, 0))
        out_shape[2] = jax.ShapeDtypeStruct((n // LANES, nk, LANES), BF16)
    if emit_f32:
        out_specs += [_row_spec(TM, nk), _row_spec(TM, nk)]
        out_shape += [jax.ShapeDtypeStruct((n, nk), F32)] * 2
    return pl.pallas_call(
        functools.partial(_qkv_kernel, nq=nq, nk=nk, norm=norm, rope=rope, emit_f32=emit_f32,
                          v_layout=v_layout, qscale=qscale),
        grid=(n // TM,),
        in_specs=in_specs,
        out_specs=out_specs,
        out_shape=out_shape,
        compiler_params=_params("arbitrary"),
        name="qkv_norm_rope" if norm else "qkv",
    )(*args)


def _attend(q, kvs, biases=None):
    scores = []
    for idx, (k, _) in enumerate(kvs):
        s = lax.dot_general(q, k, (((1,), (1,)), ((), ())), preferred_element_type=F32)
        if biases is not None and biases[idx] is not None:
            s = s + biases[idx]
        scores.append(s)
    m = jnp.max(scores[0], axis=-1, keepdims=True)
    for s in scores[1:]:
        m = jnp.maximum(m, jnp.max(s, axis=-1, keepdims=True))
    denom = None
    out = None
    for s, (_, v) in zip(scores, kvs):
        p = jnp.exp(s - m)
        part = jnp.sum(p, axis=-1, keepdims=True)
        pv = jnp.dot(p.astype(BF16), v, preferred_element_type=F32)
        denom = part if denom is None else denom + part
        out = pv if out is None else out + pv
    return out * (1.0 / denom)


def _gqa_ctx_kernel(q_ref, k_ref, v_ref, o_ref):
    d = HEAD_DIM_A
    group = N_HEADS_A // N_KV_A
    for g in range(N_KV_A):
        heads = [g * group + j for j in range(group)]
        qs = jnp.concatenate([q_ref[:, h * d:(h + 1) * d] for h in heads], axis=0)
        o = _attend(qs, [(k_ref[:, g * d:(g + 1) * d], v_ref[:, g * d:(g + 1) * d])])
        for j, h in enumerate(heads):
            o_ref[:, h * d:(h + 1) * d] = o[j * SEQ:(j + 1) * SEQ].astype(o_ref.dtype)


def _gqa_ctx(q, k, v):
    nq, nk = N_HEADS_A * HEAD_DIM_A, N_KV_A * HEAD_DIM_A
    return pl.pallas_call(
        _gqa_ctx_kernel,
        grid=(BATCH,),
        in_specs=[_row_spec(SEQ, nq), _row_spec(SEQ, nk), _row_spec(SEQ, nk)],
        out_specs=_row_spec(SEQ, nq),
        out_shape=jax.ShapeDtypeStruct((N_CTX, nq), BF16),
        compiler_params=_params("arbitrary"),
        name="gqa_ctx",
    )(q, k, v)


ONES_ROWS = 16


def _scores_t(q, ks):
    return [lax.dot_general(k, q, (((1,), (1,)), ((), ())), preferred_element_type=F32) for k in ks]


def _softmax_pv_t(scores, vts):
    m = jnp.max(scores[0], axis=0, keepdims=True)
    for s in scores[1:]:
        m = jnp.maximum(m, jnp.max(s, axis=0, keepdims=True))
    acc = None
    for s, vt in zip(scores, vts):
        p = jnp.exp2(s - m).astype(BF16)
        vt_ones = jnp.concatenate([vt, jnp.ones((ONES_ROWS, vt.shape[1]), BF16)], axis=0)
        pv = jnp.dot(vt_ones, p, preferred_element_type=F32)
        acc = pv if acc is None else acc + pv
    d = LANES
    return (acc[:d] * (1.0 / acc[d:d + 1])).T


def _gqa_t_kernel(q_ref, k_ref, vt_ref, kc_ref, vct_ref, o_ref, *, n_kv, group, unit, tq):
    d = HEAD_DIM_A
    units = [(g, [g * group + u0 + j for j in range(unit)]) for g in range(n_kv) for u0 in range(0, group, unit)]
    def unit_scores(g, heads):
        ds = slice(g * d, (g + 1) * d)
        qs = jnp.concatenate([q_ref[:, h * d:(h + 1) * d] for h in heads], axis=0)
        return _scores_t(qs, [k_ref[:, ds], kc_ref[:, ds]])

    nxt = unit_scores(*units[0])
    for idx, (g, heads) in enumerate(units):
        sc = nxt
        if idx + 1 < len(units):
            nxt = unit_scores(*units[idx + 1])
        ds = slice(g * d, (g + 1) * d)
        o = _softmax_pv_t(sc, [vt_ref[ds, :], vct_ref[ds, :]])
        for j, h in enumerate(heads):
            o_ref[:, h * d:(h + 1) * d] = o[j * tq:(j + 1) * tq].astype(o_ref.dtype)


def _gqa_lat_t(q, k, vt, kc, vct):
    nq, nk = N_HEADS_A * HEAD_DIM_A, N_KV_A * HEAD_DIM_A
    tiles = DEC_SEQ // TQ_A
    q_spec = pl.BlockSpec((TQ_A, nq), lambda b, t: (b * tiles + t, 0))
    return pl.pallas_call(
        functools.partial(_gqa_t_kernel, n_kv=N_KV_A, group=N_HEADS_A // N_KV_A, unit=UNIT_A, tq=TQ_A),
        grid=(DEC_BATCH, tiles),
        in_specs=[q_spec,
                  pl.BlockSpec((DEC_SEQ, nk), lambda b, t: (b, 0)),
                  pl.BlockSpec((nk, DEC_SEQ), lambda b, t: (0, b)),
                  pl.BlockSpec((None, PAST_LEN, nk), lambda b, t: (b, 0, 0)),
                  pl.BlockSpec((None, nk, PAST_LEN), lambda b, t: (b, 0, 0))],
        out_specs=q_spec,
        out_shape=jax.ShapeDtypeStruct((N_LAT, nq), BF16),
        compiler_params=_params("arbitrary", "arbitrary"),
        name="gqa_lat",
    )(q, k, vt, kc, vct)


def _split_pair(qp):
    lo = lax.broadcasted_iota(jnp.int32, qp.shape, 1) < HEAD_DIM_B
    zero = jnp.zeros_like(qp)
    return jnp.concatenate([jnp.where(lo, qp, zero), jnp.where(lo, zero, qp)], axis=0)


def _merge_pair(o2, rows):
    lo = lax.broadcasted_iota(jnp.int32, (rows, LANES), 1) < HEAD_DIM_B
    return jnp.where(lo, o2[:rows], o2[rows:])


def _mha_ctx_kernel(q_ref, k_ref, v_ref, o_ref):
    for p in range(N_HEADS_B // 2):
        sl = slice(p * LANES, (p + 1) * LANES)
        o2 = _attend(_split_pair(q_ref[:, sl]), [(k_ref[:, sl], v_ref[:, sl])])
        o_ref[:, sl] = _merge_pair(o2, SEQ).astype(o_ref.dtype)


def _mha_ctx(q, k, v):
    return pl.pallas_call(
        _mha_ctx_kernel,
        grid=(BATCH,),
        in_specs=[_row_spec(SEQ, D_MODEL)] * 3,
        out_specs=_row_spec(SEQ, D_MODEL),
        out_shape=jax.ShapeDtypeStruct((N_CTX, D_MODEL), BF16),
        compiler_params=_params("arbitrary"),
        name="mha_ctx",
    )(q, k, v)


NAT_HALF = D_MODEL
NAT_PAIRS = NAT_HALF // LANES
NAT_R = 4
NAT_WROWS = 12
N_DROW = 2 * WIN_H - 1
MASKED_TILE = N_DROW


def _natten_kernel(q_ref, k_ref, vt_ref, kc_ref, vct_ref, tbl_ref, o_ref):
    r0 = pl.program_id(2) * NAT_R
    ws = jnp.clip(r0 - WIN_H // 2, 0, GRID_H - NAT_WROWS)
    row0 = pl.multiple_of(ws * GRID_W, LANES)
    blk0 = ws * GRID_W // LANES
    n_loc = NAT_WROWS * GRID_W

    def tile_index(a, w):
        r = r0 + a
        start = jnp.clip(r - WIN_H // 2, 0, GRID_H - WIN_H)
        wr = ws + w
        valid = (wr >= start) & (wr < start + WIN_H)
        return jnp.where(valid, wr - r + (WIN_H - 1), MASKED_TILE)

    idx = [[tile_index(a, w) for w in range(NAT_WROWS)] for a in range(NAT_R)]

    def pair_scores(p):
        sl = slice(p * LANES, (p + 1) * LANES)
        q2 = jnp.concatenate([_split_pair(q_ref[a * GRID_W:(a + 1) * GRID_W, sl]) for a in range(NAT_R)], axis=0)
        bias_t = jnp.concatenate(
            [jnp.concatenate([tbl_ref[idx[a][w], p] for w in range(NAT_WROWS)], axis=0) for a in range(NAT_R)],
            axis=1)
        s_loc, s_ctx = _scores_t(q2, [k_ref[pl.ds(row0, n_loc), sl], kc_ref[:, sl]])
        return [s_loc + bias_t, s_ctx]

    nxt = pair_scores(0)
    for p in range(NAT_PAIRS):
        sl = slice(p * LANES, (p + 1) * LANES)
        sc = nxt
        if p + 1 < NAT_PAIRS:
            nxt = pair_scores(p + 1)
        vt_win = jnp.concatenate([vt_ref[blk0 + j, sl, :] for j in range(n_loc // LANES)], axis=1)
        o2 = _softmax_pv_t(sc, [vt_win, vct_ref[sl, :]])
        for a in range(NAT_R):
            o_ref[a * GRID_W:(a + 1) * GRID_W, sl] = _merge_pair(
                o2[a * LANES:(a + 1) * LANES], GRID_W).astype(o_ref.dtype)


def _natten(q, k, vt, kc, vct, tbl):
    n_half = D_MODEL // NAT_HALF
    blocks = GRID_H // NAT_R
    seq_blocks = DEC_SEQ // LANES
    q_spec = pl.BlockSpec((NAT_R * GRID_W, NAT_HALF), lambda b, hh, r: (b * blocks + r, hh))
    return pl.pallas_call(
        _natten_kernel,
        grid=(DEC_BATCH, n_half, blocks),
        in_specs=[q_spec,
                  pl.BlockSpec((DEC_SEQ, NAT_HALF), lambda b, hh, r: (b, hh)),
                  pl.BlockSpec((seq_blocks, NAT_HALF, LANES), lambda b, hh, r: (b, hh, 0)),
                  pl.BlockSpec((None, PAST_LEN, NAT_HALF), lambda b, hh, r: (b, 0, hh)),
                  pl.BlockSpec((None, NAT_HALF, PAST_LEN), lambda b, hh, r: (b, hh, 0)),
                  pl.BlockSpec((N_DROW + 1, NAT_PAIRS, GRID_W, LANES), lambda b, hh, r: (0, hh, 0, 0))],
        out_specs=q_spec,
        out_shape=jax.ShapeDtypeStruct((N_LAT, D_MODEL), BF16),
        compiler_params=_params("arbitrary", "arbitrary", "arbitrary"),
        name="natten",
    )(q, k, vt, kc, vct, tbl)


def _bias_table_kernel(rpb_ref, o_ref):
    dr = pl.program_id(0)
    row = lax.broadcasted_iota(jnp.int32, (GRID_W, LANES), 0)
    col = lax.broadcasted_iota(jnp.int32, (GRID_W, LANES), 1)
    sub = col >= GRID_W
    qc = col % GRID_W
    kc = row
    dc = jnp.clip(kc - qc, -(WIN_W - 1), WIN_W - 1) + (WIN_W - 1)
    col_start = jnp.clip(qc - WIN_W // 2, 0, GRID_W - WIN_W)
    in_win = (kc >= col_start) & (kc < col_start + WIN_W) & (dr < N_DROW)
    n_dc = 2 * WIN_W - 1
    drc = jnp.minimum(dr, N_DROW - 1)
    for pr in range(N_HEADS_B // 2):
        b0 = ((2 * pr) * N_DROW + drc) * n_dc
        b1 = ((2 * pr + 1) * N_DROW + drc) * n_dc

        def body(j, acc, b0=b0, b1=b1):
            return jnp.where(dc == j, jnp.where(sub, rpb_ref[b1 + j], rpb_ref[b0 + j]), acc)

        acc = lax.fori_loop(0, n_dc, body, jnp.zeros((GRID_W, LANES), F32))
        o_ref[pr] = jnp.where(in_win, acc * LOG2_E, NEG_INF)


def _bias_table(rpb):
    return pl.pallas_call(
        _bias_table_kernel,
        grid=(N_DROW + 1,),
        in_specs=[pl.BlockSpec(memory_space=pltpu.SMEM)],
        out_specs=pl.BlockSpec((None, N_HEADS_B // 2, GRID_W, LANES), lambda d: (d, 0, 0, 0)),
        out_shape=jax.ShapeDtypeStruct((N_DROW + 1, N_HEADS_B // 2, GRID_W, LANES), F32),
        compiler_params=_params("arbitrary"),
        name="natten_bias_table",
    )(rpb.reshape(-1))


def _wo_ln_kernel(a_ref, w_ref, x_ref, gate_ref, g_ref, b_ref, o_ref):
    o = jnp.dot(a_ref[...], w_ref[...], preferred_element_type=F32)
    y = ALPHA * x_ref[...] + gate_ref[...] * o
    o_ref[...] = _layer_norm(y, g_ref[...], b_ref[...])


def _wo_ln(stream, a, w, x, gate, ln_g, ln_b):
    vec = _full_spec((1, D_MODEL))
    return pl.pallas_call(
        _wo_ln_kernel,
        grid=(stream.n // TM,),
        in_specs=[_row_spec(TM, a.shape[1]), _full_spec(w.shape), _row_spec(TM, D_MODEL),
                  _mod_spec(stream, TM), vec, vec],
        out_specs=_row_spec(TM, D_MODEL),
        out_shape=jax.ShapeDtypeStruct((stream.n, D_MODEL), F32),
        compiler_params=_params("arbitrary"),
        name="wo_postnorm",
    )(a, w, x, gate, ln_g, ln_b)


def _swiglu(h, wg_ref, wu_ref, wd_ref):
    acc = None
    for lo, hi in FF_CHUNKS:
        g = jnp.dot(h, wg_ref[:, lo:hi], preferred_element_type=F32)
        u = jnp.dot(h, wu_ref[:, lo:hi], preferred_element_type=F32)
        a = (g * jax.nn.sigmoid(g) * u).astype(BF16)
        part = jnp.dot(a, wd_ref[lo:hi, :], preferred_element_type=F32)
        acc = part if acc is None else acc + part
    return acc


def _ffn_dense_kernel(x_ref, sh_ref, sc_ref, wg_ref, wu_ref, wd_ref, gate_ref, g_ref, b_ref, o_ref):
    x = x_ref[...]
    h = (x * (1.0 + sc_ref[...]) + sh_ref[...]).astype(BF16)
    y = ALPHA * x + gate_ref[...] * _swiglu(h, wg_ref, wu_ref, wd_ref)
    o_ref[...] = _layer_norm(y, g_ref[...], b_ref[...])


def _ffn_dense(stream, x, shift, scale, w_gu, w_down, gate, ln_g, ln_b):
    vec = _full_spec((1, D_MODEL))
    mod = _mod_spec(stream, TM_FFN)
    row = _row_spec(TM_FFN, D_MODEL)
    return pl.pallas_call(
        _ffn_dense_kernel,
        grid=(stream.n // TM_FFN,),
        in_specs=[row, mod, mod,
                  pl.BlockSpec((D_MODEL, D_FF), lambda i: (0, 0)),
                  pl.BlockSpec((D_MODEL, D_FF), lambda i: (0, 1)),
                  _full_spec((D_FF, D_MODEL)),
                  mod, vec, vec],
        out_specs=row,
        out_shape=jax.ShapeDtypeStruct((stream.n, D_MODEL), F32),
        compiler_params=_params("arbitrary"),
        name="ffn_dense",
    )(x, shift, scale, w_gu, w_gu, w_down, gate, ln_g, ln_b)


def _ffn_grouped_kernel(te_ref, tv_ref, x_ref, wg_ref, wu_ref, wd_ref, o_ref):
    i = pl.program_id(0)

    @pl.when(tv_ref[i] != 0)
    def _():
        o_ref[...] = _swiglu(x_ref[...].astype(BF16), wg_ref, wu_ref, wd_ref)

    @pl.when(tv_ref[i] == 0)
    def _():
        o_ref[...] = jnp.zeros_like(o_ref)


def _ffn_grouped(tile_expert, tile_valid, xs, w_gu, w_down):
    row = pl.BlockSpec((TMG, D_MODEL), lambda i, te, tv: (i, 0))
    grid_spec = pltpu.PrefetchScalarGridSpec(
        num_scalar_prefetch=2,
        grid=(NT_MOE,),
        in_specs=[row,
                  pl.BlockSpec((None, D_MODEL, D_FF), lambda i, te, tv: (te[i], 0, 0)),
                  pl.BlockSpec((None, D_MODEL, D_FF), lambda i, te, tv: (te[i], 0, 1)),
                  pl.BlockSpec((None, D_FF, D_MODEL), lambda i, te, tv: (te[i], 0, 0))],
        out_specs=row,
    )
    return pl.pallas_call(
        _ffn_grouped_kernel,
        grid_spec=grid_spec,
        out_shape=jax.ShapeDtypeStruct((NT_MOE * TMG, D_MODEL), F32),
        compiler_params=_params("arbitrary"),
        name="ffn_grouped",
    )(tile_expert, tile_valid, xs, w_gu, w_gu, w_down)


META_E0, META_E1, META_W0, META_W1, META_R0, META_R1 = range(6)


def _router_kernel(x_ref, sh_ref, sc_ref, rw_ref, cnt_in_ref, h_ref, meta_ref, cnt_ref, carry_scr):
    i = pl.program_id(0)

    @pl.when(i == 0)
    def _():
        carry_scr[...] = cnt_in_ref[...]

    h = x_ref[...] * (1.0 + sc_ref[...]) + sh_ref[...]
    h_ref[...] = h
    w = rw_ref[...]
    h_hi = h.astype(BF16)
    w_hi = w.astype(BF16)
    h_lo = (h - h_hi.astype(F32)).astype(BF16)
    w_lo = (w - w_hi.astype(F32)).astype(BF16)
    logits = (jnp.dot(h_hi, w_hi, preferred_element_type=F32)
              + (jnp.dot(h_lo, w_hi, preferred_element_type=F32)
                 + jnp.dot(h_hi, w_lo, preferred_element_type=F32)))
    lane = lax.broadcasted_iota(jnp.int32, logits.shape, 1).astype(F32)
    lg = jnp.where(lane < N_EXPERTS, logits, -jnp.inf)
    m1 = jnp.max(lg, axis=-1, keepdims=True)
    i1 = jnp.min(jnp.where(lg == m1, lane, float(LANES)), axis=-1, keepdims=True)
    lg2 = jnp.where(lane == i1, -jnp.inf, lg)
    m2 = jnp.max(lg2, axis=-1, keepdims=True)
    i2 = jnp.min(jnp.where(lg2 == m2, lane, float(LANES)), axis=-1, keepdims=True)
    e = jnp.exp(m2 - m1)
    w1 = 1.0 / (1.0 + e)
    w2 = e / (1.0 + e)

    sel1 = lane == i1
    sel2 = lane == i2
    onehot = jnp.where(sel1 | sel2, 1.0, 0.0)
    rr = lax.broadcasted_iota(jnp.int32, (TM, TM), 0)
    cc = lax.broadcasted_iota(jnp.int32, (TM, TM), 1)
    lower = jnp.where(cc < rr, 1.0, 0.0).astype(BF16)
    before = jnp.dot(lower, onehot.astype(BF16), preferred_element_type=F32) + carry_scr[0:1, :]
    r1 = jnp.sum(jnp.where(sel1, before, 0.0), axis=-1, keepdims=True)
    r2 = jnp.sum(jnp.where(sel2, before, 0.0), axis=-1, keepdims=True)
    carry_scr[...] = carry_scr[...] + jnp.sum(onehot, axis=0, keepdims=True)
    cnt_ref[...] = carry_scr[...]

    cols = [i1, i2, w1, w2, r1, r2]
    meta = jnp.zeros(logits.shape, F32)
    for c, val in enumerate(cols):
        meta = jnp.where(lane == c, val, meta)
    meta_ref[...] = meta


def _router(stream, x, shift, scale, rw_pad, cnt_in):
    cnt_spec = _full_spec((SUBLANES, LANES))
    return pl.pallas_call(
        _router_kernel,
        grid=(stream.n // TM,),
        in_specs=[_row_spec(TM, D_MODEL), _mod_spec(stream, TM), _mod_spec(stream, TM),
                  _full_spec((D_MODEL, LANES)), cnt_spec],
        out_specs=[_row_spec(TM, D_MODEL), _row_spec(TM, LANES), cnt_spec],
        out_shape=[jax.ShapeDtypeStruct((stream.n, D_MODEL), F32), jax.ShapeDtypeStruct((stream.n, LANES), F32),
                   jax.ShapeDtypeStruct((SUBLANES, LANES), F32)],
        scratch_shapes=[pltpu.VMEM((SUBLANES, LANES), F32)],
        compiler_params=_params("arbitrary"),
        name="moe_router",
    )(x, shift, scale, rw_pad, cnt_in)


def _row_copy(src_ref, src_row, dst_ref, dst_row, sem):
    return pltpu.make_async_copy(src_ref.at[pl.ds(src_row, 1)], dst_ref.at[pl.ds(dst_row, 1)], sem)


def _scatter_rows(d0_ref, d1_ref, h_ref, buf_ref, sem):
    base = pl.program_id(0) * TM
    for r in range(TM):
        _row_copy(h_ref, r, buf_ref, d0_ref[base + r], sem).start(priority=0)
        _row_copy(h_ref, r, buf_ref, d1_ref[base + r], sem).start(priority=1)


def _scatter_wait(h_ref, buf_ref, sem, n_rows):
    pltpu.make_async_copy(h_ref.at[pl.ds(0, n_rows)], buf_ref.at[pl.ds(0, n_rows)], sem).wait()


def _scatter_kernel(d0_ref, d1_ref, pad_ref, hc_ref, hl_ref, buf_ref, zero_scr, sem, *, ctx_steps, pads_per_step):
    i = pl.program_id(0)

    @pl.when(i < ctx_steps)
    def _():
        _scatter_rows(d0_ref, d1_ref, hc_ref, buf_ref, sem)
        zero_scr[...] = jnp.zeros_like(zero_scr)
        pbase = i * pads_per_step
        for r in range(pads_per_step):
            _row_copy(zero_scr, 0, buf_ref, pad_ref[pbase + r], sem).start(priority=r % 2)
        _scatter_wait(hc_ref, buf_ref, sem, TM)
        _scatter_wait(hc_ref, buf_ref, sem, TM)
        _scatter_wait(hc_ref, buf_ref, sem, pads_per_step)

    @pl.when(i >= ctx_steps)
    def _():
        _scatter_rows(d0_ref, d1_ref, hl_ref, buf_ref, sem)
        _scatter_wait(hl_ref, buf_ref, sem, TM)
        _scatter_wait(hl_ref, buf_ref, sem, TM)


def _scatter(d0, d1, pad_rows, h_ctx, h_lat):
    ctx_steps = CTX.n // TM
    steps = ctx_steps + LAT.n // TM
    assert N_PAD_ROWS % ctx_steps == 0
    grid_spec = pltpu.PrefetchScalarGridSpec(
        num_scalar_prefetch=3,
        grid=(steps,),
        in_specs=[pl.BlockSpec((TM, D_MODEL), lambda i, a, b, c: (jnp.minimum(i, ctx_steps - 1), 0)),
                  pl.BlockSpec((TM, D_MODEL), lambda i, a, b, c: (jnp.maximum(i - ctx_steps, 0), 0))],
        out_specs=pl.BlockSpec(memory_space=pl.ANY),
        scratch_shapes=[pltpu.VMEM((SUBLANES, D_MODEL), F32), pltpu.SemaphoreType.DMA],
    )
    return pl.pallas_call(
        functools.partial(_scatter_kernel, ctx_steps=ctx_steps, pads_per_step=N_PAD_ROWS // ctx_steps),
        grid_spec=grid_spec,
        out_shape=jax.ShapeDtypeStruct((NT_MOE * TMG, D_MODEL), F32),
        compiler_params=_params("arbitrary"),
        name="moe_scatter",
    )(d0, d1, pad_rows, h_ctx, h_lat)


def _gather_start(d0_ref, d1_ref, ys_ref, rows_scr, sems, tile, slot):
    base = tile * TM
    for r in range(TM):
        _row_copy(ys_ref, d0_ref[base + r], rows_scr.at[slot, 0], r, sems.at[slot]).start(priority=0)
        _row_copy(ys_ref, d1_ref[base + r], rows_scr.at[slot, 1], r, sems.at[slot]).start(priority=1)


def _gather_wait(ys_ref, rows_scr, sems, slot):
    for s in range(2):
        pltpu.make_async_copy(ys_ref.at[pl.ds(0, TM)], rows_scr.at[slot, s], sems.at[slot]).wait()


def _combine_kernel(d0_ref, d1_ref, ys_ref, meta_ref, x_ref, gate_ref, g_ref, b_ref, o_ref, rows_scr, sems,
                    *, n_tiles):
    i = pl.program_id(0)
    slot = i % 2

    @pl.when(i == 0)
    def _():
        _gather_start(d0_ref, d1_ref, ys_ref, rows_scr, sems, 0, 0)

    _gather_start(d0_ref, d1_ref, ys_ref, rows_scr, sems, jnp.minimum(i + 1, n_tiles - 1), 1 - slot)
    _gather_wait(ys_ref, rows_scr, sems, slot)

    meta = meta_ref[...]
    w0 = meta[:, META_W0:META_W0 + 1]
    w1 = meta[:, META_W1:META_W1 + 1]
    f = w0 * rows_scr[slot, 0] + w1 * rows_scr[slot, 1]
    y = ALPHA * x_ref[...] + gate_ref[...] * f
    o_ref[...] = _layer_norm(y, g_ref[...], b_ref[...])

    @pl.when(i == n_tiles - 1)
    def _():
        _gather_wait(ys_ref, rows_scr, sems, 1 - slot)


def _combine(stream, d0, d1, ys, meta, x, gate, ln_g, ln_b):
    vec = pl.BlockSpec((1, D_MODEL), lambda i, a, b: (0, 0))
    grid_spec = pltpu.PrefetchScalarGridSpec(
        num_scalar_prefetch=2,
        grid=(stream.n // TM,),
        in_specs=[pl.BlockSpec(memory_space=pl.ANY), _row_spec(TM, LANES), _row_spec(TM, D_MODEL),
                  _mod_spec(stream, TM), vec, vec],
        out_specs=_row_spec(TM, D_MODEL),
        scratch_shapes=[pltpu.VMEM((2, 2, TM, D_MODEL), F32), pltpu.SemaphoreType.DMA((2,))],
    )
    return pl.pallas_call(
        functools.partial(_combine_kernel, n_tiles=stream.n // TM),
        grid_spec=grid_spec,
        out_shape=jax.ShapeDtypeStruct((stream.n, D_MODEL), F32),
        compiler_params=_params("arbitrary"),
        name="moe_combine",
    )(d0, d1, ys, meta, x, gate, ln_g, ln_b)


def _rope_tables():
    t = jnp.arange(DEC_SEQ)
    row = (t // GRID_W).astype(F32)
    col = (t % GRID_W).astype(F32)
    freqs = ROPE_THETA ** (-jnp.arange(ROT_FREQS, dtype=F32) / ROT_FREQS)
    ar = row[:, None] * freqs
    ac = col[:, None] * freqs
    cos = jnp.concatenate([jnp.cos(ar), jnp.cos(ar), jnp.cos(ac), jnp.cos(ac)], axis=1)
    sin = jnp.concatenate([-jnp.sin(ar), jnp.sin(ar), -jnp.sin(ac), jnp.sin(ac)], axis=1)
    return cos, sin


def _routing_plan(metas, counts):
    cnt = counts[0, :N_EXPERTS].astype(jnp.int32)
    tiles_e = (cnt + TMG - 1) // TMG
    tile_end = jnp.cumsum(tiles_e)
    tile_start = tile_end - tiles_e
    offs = tile_start * TMG
    dests = []
    for meta in metas:
        pair = []
        for ecol, rcol in ((META_E0, META_R0), (META_E1, META_R1)):
            e = meta[:, ecol].astype(jnp.int32)
            d = meta[:, rcol].astype(jnp.int32)
            for k in range(N_EXPERTS):
                d = d + jnp.where(e == k, offs[k], 0)
            pair.append(d)
        dests.append(tuple(pair))
    tid = jnp.arange(NT_MOE, dtype=jnp.int32)
    te = jnp.minimum(jnp.sum((tid[:, None] >= tile_end[None, :]).astype(jnp.int32), axis=1), N_EXPERTS - 1)
    total = tile_end[-1]
    valid = tid < total
    te_last = jnp.max(jnp.where(valid, te, 0))
    te = jnp.where(valid, te, te_last)
    pad_cnt = tiles_e * TMG - cnt
    pad_end = jnp.cumsum(pad_cnt)
    pad_start = pad_end - pad_cnt
    k = jnp.arange(N_PAD_ROWS, dtype=jnp.int32)
    grp = jnp.sum((k[:, None] >= pad_end[None, :]).astype(jnp.int32), axis=1)
    pad_rows = total * TMG + (k - pad_end[-1])
    for e in range(N_EXPERTS):
        pad_rows = jnp.where(grp == e, offs[e] + cnt[e] + (k - pad_start[e]), pad_rows)
    return dests, te, valid.astype(jnp.int32), pad_rows


def kernel(x_prompt, x_sample, cache_k_a, cache_v_a, cache_k_b, cache_v_b, c, c_ctx, ada_w, ada_b, ln_attn_g, ln_attn_b, ln_ffn_g, ln_ffn_b, wqkv_a, qnorm_a, knorm_a, wo_a, wqkv_b, rpb_b, wo_b, ffn_w_gu, ffn_w_down, router_w, moe_w_gu, moe_w_down):
    streams = (CTX, LAT)
    xs = [x_prompt.reshape(N_CTX, D_MODEL), x_sample.reshape(N_LAT, D_MODEL)]

    cond = jnp.zeros((GROUP_PAD, D_MODEL), F32).at[0].set(c_ctx).at[1:N_GROUPS].set(c)
    mods = _ada_mods(cond, ada_w, ada_b)
    vec = lambda a, l: a[l].reshape(1, D_MODEL)

    m = mods[0]
    w_qkv = wqkv_a[0].astype(BF16)
    w_o = wo_a[0].astype(BF16)
    w_gu = ffn_w_gu[0].astype(BF16)
    w_dn = ffn_w_down[0].astype(BF16)
    gains = (qnorm_a[0].reshape(1, HEAD_DIM_A), knorm_a[0].reshape(1, HEAD_DIM_A))
    nq, nk = N_HEADS_A * HEAD_DIM_A, N_KV_A * HEAD_DIM_A
    qscale = HEAD_DIM_A ** -0.5

    qp, kp, vp, kp32, vp32 = _qkv(CTX, xs[0], m[0], m[1], w_qkv, nq=nq, nk=nk, qscale=qscale, gains=gains,
                                  emit_f32=True)
    new_k_a = kp32.reshape(BATCH, 1, SEQ, N_KV_A, HEAD_DIM_A)
    new_v_a = vp32.reshape(BATCH, 1, SEQ, N_KV_A, HEAD_DIM_A)
    ql, kl, vlt = _qkv(LAT, xs[1], m[0], m[1], w_qkv, nq=nq, nk=nk, qscale=qscale * LOG2_E, gains=gains,
                       rope_tables=_rope_tables(), v_layout="t")
    cache_k = cache_k_a[:, 0].reshape(DEC_BATCH, PAST_LEN, nk).astype(BF16)
    cache_vt = jnp.swapaxes(cache_v_a[:, 0].reshape(DEC_BATCH, PAST_LEN, nk), 1, 2).astype(BF16)
    attn = [_gqa_ctx(qp, kp, vp), _gqa_lat_t(ql, kl, vlt, cache_k, cache_vt)]
    xs = [_wo_ln(s, a, w_o, x, m[2], vec(ln_attn_g, 0), vec(ln_attn_b, 0)) for s, a, x in zip(streams, attn, xs)]
    xs = [_ffn_dense(s, x, m[3], m[4], w_gu, w_dn, m[5], vec(ln_ffn_g, 0), vec(ln_ffn_b, 0))
          for s, x in zip(streams, xs)]

    m = mods[1]
    w_qkv = wqkv_b[0].astype(BF16)
    w_o = wo_b[0].astype(BF16)
    qscale = HEAD_DIM_B ** -0.5
    qp, kp, vp, kp32, vp32 = _qkv(CTX, xs[0], m[0], m[1], w_qkv, nq=D_MODEL, nk=D_MODEL, qscale=qscale,
                                  emit_f32=True)
    new_k_b = kp32.reshape(BATCH, 1, SEQ, N_HEADS_B, HEAD_DIM_B)
    new_v_b = vp32.reshape(BATCH, 1, SEQ, N_HEADS_B, HEAD_DIM_B)
    ql, kl, vlt = _qkv(LAT, xs[1], m[0], m[1], w_qkv, nq=D_MODEL, nk=D_MODEL, qscale=qscale * LOG2_E,
                       v_layout="t_blocked")
    cache_k = cache_k_b[:, 0].reshape(DEC_BATCH, PAST_LEN, D_MODEL).astype(BF16)
    cache_vt = jnp.swapaxes(cache_v_b[:, 0].reshape(DEC_BATCH, PAST_LEN, D_MODEL), 1, 2).astype(BF16)
    attn = [_mha_ctx(qp, kp, vp), _natten(ql, kl, vlt, cache_k, cache_vt, _bias_table(rpb_b[0]))]
    xs = [_wo_ln(s, a, w_o, x, m[2], vec(ln_attn_g, 1), vec(ln_attn_b, 1)) for s, a, x in zip(streams, attn, xs)]

    rw_pad = jnp.zeros((D_MODEL, LANES), F32).at[:, :N_EXPERTS].set(router_w[0])
    counts = jnp.zeros((SUBLANES, LANES), F32)
    hs, metas = [], []
    for s, x in zip(streams, xs):
        h, meta, counts = _router(s, x, m[3], m[4], rw_pad, counts)
        hs.append(h)
        metas.append(meta)
    dests, tile_expert, tile_valid, pad_rows = _routing_plan(metas, counts)
    d0_all = jnp.concatenate([dests[0][0], dests[1][0]])
    d1_all = jnp.concatenate([dests[0][1], dests[1][1]])
    sorted_rows = _scatter(d0_all, d1_all, pad_rows, hs[0], hs[1])
    ys = _ffn_grouped(tile_expert, tile_valid, sorted_rows, moe_w_gu[0].astype(BF16), moe_w_down[0].astype(BF16))
    outs = [_combine(s, d0, d1, ys, meta, x, m[5], vec(ln_ffn_g, 1), vec(ln_ffn_b, 1))
            for s, (d0, d1), meta, x in zip(streams, dests, metas, xs)]

    y_prompt = outs[0].reshape(BATCH, SEQ, D_MODEL)
    y_sample = outs[1].reshape(DEC_BATCH, DEC_SEQ, D_MODEL)
    return (y_prompt, y_sample, new_k_a, new_v_a, new_k_b, new_v_b)
```

```python
import functools
import math

import jax
import jax.numpy as jnp
from jax import lax
from jax.experimental import pallas as pl
from jax.experimental.pallas import tpu as pltpu

F32 = jnp.float32
BF16 = jnp.bfloat16
HIGHEST = lax.Precision.HIGHEST

D_MODEL = 1024
BATCH, SEQ = 32, 256
DEC_BATCH, DEC_SEQ = 4, 4096
PAST_LEN = 256
DEPTH = 2
GRID_W = 64
GRID_H = DEC_SEQ // GRID_W
N_HEADS_A, N_KV_A, HEAD_DIM_A = 8, 2, 128
ROT_FREQS = HEAD_DIM_A // 4
ROPE_THETA = 10000.0
N_HEADS_B, HEAD_DIM_B = 16, 64
WIN_H, WIN_W = 8, 16
D_FF = 2816
N_EXPERTS = 8
EPS = 1e-6
NEG_INF = -1e30
ALPHA = (2.0 * DEPTH) ** 0.25
LOG2_E = math.log2(math.e)

N_CTX = BATCH * SEQ
N_LAT = DEC_BATCH * DEC_SEQ
N_GROUPS = 1 + DEC_BATCH
GROUP_PAD = 8

LANES = 128
SUBLANES = 8
VMEM_LIMIT = 56 * 2**20

TM = 512
MXU_DIM = 256
FF_CHUNKS = ((0, 1024), (1024, 2048), (2048, D_FF))
TM_FFN = 512
TMG = 512
TQ_A = 256
UNIT_A = 2
TN_ADA = 1536
N_PAIRS = 2 * (N_CTX + N_LAT)
NT_MOE = N_PAIRS // TMG + N_EXPERTS
N_PAD_ROWS = NT_MOE * TMG - N_PAIRS
TOKEN_TILE_ROWS = D_MODEL // LANES
assert TOKEN_TILE_ROWS == SUBLANES


class Stream:
    def __init__(self, n_rows, group_offset, rows_per_group):
        self.n = n_rows
        self.goff = group_offset
        self.rpg = rows_per_group

    def group_map(self, tm):
        tiles_per_group = self.rpg // tm
        goff = self.goff
        return lambda i, *_: (goff + i // tiles_per_group, 0, 0)


CTX = Stream(N_CTX, 0, N_CTX)
LAT = Stream(N_LAT, 1, DEC_SEQ)


def _params(*sem):
    return pltpu.CompilerParams(dimension_semantics=sem, vmem_limit_bytes=VMEM_LIMIT)


def _mod_spec(stream, tm):
    return pl.BlockSpec((None, 1, D_MODEL), stream.group_map(tm))


def _row_spec(tm, width):
    return pl.BlockSpec((tm, width), lambda i, *_: (i, 0))


def _full_spec(shape):
    nd = len(shape)
    return pl.BlockSpec(shape, lambda *_: (0,) * nd)


def _layer_norm(y, g, b):
    mu = jnp.mean(y, axis=-1, keepdims=True)
    d = y - mu
    var = jnp.mean(d * d, axis=-1, keepdims=True)
    return d * lax.rsqrt(var + EPS) * g + b


def _ada_kernel(c_ref, w_ref, b_ref, o_ref):
    c = c_ref[...]
    s = c * jax.nn.sigmoid(c)
    o_ref[...] = jnp.dot(s, w_ref[...], preferred_element_type=F32, precision=HIGHEST) + b_ref[...]


def _ada_mods(cond, ada_w, ada_b):
    n_out = 6 * D_MODEL
    out = pl.pallas_call(
        _ada_kernel,
        grid=(DEPTH, n_out // TN_ADA),
        in_specs=[
            pl.BlockSpec((GROUP_PAD, D_MODEL), lambda l, n: (0, 0)),
            pl.BlockSpec((None, D_MODEL, TN_ADA), lambda l, n: (l, 0, n)),
            pl.BlockSpec((None, 1, TN_ADA), lambda l, n: (l, 0, n)),
        ],
        out_specs=pl.BlockSpec((None, GROUP_PAD, TN_ADA), lambda l, n: (l, 0, n)),
        out_shape=jax.ShapeDtypeStruct((DEPTH, GROUP_PAD, n_out), F32),
        compiler_params=_params("arbitrary", "arbitrary"),
        name="ada_mods",
    )(cond, ada_w, ada_b.reshape(DEPTH, 1, n_out))
    out = out.reshape(DEPTH, GROUP_PAD, 6, D_MODEL).transpose(0, 2, 1, 3)
    return out[:, :, :, None, :]


def _swap_halves(t):
    lane = lax.broadcasted_iota(jnp.int32, t.shape, 1)
    fwd = pltpu.roll(t, LANES - ROT_FREQS, 1)
    bwd = pltpu.roll(t, ROT_FREQS, 1)
    return jnp.where((lane % (2 * ROT_FREQS)) < ROT_FREQS, fwd, bwd)


def _qkv_kernel(*refs, nq, nk, norm, rope, emit_f32, v_layout, qscale):
    refs = list(refs)
    x_ref, sh_ref, sc_ref, w_ref = refs[:4]
    pos = 4
    if norm:
        qg_ref, kg_ref = refs[pos:pos + 2]
        pos += 2
    if rope:
        cos_ref, sin_ref = refs[pos:pos + 2]
        pos += 2
    q_ref, k_ref, v_ref = refs[pos:pos + 3]
    pos += 3
    if emit_f32:
        kf_ref, vf_ref = refs[pos:pos + 2]

    h = (x_ref[...] * (1.0 + sc_ref[...]) + sh_ref[...]).astype(BF16)
    qkv = jnp.dot(h, w_ref[...], preferred_element_type=F32)
    if norm:
        n_heads = (nq + nk) // HEAD_DIM_A
        for hd in range(n_heads):
            lo = hd * HEAD_DIM_A
            t = qkv[:, lo:lo + HEAD_DIM_A]
            ms = jnp.mean(t * t, axis=-1, keepdims=True)
            gain = qg_ref[...] if lo < nq else kg_ref[...]
            t = t * lax.rsqrt(ms + EPS) * gain
            if rope:
                t = t * cos_ref[...] + _swap_halves(t) * sin_ref[...]
            if lo < nq:
                q_ref[:, lo:lo + HEAD_DIM_A] = (t * qscale).astype(BF16)
            else:
                k_ref[:, lo - nq:lo - nq + HEAD_DIM_A] = t.astype(BF16)
                if emit_f32:
                    kf_ref[:, lo - nq:lo - nq + HEAD_DIM_A] = t
    else:
        q_ref[...] = (qkv[:, :nq] * qscale).astype(BF16)
        k = qkv[:, nq:nq + nk]
        k_ref[...] = k.astype(BF16)
        if emit_f32:
            kf_ref[...] = k
    v = qkv[:, nq + nk:]
    if v_layout == "rows":
        v_ref[...] = v.astype(BF16)
    elif v_layout == "t":
        v_ref[...] = v.T.astype(BF16)
    else:
        vt = v.T.astype(BF16)
        for j in range(vt.shape[1] // LANES):
            v_ref[j] = vt[:, j * LANES:(j + 1) * LANES]
    if emit_f32:
        vf_ref[...] = v


def _qkv(stream, x, shift, scale, w, *, nq, nk, qscale, gains=None, rope_tables=None, emit_f32=False,
         v_layout="rows"):
    n = stream.n
    nw = w.shape[1]
    norm = gains is not None
    rope = rope_tables is not None
    in_specs = [_row_spec(TM, D_MODEL), _mod_spec(stream, TM), _mod_spec(stream, TM), _full_spec((D_MODEL, nw))]
    args = [x, shift, scale, w]
    if norm:
        in_specs += [_full_spec((1, HEAD_DIM_A))] * 2
        args += list(gains)
    if rope:
        tiles_per_seq = DEC_SEQ // TM
        tbl_spec = pl.BlockSpec((TM, HEAD_DIM_A), lambda i: (i % tiles_per_seq, 0))
        in_specs += [tbl_spec, tbl_spec]
        args += list(rope_tables)
    out_specs = [_row_spec(TM, nq), _row_spec(TM, nk), _row_spec(TM, nk)]
    out_shape = [jax.ShapeDtypeStruct((n, nq), BF16), jax.ShapeDtypeStruct((n, nk), BF16),
                 jax.ShapeDtypeStruct((n, nk), BF16)]
    if v_layout == "t":
        out_specs[2] = pl.BlockSpec((nk, TM), lambda i: (0, i))
        out_shape[2] = jax.ShapeDtypeStruct((nk, n), BF16)
    elif v_layout == "t_blocked":
        out_specs[2] = pl.BlockSpec((TM // LANES, nk, LANES), lambda i: (i, 0, 0))
        out_shape[2] = jax.ShapeDtypeStruct((n // LANES, nk, LANES), BF16)
    if emit_f32:
        out_specs += [_row_spec(TM, nk), _row_spec(TM, nk)]
        out_shape += [jax.ShapeDtypeStruct((n, nk), F32)] * 2
    return pl.pallas_call(
        functools.partial(_qkv_kernel, nq=nq, nk=nk, norm=norm, rope=rope, emit_f32=emit_f32,
                          v_layout=v_layout, qscale=qscale),
        grid=(n // TM,),
        in_specs=in_specs,
        out_specs=out_specs,
        out_shape=out_shape,
        compiler_params=_params("arbitrary"),
        name="qkv_norm_rope" if norm else "qkv",
    )(*args)


def _attend(q, kvs, biases=None):
    scores = []
    for idx, (k, _) in enumerate(kvs):
        s = lax.dot_general(q, k, (((1,), (1,)), ((), ())), preferred_element_type=F32)
        if biases is not None and biases[idx] is not None:
            s = s + biases[idx]
        scores.append(s)
    m = jnp.max(scores[0], axis=-1, keepdims=True)
    for s in scores[1:]:
        m = jnp.maximum(m, jnp.max(s, axis=-1, keepdims=True))
    denom = None
    out = None
    for s, (_, v) in zip(scores, kvs):
        p = jnp.exp(s - m)
        part = jnp.sum(p, axis=-1, keepdims=True)
        pv = jnp.dot(p.astype(BF16), v, preferred_element_type=F32)
        denom = part if denom is None else denom + part
        out = pv if out is None else out + pv
    return out * (1.0 / denom)


def _gqa_ctx_kernel(q_ref, k_ref, v_ref, o_ref):
    d = HEAD_DIM_A
    group = N_HEADS_A // N_KV_A
    for g in range(N_KV_A):
        heads = [g * group + j for j in range(group)]
        qs = jnp.concatenate([q_ref[:, h * d:(h + 1) * d] for h in heads], axis=0)
        o = _attend(qs, [(k_ref[:, g * d:(g + 1) * d], v_ref[:, g * d:(g + 1) * d])])
        for j, h in enumerate(heads):
            o_ref[:, h * d:(h + 1) * d] = o[j * SEQ:(j + 1) * SEQ].astype(o_ref.dtype)


def _gqa_ctx(q, k, v):
    nq, nk = N_HEADS_A * HEAD_DIM_A, N_KV_A * HEAD_DIM_A
    return pl.pallas_call(
        _gqa_ctx_kernel,
        grid=(BATCH,),
        in_specs=[_row_spec(SEQ, nq), _row_spec(SEQ, nk), _row_spec(SEQ, nk)],
        out_specs=_row_spec(SEQ, nq),
        out_shape=jax.ShapeDtypeStruct((N_CTX, nq), BF16),
        compiler_params=_params("arbitrary"),
        name="gqa_ctx",
    )(q, k, v)


ONES_ROWS = 16


def _scores_t(q, ks):
    return [lax.dot_general(k, q, (((1,), (1,)), ((), ())), preferred_element_type=F32) for k in ks]


def _softmax_pv_t(scores, vts):
    m = jnp.max(scores[0], axis=0, keepdims=True)
    for s in scores[1:]:
        m = jnp.maximum(m, jnp.max(s, axis=0, keepdims=True))
    acc = None
    for s, vt in zip(scores, vts):
        p = jnp.exp2(s - m).astype(BF16)
        vt_ones = jnp.concatenate([vt, jnp.ones((ONES_ROWS, vt.shape[1]), BF16)], axis=0)
        pv = jnp.dot(vt_ones, p, preferred_element_type=F32)
        acc = pv if acc is None else acc + pv
    d = LANES
    return (acc[:d] * (1.0 / acc[d:d + 1])).T


def _gqa_t_kernel(q_ref, k_ref, vt_ref, kc_ref, vct_ref, o_ref, *, n_kv, group, unit, tq):
    d = HEAD_DIM_A
    units = [(g, [g * group + u0 + j for j in range(unit)]) for g in range(n_kv) for u0 in range(0, group, unit)]
    def unit_scores(g, heads):
        ds = slice(g * d, (g + 1) * d)
        qs = jnp.concatenate([q_ref[:, h * d:(h + 1) * d] for h in heads], axis=0)
        return _scores_t(qs, [k_ref[:, ds], kc_ref[:, ds]])

    nxt = unit_scores(*units[0])
    for idx, (g, heads) in enumerate(units):
        sc = nxt
        if idx + 1 < len(units):
            nxt = unit_scores(*units[idx + 1])
        ds = slice(g * d, (g + 1) * d)
        o = _softmax_pv_t(sc, [vt_ref[ds, :], vct_ref[ds, :]])
        for j, h in enumerate(heads):
            o_ref[:, h * d:(h + 1) * d] = o[j * tq:(j + 1) * tq].astype(o_ref.dtype)


def _gqa_lat_t(q, k, vt, kc, vct):
    nq, nk = N_HEADS_A * HEAD_DIM_A, N_KV_A * HEAD_DIM_A
    tiles = DEC_SEQ // TQ_A
    q_spec = pl.BlockSpec((TQ_A, nq), lambda b, t: (b * tiles + t, 0))
    return pl.pallas_call(
        functools.partial(_gqa_t_kernel, n_kv=N_KV_A, group=N_HEADS_A // N_KV_A, unit=UNIT_A, tq=TQ_A),
        grid=(DEC_BATCH, tiles),
        in_specs=[q_spec,
                  pl.BlockSpec((DEC_SEQ, nk), lambda b, t: (b, 0)),
                  pl.BlockSpec((nk, DEC_SEQ), lambda b, t: (0, b)),
                  pl.BlockSpec((None, PAST_LEN, nk), lambda b, t: (b, 0, 0)),
                  pl.BlockSpec((None, nk, PAST_LEN), lambda b, t: (b, 0, 0))],
        out_specs=q_spec,
        out_shape=jax.ShapeDtypeStruct((N_LAT, nq), BF16),
        compiler_params=_params("arbitrary", "arbitrary"),
        name="gqa_lat",
    )(q, k, vt, kc, vct)


def _split_pair(qp):
    lo = lax.broadcasted_iota(jnp.int32, qp.shape, 1) < HEAD_DIM_B
    zero = jnp.zeros_like(qp)
    return jnp.concatenate([jnp.where(lo, qp, zero), jnp.where(lo, zero, qp)], axis=0)


def _merge_pair(o2, rows):
    lo = lax.broadcasted_iota(jnp.int32, (rows, LANES), 1) < HEAD_DIM_B
    return jnp.where(lo, o2[:rows], o2[rows:])


def _mha_ctx_kernel(q_ref, k_ref, v_ref, o_ref):
    for p in range(N_HEADS_B // 2):
        sl = slice(p * LANES, (p + 1) * LANES)
        o2 = _attend(_split_pair(q_ref[:, sl]), [(k_ref[:, sl], v_ref[:, sl])])
        o_ref[:, sl] = _merge_pair(o2, SEQ).astype(o_ref.dtype)


def _mha_ctx(q, k, v):
    return pl.pallas_call(
        _mha_ctx_kernel,
        grid=(BATCH,),
        in_specs=[_row_spec(SEQ, D_MODEL)] * 3,
        out_specs=_row_spec(SEQ, D_MODEL),
        out_shape=jax.ShapeDtypeStruct((N_CTX, D_MODEL), BF16),
        compiler_params=_params("arbitrary"),
        name="mha_ctx",
    )(q, k, v)


NAT_HALF = D_MODEL
NAT_PAIRS = NAT_HALF // LANES
NAT_R = 4
NAT_WROWS = 12
N_DROW = 2 * WIN_H - 1
MASKED_TILE = N_DROW


def _natten_kernel(q_ref, k_ref, vt_ref, kc_ref, vct_ref, tbl_ref, o_ref):
    r0 = pl.program_id(2) * NAT_R
    ws = jnp.clip(r0 - WIN_H // 2, 0, GRID_H - NAT_WROWS)
    row0 = pl.multiple_of(ws * GRID_W, LANES)
    blk0 = ws * GRID_W // LANES
    n_loc = NAT_WROWS * GRID_W

    def tile_index(a, w):
        r = r0 + a
        start = jnp.clip(r - WIN_H // 2, 0, GRID_H - WIN_H)
        wr = ws + w
        valid = (wr >= start) & (wr < start + WIN_H)
        return jnp.where(valid, wr - r + (WIN_H - 1), MASKED_TILE)

    idx = [[tile_index(a, w) for w in range(NAT_WROWS)] for a in range(NAT_R)]

    def pair_scores(p):
        sl = slice(p * LANES, (p + 1) * LANES)
        q2 = jnp.concatenate([_split_pair(q_ref[a * GRID_W:(a + 1) * GRID_W, sl]) for a in range(NAT_R)], axis=0)
        bias_t = jnp.concatenate(
            [jnp.concatenate([tbl_ref[idx[a][w], p] for w in range(NAT_WROWS)], axis=0) for a in range(NAT_R)],
            axis=1)
        s_loc, s_ctx = _scores_t(q2, [k_ref[pl.ds(row0, n_loc), sl], kc_ref[:, sl]])
        return [s_loc + bias_t, s_ctx]

    nxt = pair_scores(0)
    for p in range(NAT_PAIRS):
        sl = slice(p * LANES, (p + 1) * LANES)
        sc = nxt
        if p + 1 < NAT_PAIRS:
            nxt = pair_scores(p + 1)
        vt_win = jnp.concatenate([vt_ref[blk0 + j, sl, :] for j in range(n_loc // LANES)], axis=1)
        o2 = _softmax_pv_t(sc, [vt_win, vct_ref[sl, :]])
        for a in range(NAT_R):
            o_ref[a * GRID_W:(a + 1) * GRID_W, sl] = _merge_pair(
                o2[a * LANES:(a + 1) * LANES], GRID_W).astype(o_ref.dtype)


def _natten(q, k, vt, kc, vct, tbl):
    n_half = D_MODEL // NAT_HALF
    blocks = GRID_H // NAT_R
    seq_blocks = DEC_SEQ // LANES
    q_spec = pl.BlockSpec((NAT_R * GRID_W, NAT_HALF), lambda b, hh, r: (b * blocks + r, hh))
    return pl.pallas_call(
        _natten_kernel,
        grid=(DEC_BATCH, n_half, blocks),
        in_specs=[q_spec,
                  pl.BlockSpec((DEC_SEQ, NAT_HALF), lambda b, hh, r: (b, hh)),
                  pl.BlockSpec((seq_blocks, NAT_HALF, LANES), lambda b, hh, r: (b, hh, 0)),
                  pl.BlockSpec((None, PAST_LEN, NAT_HALF), lambda b, hh, r: (b, 0, hh)),
                  pl.BlockSpec((None, NAT_HALF, PAST_LEN), lambda b, hh, r: (b, hh, 0)),
                  pl.BlockSpec((N_DROW + 1, NAT_PAIRS, GRID_W, LANES), lambda b, hh, r: (0, hh, 0, 0))],
        out_specs=q_spec,
        out_shape=jax.ShapeDtypeStruct((N_LAT, D_MODEL), BF16),
        compiler_params=_params("arbitrary", "arbitrary", "arbitrary"),
        name="natten",
    )(q, k, vt, kc, vct, tbl)


def _bias_table_kernel(rpb_ref, o_ref):
    dr = pl.program_id(0)
    row = lax.broadcasted_iota(jnp.int32, (GRID_W, LANES), 0)
    col = lax.broadcasted_iota(jnp.int32, (GRID_W, LANES), 1)
    sub = col >= GRID_W
    qc = col % GRID_W
    kc = row
    dc = jnp.clip(kc - qc, -(WIN_W - 1), WIN_W - 1) + (WIN_W - 1)
    col_start = jnp.clip(qc - WIN_W // 2, 0, GRID_W - WIN_W)
    in_win = (kc >= col_start) & (kc < col_start + WIN_W) & (dr < N_DROW)
    n_dc = 2 * WIN_W - 1
    drc = jnp.minimum(dr, N_DROW - 1)
    for pr in range(N_HEADS_B // 2):
        b0 = ((2 * pr) * N_DROW + drc) * n_dc
        b1 = ((2 * pr + 1) * N_DROW + drc) * n_dc

        def body(j, acc, b0=b0, b1=b1):
            return jnp.where(dc == j, jnp.where(sub, rpb_ref[b1 + j], rpb_ref[b0 + j]), acc)

        acc = lax.fori_loop(0, n_dc, body, jnp.zeros((GRID_W, LANES), F32))
        o_ref[pr] = jnp.where(in_win, acc * LOG2_E, NEG_INF)


def _bias_table(rpb):
    return pl.pallas_call(
        _bias_table_kernel,
        grid=(N_DROW + 1,),
        in_specs=[pl.BlockSpec(memory_space=pltpu.SMEM)],
        out_specs=pl.BlockSpec((None, N_HEADS_B // 2, GRID_W, LANES), lambda d: (d, 0, 0, 0)),
        out_shape=jax.ShapeDtypeStruct((N_DROW + 1, N_HEADS_B // 2, GRID_W, LANES), F32),
        compiler_params=_params("arbitrary"),
        name="natten_bias_table",
    )(rpb.reshape(-1))


def _wo_ln_kernel(a_ref, w_ref, x_ref, gate_ref, g_ref, b_ref, o_ref):
    o = jnp.dot(a_ref[...], w_ref[...], preferred_element_type=F32)
    y = ALPHA * x_ref[...] + gate_ref[...] * o
    o_ref[...] = _layer_norm(y, g_ref[...], b_ref[...])


def _wo_ln(stream, a, w, x, gate, ln_g, ln_b):
    vec = _full_spec((1, D_MODEL))
    return pl.pallas_call(
        _wo_ln_kernel,
        grid=(stream.n // TM,),
        in_specs=[_row_spec(TM, a.shape[1]), _full_spec(w.shape), _row_spec(TM, D_MODEL),
                  _mod_spec(stream, TM), vec, vec],
        out_specs=_row_spec(TM, D_MODEL),
        out_shape=jax.ShapeDtypeStruct((stream.n, D_MODEL), F32),
        compiler_params=_params("arbitrary"),
        name="wo_postnorm",
    )(a, w, x, gate, ln_g, ln_b)


def _swiglu(h, wg_ref, wu_ref, wd_ref):
    acc = None
    for lo, hi in FF_CHUNKS:
        g = jnp.dot(h, wg_ref[:, lo:hi], preferred_element_type=F32)
        u = jnp.dot(h, wu_ref[:, lo:hi], preferred_element_type=F32)
        a = (g * jax.nn.sigmoid(g) * u).astype(BF16)
        part = jnp.dot(a, wd_ref[lo:hi, :], preferred_element_type=F32)
        acc = part if acc is None else acc + part
    return acc


def _ffn_dense_kernel(x_ref, sh_ref, sc_ref, wg_ref, wu_ref, wd_ref, gate_ref, g_ref, b_ref, o_ref):
    x = x_ref[...]
    h = (x * (1.0 + sc_ref[...]) + sh_ref[...]).astype(BF16)
    y = ALPHA * x + gate_ref[...] * _swiglu(h, wg_ref, wu_ref, wd_ref)
    o_ref[...] = _layer_norm(y, g_ref[...], b_ref[...])


def _ffn_dense(stream, x, shift, scale, w_gu, w_down, gate, ln_g, ln_b):
    vec = _full_spec((1, D_MODEL))
    mod = _mod_spec(stream, TM_FFN)
    row = _row_spec(TM_FFN, D_MODEL)
    return pl.pallas_call(
        _ffn_dense_kernel,
        grid=(stream.n // TM_FFN,),
        in_specs=[row, mod, mod,
                  pl.BlockSpec((D_MODEL, D_FF), lambda i: (0, 0)),
                  pl.BlockSpec((D_MODEL, D_FF), lambda i: (0, 1)),
                  _full_spec((D_FF, D_MODEL)),
                  mod, vec, vec],
        out_specs=row,
        out_shape=jax.ShapeDtypeStruct((stream.n, D_MODEL), F32),
        compiler_params=_params("arbitrary"),
        name="ffn_dense",
    )(x, shift, scale, w_gu, w_gu, w_down, gate, ln_g, ln_b)


def _ffn_grouped_kernel(te_ref, tv_ref, x_ref, wg_ref, wu_ref, wd_ref, o_ref):
    i = pl.program_id(0)

    @pl.when(tv_ref[i] != 0)
    def _():
        x = _from_token_tiles(x_ref, TMG).astype(BF16)
        _to_token_tiles(o_ref, _swiglu(x, wg_ref, wu_ref, wd_ref))

    @pl.when(tv_ref[i] == 0)
    def _():
        o_ref[...] = jnp.zeros_like(o_ref)


def _ffn_grouped(tile_expert, tile_valid, xs, w_gu, w_down):
    row = pl.BlockSpec((TMG * TOKEN_TILE_ROWS, LANES), lambda i, te, tv: (i, 0))
    grid_spec = pltpu.PrefetchScalarGridSpec(
        num_scalar_prefetch=2,
        grid=(NT_MOE,),
        in_specs=[row,
                  pl.BlockSpec((None, D_MODEL, D_FF), lambda i, te, tv: (te[i], 0, 0)),
                  pl.BlockSpec((None, D_MODEL, D_FF), lambda i, te, tv: (te[i], 0, 1)),
                  pl.BlockSpec((None, D_FF, D_MODEL), lambda i, te, tv: (te[i], 0, 0))],
        out_specs=row,
    )
    return pl.pallas_call(
        _ffn_grouped_kernel,
        grid_spec=grid_spec,
        out_shape=jax.ShapeDtypeStruct((NT_MOE * TMG * TOKEN_TILE_ROWS, LANES), F32),
        compiler_params=_params("arbitrary"),
        name="ffn_grouped",
    )(tile_expert, tile_valid, xs, w_gu, w_gu, w_down)


META_E0, META_E1, META_W0, META_W1, META_R0, META_R1 = range(6)


def _router_kernel(x_ref, sh_ref, sc_ref, rw_ref, cnt_in_ref, h_ref, meta_ref, cnt_ref, carry_scr):
    i = pl.program_id(0)

    @pl.when(i == 0)
    def _():
        carry_scr[...] = cnt_in_ref[...]

    h = x_ref[...] * (1.0 + sc_ref[...]) + sh_ref[...]
    _to_token_tiles(h_ref, h)
    w = rw_ref[...]
    h_hi = h.astype(BF16)
    w_hi = w.astype(BF16)
    h_lo = (h - h_hi.astype(F32)).astype(BF16)
    w_lo = (w - w_hi.astype(F32)).astype(BF16)
    logits = (jnp.dot(h_hi, w_hi, preferred_element_type=F32)
              + (jnp.dot(h_lo, w_hi, preferred_element_type=F32)
                 + jnp.dot(h_hi, w_lo, preferred_element_type=F32)))
    lane = lax.broadcasted_iota(jnp.int32, logits.shape, 1).astype(F32)
    lg = jnp.where(lane < N_EXPERTS, logits, -jnp.inf)
    m1 = jnp.max(lg, axis=-1, keepdims=True)
    i1 = jnp.min(jnp.where(lg == m1, lane, float(LANES)), axis=-1, keepdims=True)
    lg2 = jnp.where(lane == i1, -jnp.inf, lg)
    m2 = jnp.max(lg2, axis=-1, keepdims=True)
    i2 = jnp.min(jnp.where(lg2 == m2, lane, float(LANES)), axis=-1, keepdims=True)
    e = jnp.exp(m2 - m1)
    w1 = 1.0 / (1.0 + e)
    w2 = e / (1.0 + e)

    sel1 = lane == i1
    sel2 = lane == i2
    onehot = jnp.where(sel1 | sel2, 1.0, 0.0)
    rr = lax.broadcasted_iota(jnp.int32, (TM, TM), 0)
    cc = lax.broadcasted_iota(jnp.int32, (TM, TM), 1)
    lower = jnp.where(cc < rr, 1.0, 0.0).astype(BF16)
    before = jnp.dot(lower, onehot.astype(BF16), preferred_element_type=F32) + carry_scr[0:1, :]
    r1 = jnp.sum(jnp.where(sel1, before, 0.0), axis=-1, keepdims=True)
    r2 = jnp.sum(jnp.where(sel2, before, 0.0), axis=-1, keepdims=True)
    carry_scr[...] = carry_scr[...] + jnp.sum(onehot, axis=0, keepdims=True)
    cnt_ref[...] = carry_scr[...]

    cols = [i1, i2, w1, w2, r1, r2]
    meta = jnp.zeros(logits.shape, F32)
    for c, val in enumerate(cols):
        meta = jnp.where(lane == c, val, meta)
    meta_ref[...] = meta


def _router(stream, x, shift, scale, rw_pad, cnt_in):
    cnt_spec = _full_spec((SUBLANES, LANES))
    return pl.pallas_call(
        _router_kernel,
        grid=(stream.n // TM,),
        in_specs=[_row_spec(TM, D_MODEL), _mod_spec(stream, TM), _mod_spec(stream, TM),
                  _full_spec((D_MODEL, LANES)), cnt_spec],
        out_specs=[_row_spec(TM * TOKEN_TILE_ROWS, LANES), _row_spec(TM, LANES), cnt_spec],
        out_shape=[jax.ShapeDtypeStruct((stream.n * TOKEN_TILE_ROWS, LANES), F32),
                   jax.ShapeDtypeStruct((stream.n, LANES), F32),
                   jax.ShapeDtypeStruct((SUBLANES, LANES), F32)],
        scratch_shapes=[pltpu.VMEM((SUBLANES, LANES), F32)],
        compiler_params=_params("arbitrary"),
        name="moe_router",
    )(x, shift, scale, rw_pad, cnt_in)


def _tile_rows(token, count=1):
    start = token * TOKEN_TILE_ROWS
    if not isinstance(start, int):
        start = pl.multiple_of(start, TOKEN_TILE_ROWS)
    return pl.ds(start, count * TOKEN_TILE_ROWS)


def _to_token_tiles(ref, x):
    for j in range(TOKEN_TILE_ROWS):
        ref[pl.ds(j, x.shape[0], stride=TOKEN_TILE_ROWS), :] = x[:, j * LANES:(j + 1) * LANES]


def _from_token_tiles(ref, tokens):
    return jnp.concatenate([ref[pl.ds(j, tokens, stride=TOKEN_TILE_ROWS), :] for j in range(TOKEN_TILE_ROWS)], axis=1)


def _row_copy(src_ref, src_row, dst_ref, dst_row, sem):
    return pltpu.make_async_copy(src_ref.at[_tile_rows(src_row)], dst_ref.at[_tile_rows(dst_row)], sem)


def _scatter_rows(d0_ref, d1_ref, h_ref, buf_ref, sem):
    base = pl.program_id(0) * TM
    for r in range(TM):
        _row_copy(h_ref, r, buf_ref, d0_ref[base + r], sem).start(priority=0)
        _row_copy(h_ref, r, buf_ref, d1_ref[base + r], sem).start(priority=1)


def _scatter_wait(h_ref, buf_ref, sem, n_rows):
    pltpu.make_async_copy(h_ref.at[_tile_rows(0, n_rows)], buf_ref.at[_tile_rows(0, n_rows)], sem).wait()


def _scatter_kernel(d0_ref, d1_ref, pad_ref, hc_ref, hl_ref, buf_ref, zero_scr, sem, *, ctx_steps, pads_per_step):
    i = pl.program_id(0)

    @pl.when(i < ctx_steps)
    def _():
        _scatter_rows(d0_ref, d1_ref, hc_ref, buf_ref, sem)
        zero_scr[...] = jnp.zeros_like(zero_scr)
        pbase = i * pads_per_step
        for r in range(pads_per_step):
            _row_copy(zero_scr, 0, buf_ref, pad_ref[pbase + r], sem).start(priority=r % 2)
        _scatter_wait(hc_ref, buf_ref, sem, TM)
        _scatter_wait(hc_ref, buf_ref, sem, TM)
        _scatter_wait(hc_ref, buf_ref, sem, pads_per_step)

    @pl.when(i >= ctx_steps)
    def _():
        _scatter_rows(d0_ref, d1_ref, hl_ref, buf_ref, sem)
        _scatter_wait(hl_ref, buf_ref, sem, TM)
        _scatter_wait(hl_ref, buf_ref, sem, TM)


def _scatter(d0, d1, pad_rows, h_ctx, h_lat):
    ctx_steps = CTX.n // TM
    steps = ctx_steps + LAT.n // TM
    assert N_PAD_ROWS % ctx_steps == 0
    grid_spec = pltpu.PrefetchScalarGridSpec(
        num_scalar_prefetch=3,
        grid=(steps,),
        in_specs=[pl.BlockSpec((TM * TOKEN_TILE_ROWS, LANES), lambda i, a, b, c: (jnp.minimum(i, ctx_steps - 1), 0)),
                  pl.BlockSpec((TM * TOKEN_TILE_ROWS, LANES), lambda i, a, b, c: (jnp.maximum(i - ctx_steps, 0), 0))],
        out_specs=pl.BlockSpec(memory_space=pl.ANY),
        scratch_shapes=[pltpu.VMEM((TOKEN_TILE_ROWS, LANES), F32), pltpu.SemaphoreType.DMA],
    )
    return pl.pallas_call(
        functools.partial(_scatter_kernel, ctx_steps=ctx_steps, pads_per_step=N_PAD_ROWS // ctx_steps),
        grid_spec=grid_spec,
        out_shape=jax.ShapeDtypeStruct((NT_MOE * TMG * TOKEN_TILE_ROWS, LANES), F32),
        compiler_params=_params("arbitrary"),
        name="moe_scatter",
    )(d0, d1, pad_rows, h_ctx, h_lat)


def _gather_start(d0_ref, d1_ref, ys_ref, rows_scr, sems, tile, slot):
    base = tile * TM
    for r in range(TM):
        _row_copy(ys_ref, d0_ref[base + r], rows_scr.at[slot, 0], r, sems.at[slot]).start(priority=0)
        _row_copy(ys_ref, d1_ref[base + r], rows_scr.at[slot, 1], r, sems.at[slot]).start(priority=1)


def _gather_wait(ys_ref, rows_scr, sems, slot):
    for s in range(2):
        pltpu.make_async_copy(ys_ref.at[_tile_rows(0, TM)], rows_scr.at[slot, s], sems.at[slot]).wait()


def _combine_kernel(d0_ref, d1_ref, ys_ref, meta_ref, x_ref, gate_ref, g_ref, b_ref, o_ref, rows_scr, sems,
                    *, n_tiles):
    i = pl.program_id(0)
    slot = i % 2

    @pl.when(i == 0)
    def _():
        _gather_start(d0_ref, d1_ref, ys_ref, rows_scr, sems, 0, 0)

    _gather_start(d0_ref, d1_ref, ys_ref, rows_scr, sems, jnp.minimum(i + 1, n_tiles - 1), 1 - slot)
    _gather_wait(ys_ref, rows_scr, sems, slot)

    meta = meta_ref[...]
    w0 = meta[:, META_W0:META_W0 + 1]
    w1 = meta[:, META_W1:META_W1 + 1]
    f = w0 * _from_token_tiles(rows_scr.at[slot, 0], TM) + w1 * _from_token_tiles(rows_scr.at[slot, 1], TM)
    y = ALPHA * x_ref[...] + gate_ref[...] * f
    o_ref[...] = _layer_norm(y, g_ref[...], b_ref[...])

    @pl.when(i == n_tiles - 1)
    def _():
        _gather_wait(ys_ref, rows_scr, sems, 1 - slot)


def _combine(stream, d0, d1, ys, meta, x, gate, ln_g, ln_b):
    vec = pl.BlockSpec((1, D_MODEL), lambda i, a, b: (0, 0))
    grid_spec = pltpu.PrefetchScalarGridSpec(
        num_scalar_prefetch=2,
        grid=(stream.n // TM,),
        in_specs=[pl.BlockSpec(memory_space=pl.ANY), _row_spec(TM, LANES), _row_spec(TM, D_MODEL),
                  _mod_spec(stream, TM), vec, vec],
        out_specs=_row_spec(TM, D_MODEL),
        scratch_shapes=[pltpu.VMEM((2, 2, TM * TOKEN_TILE_ROWS, LANES), F32), pltpu.SemaphoreType.DMA((2,))],
    )
    return pl.pallas_call(
        functools.partial(_combine_kernel, n_tiles=stream.n // TM),
        grid_spec=grid_spec,
        out_shape=jax.ShapeDtypeStruct((stream.n, D_MODEL), F32),
        compiler_params=_params("arbitrary"),
        name="moe_combine",
    )(d0, d1, ys, meta, x, gate, ln_g, ln_b)


def _rope_tables():
    t = jnp.arange(DEC_SEQ)
    row = (t // GRID_W).astype(F32)
    col = (t % GRID_W).astype(F32)
    freqs = ROPE_THETA ** (-jnp.arange(ROT_FREQS, dtype=F32) / ROT_FREQS)
    ar = row[:, None] * freqs
    ac = col[:, None] * freqs
    cos = jnp.concatenate([jnp.cos(ar), jnp.cos(ar), jnp.cos(ac), jnp.cos(ac)], axis=1)
    sin = jnp.concatenate([-jnp.sin(ar), jnp.sin(ar), -jnp.sin(ac), jnp.sin(ac)], axis=1)
    return cos, sin


def _routing_plan(metas, counts):
    cnt = counts[0, :N_EXPERTS].astype(jnp.int32)
    tiles_e = (cnt + TMG - 1) // TMG
    tile_end = jnp.cumsum(tiles_e)
    tile_start = tile_end - tiles_e
    offs = tile_start * TMG
    dests = []
    for meta in metas:
        pair = []
        for ecol, rcol in ((META_E0, META_R0), (META_E1, META_R1)):
            e = meta[:, ecol].astype(jnp.int32)
            d = meta[:, rcol].astype(jnp.int32)
            for k in range(N_EXPERTS):
                d = d + jnp.where(e == k, offs[k], 0)
            pair.append(d)
        dests.append(tuple(pair))
    tid = jnp.arange(NT_MOE, dtype=jnp.int32)
    te = jnp.minimum(jnp.sum((tid[:, None] >= tile_end[None, :]).astype(jnp.int32), axis=1), N_EXPERTS - 1)
    total = tile_end[-1]
    valid = tid < total
    te_last = jnp.max(jnp.where(valid, te, 0))
    te = jnp.where(valid, te, te_last)
    pad_cnt = tiles_e * TMG - cnt
    pad_end = jnp.cumsum(pad_cnt)
    pad_start = pad_end - pad_cnt
    k = jnp.arange(N_PAD_ROWS, dtype=jnp.int32)
    grp = jnp.sum((k[:, None] >= pad_end[None, :]).astype(jnp.int32), axis=1)
    pad_rows = total * TMG + (k - pad_end[-1])
    for e in range(N_EXPERTS):
        pad_rows = jnp.where(grp == e, offs[e] + cnt[e] + (k - pad_start[e]), pad_rows)
    return dests, te, valid.astype(jnp.int32), pad_rows


def kernel(x_prompt, x_sample, cache_k_a, cache_v_a, cache_k_b, cache_v_b, c, c_ctx, ada_w, ada_b, ln_attn_g, ln_attn_b, ln_ffn_g, ln_ffn_b, wqkv_a, qnorm_a, knorm_a, wo_a, wqkv_b, rpb_b, wo_b, ffn_w_gu, ffn_w_down, router_w, moe_w_gu, moe_w_down):
    streams = (CTX, LAT)
    xs = [x_prompt.reshape(N_CTX, D_MODEL), x_sample.reshape(N_LAT, D_MODEL)]

    cond = jnp.zeros((GROUP_PAD, D_MODEL), F32).at[0].set(c_ctx).at[1:N_GROUPS].set(c)
    mods = _ada_mods(cond, ada_w, ada_b)
    vec = lambda a, l: a[l].reshape(1, D_MODEL)

    m = mods[0]
    w_qkv = wqkv_a[0].astype(BF16)
    w_o = wo_a[0].astype(BF16)
    w_gu = ffn_w_gu[0].astype(BF16)
    w_dn = ffn_w_down[0].astype(BF16)
    gains = (qnorm_a[0].reshape(1, HEAD_DIM_A), knorm_a[0].reshape(1, HEAD_DIM_A))
    nq, nk = N_HEADS_A * HEAD_DIM_A, N_KV_A * HEAD_DIM_A
    qscale = HEAD_DIM_A ** -0.5

    qp, kp, vp, kp32, vp32 = _qkv(CTX, xs[0], m[0], m[1], w_qkv, nq=nq, nk=nk, qscale=qscale, gains=gains,
                                  emit_f32=True)
    new_k_a = kp32.reshape(BATCH, 1, SEQ, N_KV_A, HEAD_DIM_A)
    new_v_a = vp32.reshape(BATCH, 1, SEQ, N_KV_A, HEAD_DIM_A)
    ql, kl, vlt = _qkv(LAT, xs[1], m[0], m[1], w_qkv, nq=nq, nk=nk, qscale=qscale * LOG2_E, gains=gains,
                       rope_tables=_rope_tables(), v_layout="t")
    cache_k = cache_k_a[:, 0].reshape(DEC_BATCH, PAST_LEN, nk).astype(BF16)
    cache_vt = jnp.swapaxes(cache_v_a[:, 0].reshape(DEC_BATCH, PAST_LEN, nk), 1, 2).astype(BF16)
    attn = [_gqa_ctx(qp, kp, vp), _gqa_lat_t(ql, kl, vlt, cache_k, cache_vt)]
    xs = [_wo_ln(s, a, w_o, x, m[2], vec(ln_attn_g, 0), vec(ln_attn_b, 0)) for s, a, x in zip(streams, attn, xs)]
    xs = [_ffn_dense(s, x, m[3], m[4], w_gu, w_dn, m[5], vec(ln_ffn_g, 0), vec(ln_ffn_b, 0))
          for s, x in zip(streams, xs)]

    m = mods[1]
    w_qkv = wqkv_b[0].astype(BF16)
    w_o = wo_b[0].astype(BF16)
    qscale = HEAD_DIM_B ** -0.5
    qp, kp, vp, kp32, vp32 = _qkv(CTX, xs[0], m[0], m[1], w_qkv, nq=D_MODEL, nk=D_MODEL, qscale=qscale,
                                  emit_f32=True)
    new_k_b = kp32.reshape(BATCH, 1, SEQ, N_HEADS_B, HEAD_DIM_B)
    new_v_b = vp32.reshape(BATCH, 1, SEQ, N_HEADS_B, HEAD_DIM_B)
    ql, kl, vlt = _qkv(LAT, xs[1], m[0], m[1], w_qkv, nq=D_MODEL, nk=D_MODEL, qscale=qscale * LOG2_E,
                       v_layout="t_blocked")
    cache_k = cache_k_b[:, 0].reshape(DEC_BATCH, PAST_LEN, D_MODEL).astype(BF16)
    cache_vt = jnp.swapaxes(cache_v_b[:, 0].reshape(DEC_BATCH, PAST_LEN, D_MODEL), 1, 2).astype(BF16)
    attn = [_mha_ctx(qp, kp, vp), _natten(ql, kl, vlt, cache_k, cache_vt, _bias_table(rpb_b[0]))]
    xs = [_wo_ln(s, a, w_o, x, m[2], vec(ln_attn_g, 1), vec(ln_attn_b, 1)) for s, a, x in zip(streams, attn, xs)]

    rw_pad = jnp.zeros((D_MODEL, LANES), F32).at[:, :N_EXPERTS].set(router_w[0])
    counts = jnp.zeros((SUBLANES, LANES), F32)
    hs, metas = [], []
    for s, x in zip(streams, xs):
        h, meta, counts = _router(s, x, m[3], m[4], rw_pad, counts)
        hs.append(h)
        metas.append(meta)
    dests, tile_expert, tile_valid, pad_rows = _routing_plan(metas, counts)
    d0_all = jnp.concatenate([dests[0][0], dests[1][0]])
    d1_all = jnp.concatenate([dests[0][1], dests[1][1]])
    sorted_rows = _scatter(d0_all, d1_all, pad_rows, hs[0], hs[1])
    ys = _ffn_grouped(tile_expert, tile_valid, sorted_rows, moe_w_gu[0].astype(BF16), moe_w_down[0].astype(BF16))
    outs = [_combine(s, d0, d1, ys, meta, x, m[5], vec(ln_ffn_g, 1), vec(ln_ffn_b, 1))
            for s, (d0, d1), meta, x in zip(streams, dests, metas, xs)]

    y_prompt = outs[0].reshape(BATCH, SEQ, D_MODEL)
    y_sample = outs[1].reshape(DEC_BATCH, DEC_SEQ, D_MODEL)
    return (y_prompt, y_sample, new_k_a, new_v_a, new_k_b, new_v_b)
```

```python
import functools
import math

import jax
import jax.numpy as jnp
from jax import lax
from jax.experimental import pallas as pl
from jax.experimental.pallas import tpu as pltpu

F32 = jnp.float32
BF16 = jnp.bfloat16
HIGHEST = lax.Precision.HIGHEST

D_MODEL = 1024
BATCH, SEQ = 32, 256
DEC_BATCH, DEC_SEQ = 4, 4096
PAST_LEN = 256
DEPTH = 2
GRID_W = 64
GRID_H = DEC_SEQ // GRID_W
N_HEADS_A, N_KV_A, HEAD_DIM_A = 8, 2, 128
ROT_FREQS = HEAD_DIM_A // 4
ROPE_THETA = 10000.0
N_HEADS_B, HEAD_DIM_B = 16, 64
WIN_H, WIN_W = 8, 16
D_FF = 2816
N_EXPERTS = 8
EPS = 1e-6
NEG_INF = -1e30
ALPHA = (2.0 * DEPTH) ** 0.25
LOG2_E = math.log2(math.e)

N_CTX = BATCH * SEQ
N_LAT = DEC_BATCH * DEC_SEQ
N_GROUPS = 1 + DEC_BATCH
GROUP_PAD = 8

LANES = 128
SUBLANES = 8
VMEM_LIMIT = 56 * 2**20

TM = 512
MXU_DIM = 256
FF_CHUNK = 4 * MXU_DIM
FF_CHUNKS = tuple((lo, min(lo + FF_CHUNK, D_FF)) for lo in range(0, D_FF, FF_CHUNK))
assert D_FF % MXU_DIM == 0
TM_FFN = 512
TMG = 512
TQ_A = 256
UNIT_A = 2
TN_ADA = 1536
N_PAIRS = 2 * (N_CTX + N_LAT)
NT_MOE = N_PAIRS // TMG + N_EXPERTS
N_PAD_ROWS = NT_MOE * TMG - N_PAIRS
TOKEN_TILE_ROWS = D_MODEL // LANES
assert TOKEN_TILE_ROWS == SUBLANES


class Stream:
    def __init__(self, n_rows, group_offset, rows_per_group):
        self.n = n_rows
        self.goff = group_offset
        self.rpg = rows_per_group

    def group_map(self, tm):
        tiles_per_group = self.rpg // tm
        goff = self.goff
        return lambda i, *_: (goff + i // tiles_per_group, 0, 0)


CTX = Stream(N_CTX, 0, N_CTX)
LAT = Stream(N_LAT, 1, DEC_SEQ)


def _params(*sem):
    return pltpu.CompilerParams(dimension_semantics=sem, vmem_limit_bytes=VMEM_LIMIT)


def _mod_spec(stream, tm):
    return pl.BlockSpec((None, 1, D_MODEL), stream.group_map(tm))


def _row_spec(tm, width):
    return pl.BlockSpec((tm, width), lambda i, *_: (i, 0))


def _full_spec(shape):
    nd = len(shape)
    return pl.BlockSpec(shape, lambda *_: (0,) * nd)


def _layer_norm(y, g, b):
    mu = jnp.mean(y, axis=-1, keepdims=True)
    d = y - mu
    var = jnp.mean(d * d, axis=-1, keepdims=True)
    return d * lax.rsqrt(var + EPS) * g + b


def _ada_kernel(c_ref, w_ref, b_ref, o_ref):
    c = c_ref[...]
    s = c * jax.nn.sigmoid(c)
    o_ref[...] = jnp.dot(s, w_ref[...], preferred_element_type=F32, precision=HIGHEST) + b_ref[...]


def _ada_mods(cond, ada_w, ada_b):
    n_out = 6 * D_MODEL
    out = pl.pallas_call(
        _ada_kernel,
        grid=(DEPTH, n_out // TN_ADA),
        in_specs=[
            pl.BlockSpec((GROUP_PAD, D_MODEL), lambda l, n: (0, 0)),
            pl.BlockSpec((None, D_MODEL, TN_ADA), lambda l, n: (l, 0, n)),
            pl.BlockSpec((None, 1, TN_ADA), lambda l, n: (l, 0, n)),
        ],
        out_specs=pl.BlockSpec((None, GROUP_PAD, TN_ADA), lambda l, n: (l, 0, n)),
        out_shape=jax.ShapeDtypeStruct((DEPTH, GROUP_PAD, n_out), F32),
        compiler_params=_params("arbitrary", "arbitrary"),
        name="ada_mods",
    )(cond, ada_w, ada_b.reshape(DEPTH, 1, n_out))
    out = out.reshape(DEPTH, GROUP_PAD, 6, D_MODEL).transpose(0, 2, 1, 3)
    return out[:, :, :, None, :]


def _swap_halves(t):
    lane = lax.broadcasted_iota(jnp.int32, t.shape, 1)
    fwd = pltpu.roll(t, LANES - ROT_FREQS, 1)
    bwd = pltpu.roll(t, ROT_FREQS, 1)
    return jnp.where((lane % (2 * ROT_FREQS)) < ROT_FREQS, fwd, bwd)


def _qkv_kernel(*refs, nq, nk, norm, rope, emit_f32, v_layout, qscale):
    refs = list(refs)
    x_ref, sh_ref, sc_ref, w_ref = refs[:4]
    pos = 4
    if norm:
        qg_ref, kg_ref = refs[pos:pos + 2]
        pos += 2
    if rope:
        cos_ref, sin_ref = refs[pos:pos + 2]
        pos += 2
    q_ref, k_ref, v_ref = refs[pos:pos + 3]
    pos += 3
    if emit_f32:
        kf_ref, vf_ref = refs[pos:pos + 2]

    h = (x_ref[...] * (1.0 + sc_ref[...]) + sh_ref[...]).astype(BF16)
    qkv = jnp.dot(h, w_ref[...], preferred_element_type=F32)
    if norm:
        n_heads = (nq + nk) // HEAD_DIM_A
        for hd in range(n_heads):
            lo = hd * HEAD_DIM_A
            t = qkv[:, lo:lo + HEAD_DIM_A]
            ms = jnp.mean(t * t, axis=-1, keepdims=True)
            gain = qg_ref[...] if lo < nq else kg_ref[...]
            t = t * lax.rsqrt(ms + EPS) * gain
            if rope:
                t = t * cos_ref[...] + _swap_halves(t) * sin_ref[...]
            if lo < nq:
                q_ref[:, lo:lo + HEAD_DIM_A] = (t * qscale).astype(BF16)
            else:
                k_ref[:, lo - nq:lo - nq + HEAD_DIM_A] = t.astype(BF16)
                if emit_f32:
                    kf_ref[:, lo - nq:lo - nq + HEAD_DIM_A] = t
    else:
        q_ref[...] = (qkv[:, :nq] * qscale).astype(BF16)
        k = qkv[:, nq:nq + nk]
        k_ref[...] = k.astype(BF16)
        if emit_f32:
            kf_ref[...] = k
    v = qkv[:, nq + nk:]
    if v_layout == "rows":
        v_ref[...] = v.astype(BF16)
    elif v_layout == "t":
        v_ref[...] = v.T.astype(BF16)
    else:
        vt = v.T.astype(BF16)
        for j in range(vt.shape[1] // LANES):
            v_ref[j] = vt[:, j * LANES:(j + 1) * LANES]
    if emit_f32:
        vf_ref[...] = v


def _qkv(stream, x, shift, scale, w, *, nq, nk, qscale, gains=None, rope_tables=None, emit_f32=False,
         v_layout="rows"):
    n = stream.n
    nw = w.shape[1]
    norm = gains is not None
    rope = rope_tables is not None
    in_specs = [_row_spec(TM, D_MODEL), _mod_spec(stream, TM), _mod_spec(stream, TM), _full_spec((D_MODEL, nw))]
    args = [x, shift, scale, w]
    if norm:
        in_specs += [_full_spec((1, HEAD_DIM_A))] * 2
        args += list(gains)
    if rope:
        tiles_per_seq = DEC_SEQ // TM
        tbl_spec = pl.BlockSpec((TM, HEAD_DIM_A), lambda i: (i % tiles_per_seq, 0))
        in_specs += [tbl_spec, tbl_spec]
        args += list(rope_tables)
    out_specs = [_row_spec(TM, nq), _row_spec(TM, nk), _row_spec(TM, nk)]
    out_shape = [jax.ShapeDtypeStruct((n, nq), BF16), jax.ShapeDtypeStruct((n, nk), BF16),
                 jax.ShapeDtypeStruct((n, nk), BF16)]
    if v_layout == "t":
        out_specs[2] = pl.BlockSpec((nk, TM), lambda i: (0, i))
        out_shape[2] = jax.ShapeDtypeStruct((nk, n), BF16)
    elif v_layout == "t_blocked":
        out_specs[2] = pl.BlockSpec((TM // LANES, nk, LANES), lambda i: (i, 0, 0))
        out_shape[2] = jax.ShapeDtypeStruct((n // LANES, nk, LANES), BF16)
    if emit_f32:
        out_specs += [_row_spec(TM, nk), _row_spec(TM, nk)]
        out_shape += [jax.ShapeDtypeStruct((n, nk), F32)] * 2
    return pl.pallas_call(
        functools.partial(_qkv_kernel, nq=nq, nk=nk, norm=norm, rope=rope, emit_f32=emit_f32,
                          v_layout=v_layout, qscale=qscale),
        grid=(n // TM,),
        in_specs=in_specs,
        out_specs=out_specs,
        out_shape=out_shape,
        compiler_params=_params("arbitrary"),
        name="qkv_norm_rope" if norm else "qkv",
    )(*args)


def _attend(q, kvs, biases=None):
    scores = []
    for idx, (k, _) in enumerate(kvs):
        s = lax.dot_general(q, k, (((1,), (1,)), ((), ())), preferred_element_type=F32)
        if biases is not None and biases[idx] is not None:
            s = s + biases[idx]
        scores.append(s)
    m = jnp.max(scores[0], axis=-1, keepdims=True)
    for s in scores[1:]:
        m = jnp.maximum(m, jnp.max(s, axis=-1, keepdims=True))
    denom = None
    out = None
    for s, (_, v) in zip(scores, kvs):
        p = jnp.exp(s - m)
        part = jnp.sum(p, axis=-1, keepdims=True)
        pv = jnp.dot(p.astype(BF16), v, preferred_element_type=F32)
        denom = part if denom is None else denom + part
        out = pv if out is None else out + pv
    return out * (1.0 / denom)


def _gqa_ctx_kernel(q_ref, k_ref, v_ref, o_ref):
    d = HEAD_DIM_A
    group = N_HEADS_A // N_KV_A
    for g in range(N_KV_A):
        heads = [g * group + j for j in range(group)]
        qs = jnp.concatenate([q_ref[:, h * d:(h + 1) * d] for h in heads], axis=0)
        o = _attend(qs, [(k_ref[:, g * d:(g + 1) * d], v_ref[:, g * d:(g + 1) * d])])
        for j, h in enumerate(heads):
            o_ref[:, h * d:(h + 1) * d] = o[j * SEQ:(j + 1) * SEQ].astype(o_ref.dtype)


def _gqa_ctx(q, k, v):
    nq, nk = N_HEADS_A * HEAD_DIM_A, N_KV_A * HEAD_DIM_A
    return pl.pallas_call(
        _gqa_ctx_kernel,
        grid=(BATCH,),
        in_specs=[_row_spec(SEQ, nq), _row_spec(SEQ, nk), _row_spec(SEQ, nk)],
        out_specs=_row_spec(SEQ, nq),
        out_shape=jax.ShapeDtypeStruct((N_CTX, nq), BF16),
        compiler_params=_params("arbitrary"),
        name="gqa_ctx",
    )(q, k, v)


ONES_ROWS = 16


def _scores_t(q, ks):
    return [lax.dot_general(k, q, (((1,), (1,)), ((), ())), preferred_element_type=F32) for k in ks]


def _softmax_pv_t(scores, vts):
    m = jnp.max(scores[0], axis=0, keepdims=True)
    for s in scores[1:]:
        m = jnp.maximum(m, jnp.max(s, axis=0, keepdims=True))
    acc = None
    for s, vt in zip(scores, vts):
        p = jnp.exp2(s - m).astype(BF16)
        vt_ones = jnp.concatenate([vt, jnp.ones((ONES_ROWS, vt.shape[1]), BF16)], axis=0)
        pv = jnp.dot(vt_ones, p, preferred_element_type=F32)
        acc = pv if acc is None else acc + pv
    d = LANES
    return (acc[:d] * (1.0 / acc[d:d + 1])).T


def _gqa_t_kernel(q_ref, k_ref, vt_ref, kc_ref, vct_ref, o_ref, *, n_kv, group, unit, tq):
    d = HEAD_DIM_A
    units = [(g, [g * group + u0 + j for j in range(unit)]) for g in range(n_kv) for u0 in range(0, group, unit)]
    def unit_scores(g, heads):
        ds = slice(g * d, (g + 1) * d)
        qs = jnp.concatenate([q_ref[:, h * d:(h + 1) * d] for h in heads], axis=0)
        return _scores_t(qs, [k_ref[:, ds], kc_ref[:, ds]])

    nxt = unit_scores(*units[0])
    for idx, (g, heads) in enumerate(units):
        sc = nxt
        if idx + 1 < len(units):
            nxt = unit_scores(*units[idx + 1])
        ds = slice(g * d, (g + 1) * d)
        o = _softmax_pv_t(sc, [vt_ref[ds, :], vct_ref[ds, :]])
        for j, h in enumerate(heads):
            o_ref[:, h * d:(h + 1) * d] = o[j * tq:(j + 1) * tq].astype(o_ref.dtype)


def _gqa_lat_t(q, k, vt, kc, vct):
    nq, nk = N_HEADS_A * HEAD_DIM_A, N_KV_A * HEAD_DIM_A
    tiles = DEC_SEQ // TQ_A
    q_spec = pl.BlockSpec((TQ_A, nq), lambda b, t: (b * tiles + t, 0))
    return pl.pallas_call(
        functools.partial(_gqa_t_kernel, n_kv=N_KV_A, group=N_HEADS_A // N_KV_A, unit=UNIT_A, tq=TQ_A),
        grid=(DEC_BATCH, tiles),
        in_specs=[q_spec,
                  pl.BlockSpec((DEC_SEQ, nk), lambda b, t: (b, 0)),
                  pl.BlockSpec((nk, DEC_SEQ), lambda b, t: (0, b)),
                  pl.BlockSpec((None, PAST_LEN, nk), lambda b, t: (b, 0, 0)),
                  pl.BlockSpec((None, nk, PAST_LEN), lambda b, t: (b, 0, 0))],
        out_specs=q_spec,
        out_shape=jax.ShapeDtypeStruct((N_LAT, nq), BF16),
        compiler_params=_params("arbitrary", "arbitrary"),
        name="gqa_lat",
    )(q, k, vt, kc, vct)


def _split_pair(qp):
    lo = lax.broadcasted_iota(jnp.int32, qp.shape, 1) < HEAD_DIM_B
    zero = jnp.zeros_like(qp)
    return jnp.concatenate([jnp.where(lo, qp, zero), jnp.where(lo, zero, qp)], axis=0)


def _merge_pair(o2, rows):
    lo = lax.broadcasted_iota(jnp.int32, (rows, LANES), 1) < HEAD_DIM_B
    return jnp.where(lo, o2[:rows], o2[rows:])


def _mha_ctx_kernel(q_ref, k_ref, v_ref, o_ref):
    for p in range(N_HEADS_B // 2):
        sl = slice(p * LANES, (p + 1) * LANES)
        o2 = _attend(_split_pair(q_ref[:, sl]), [(k_ref[:, sl], v_ref[:, sl])])
        o_ref[:, sl] = _merge_pair(o2, SEQ).astype(o_ref.dtype)


def _mha_ctx(q, k, v):
    return pl.pallas_call(
        _mha_ctx_kernel,
        grid=(BATCH,),
        in_specs=[_row_spec(SEQ, D_MODEL)] * 3,
        out_specs=_row_spec(SEQ, D_MODEL),
        out_shape=jax.ShapeDtypeStruct((N_CTX, D_MODEL), BF16),
        compiler_params=_params("arbitrary"),
        name="mha_ctx",
    )(q, k, v)


NAT_HALF = D_MODEL
NAT_PAIRS = NAT_HALF // LANES
NAT_R = 4
NAT_WROWS = 12
N_DROW = 2 * WIN_H - 1
MASKED_TILE = N_DROW


def _natten_kernel(q_ref, k_ref, vt_ref, kc_ref, vct_ref, tbl_ref, o_ref):
    r0 = pl.program_id(2) * NAT_R
    ws = jnp.clip(r0 - WIN_H // 2, 0, GRID_H - NAT_WROWS)
    row0 = pl.multiple_of(ws * GRID_W, LANES)
    blk0 = ws * GRID_W // LANES
    n_loc = NAT_WROWS * GRID_W

    def tile_index(a, w):
        r = r0 + a
        start = jnp.clip(r - WIN_H // 2, 0, GRID_H - WIN_H)
        wr = ws + w
        valid = (wr >= start) & (wr < start + WIN_H)
        return jnp.where(valid, wr - r + (WIN_H - 1), MASKED_TILE)

    idx = [[tile_index(a, w) for w in range(NAT_WROWS)] for a in range(NAT_R)]

    def pair_scores(p):
        sl = slice(p * LANES, (p + 1) * LANES)
        q2 = jnp.concatenate([_split_pair(q_ref[a * GRID_W:(a + 1) * GRID_W, sl]) for a in range(NAT_R)], axis=0)
        bias_t = jnp.concatenate(
            [jnp.concatenate([tbl_ref[idx[a][w], p] for w in range(NAT_WROWS)], axis=0) for a in range(NAT_R)],
            axis=1)
        s_loc, s_ctx = _scores_t(q2, [k_ref[pl.ds(row0, n_loc), sl], kc_ref[:, sl]])
        return [s_loc + bias_t, s_ctx]

    nxt = pair_scores(0)
    for p in range(NAT_PAIRS):
        sl = slice(p * LANES, (p + 1) * LANES)
        sc = nxt
        if p + 1 < NAT_PAIRS:
            nxt = pair_scores(p + 1)
        vt_win = jnp.concatenate([vt_ref[blk0 + j, sl, :] for j in range(n_loc // LANES)], axis=1)
        o2 = _softmax_pv_t(sc, [vt_win, vct_ref[sl, :]])
        for a in range(NAT_R):
            o_ref[a * GRID_W:(a + 1) * GRID_W, sl] = _merge_pair(
                o2[a * LANES:(a + 1) * LANES], GRID_W).astype(o_ref.dtype)


def _natten(q, k, vt, kc, vct, tbl):
    n_half = D_MODEL // NAT_HALF
    blocks = GRID_H // NAT_R
    seq_blocks = DEC_SEQ // LANES
    q_spec = pl.BlockSpec((NAT_R * GRID_W, NAT_HALF), lambda b, hh, r: (b * blocks + r, hh))
    return pl.pallas_call(
        _natten_kernel,
        grid=(DEC_BATCH, n_half, blocks),
        in_specs=[q_spec,
                  pl.BlockSpec((DEC_SEQ, NAT_HALF), lambda b, hh, r: (b, hh)),
                  pl.BlockSpec((seq_blocks, NAT_HALF, LANES), lambda b, hh, r: (b, hh, 0)),
                  pl.BlockSpec((None, PAST_LEN, NAT_HALF), lambda b, hh, r: (b, 0, hh)),
                  pl.BlockSpec((None, NAT_HALF, PAST_LEN), lambda b, hh, r: (b, hh, 0)),
                  pl.BlockSpec((N_DROW + 1, NAT_PAIRS, GRID_W, LANES), lambda b, hh, r: (0, hh, 0, 0))],
        out_specs=q_spec,
        out_shape=jax.ShapeDtypeStruct((N_LAT, D_MODEL), BF16),
        compiler_params=_params("arbitrary", "arbitrary", "arbitrary"),
        name="natten",
    )(q, k, vt, kc, vct, tbl)


def _bias_table_kernel(rpb_ref, o_ref):
    dr = pl.program_id(0)
    row = lax.broadcasted_iota(jnp.int32, (GRID_W, LANES), 0)
    col = lax.broadcasted_iota(jnp.int32, (GRID_W, LANES), 1)
    sub = col >= GRID_W
    qc = col % GRID_W
    kc = row
    dc = jnp.clip(kc - qc, -(WIN_W - 1), WIN_W - 1) + (WIN_W - 1)
    col_start = jnp.clip(qc - WIN_W // 2, 0, GRID_W - WIN_W)
    in_win = (kc >= col_start) & (kc < col_start + WIN_W) & (dr < N_DROW)
    n_dc = 2 * WIN_W - 1
    drc = jnp.minimum(dr, N_DROW - 1)
    for pr in range(N_HEADS_B // 2):
        b0 = ((2 * pr) * N_DROW + drc) * n_dc
        b1 = ((2 * pr + 1) * N_DROW + drc) * n_dc

        def body(j, acc, b0=b0, b1=b1):
            return jnp.where(dc == j, jnp.where(sub, rpb_ref[b1 + j], rpb_ref[b0 + j]), acc)

        acc = lax.fori_loop(0, n_dc, body, jnp.zeros((GRID_W, LANES), F32))
        o_ref[pr] = jnp.where(in_win, acc * LOG2_E, NEG_INF)


def _bias_table(rpb):
    return pl.pallas_call(
        _bias_table_kernel,
        grid=(N_DROW + 1,),
        in_specs=[pl.BlockSpec(memory_space=pltpu.SMEM)],
        out_specs=pl.BlockSpec((None, N_HEADS_B // 2, GRID_W, LANES), lambda d: (d, 0, 0, 0)),
        out_shape=jax.ShapeDtypeStruct((N_DROW + 1, N_HEADS_B // 2, GRID_W, LANES), F32),
        compiler_params=_params("arbitrary"),
        name="natten_bias_table",
    )(rpb.reshape(-1))


def _attn_residual(a_ref, wo_ref, x_ref, gate_ref, g_ref, b_ref):
    o = jnp.dot(a_ref[...], wo_ref[...], preferred_element_type=F32)
    return _layer_norm(ALPHA * x_ref[...] + gate_ref[...] * o, g_ref[...], b_ref[...])


def _attn_residual_specs(stream, tm):
    vec = _full_spec((1, D_MODEL))
    return [_row_spec(tm, D_MODEL), _full_spec((D_MODEL, D_MODEL)), _row_spec(tm, D_MODEL),
            _mod_spec(stream, tm), vec, vec]


def _swiglu(h, wg_ref, wu_ref, wd_ref):
    acc = None
    for lo, hi in FF_CHUNKS:
        g = jnp.dot(h, wg_ref[:, lo:hi], preferred_element_type=F32)
        u = jnp.dot(h, wu_ref[:, lo:hi], preferred_element_type=F32)
        a = (g * jax.nn.sigmoid(g) * u).astype(BF16)
        part = jnp.dot(a, wd_ref[lo:hi, :], preferred_element_type=F32)
        acc = part if acc is None else acc + part
    return acc


def _layer_dense_kernel(a_ref, wo_ref, x_ref, gate1_ref, g1_ref, b1_ref,
                        sh_ref, sc_ref, wg_ref, wu_ref, wd_ref, gate_ref, g_ref, b_ref, o_ref):
    x = _attn_residual(a_ref, wo_ref, x_ref, gate1_ref, g1_ref, b1_ref)
    h = (x * (1.0 + sc_ref[...]) + sh_ref[...]).astype(BF16)
    y = ALPHA * x + gate_ref[...] * _swiglu(h, wg_ref, wu_ref, wd_ref)
    o_ref[...] = _layer_norm(y, g_ref[...], b_ref[...])


def _layer_dense(stream, attn, w_o, x, gate1, ln1_g, ln1_b, shift, scale, w_gu, w_down, gate, ln_g, ln_b):
    vec = _full_spec((1, D_MODEL))
    mod = _mod_spec(stream, TM_FFN)
    return pl.pallas_call(
        _layer_dense_kernel,
        grid=(stream.n // TM_FFN,),
        in_specs=_attn_residual_specs(stream, TM_FFN) + [
            mod, mod,
            pl.BlockSpec((D_MODEL, D_FF), lambda i: (0, 0)),
            pl.BlockSpec((D_MODEL, D_FF), lambda i: (0, 1)),
            _full_spec((D_FF, D_MODEL)),
            mod, vec, vec],
        out_specs=_row_spec(TM_FFN, D_MODEL),
        out_shape=jax.ShapeDtypeStruct((stream.n, D_MODEL), F32),
        compiler_params=_params("arbitrary"),
        name="wo_ffn_dense",
    )(attn, w_o, x, gate1, ln1_g, ln1_b, shift, scale, w_gu, w_gu, w_down, gate, ln_g, ln_b)


def _ffn_grouped_kernel(te_ref, tv_ref, x_ref, wg_ref, wu_ref, wd_ref, o_ref):
    i = pl.program_id(0)

    @pl.when(tv_ref[i] != 0)
    def _():
        x = _from_token_tiles(x_ref, TMG).astype(BF16)
        _to_token_tiles(o_ref, _swiglu(x, wg_ref, wu_ref, wd_ref))

    @pl.when(tv_ref[i] == 0)
    def _():
        o_ref[...] = jnp.zeros_like(o_ref)


def _ffn_grouped(tile_expert, tile_valid, xs, w_gu, w_down):
    row = pl.BlockSpec((TMG * TOKEN_TILE_ROWS, LANES), lambda i, te, tv: (i, 0))
    grid_spec = pltpu.PrefetchScalarGridSpec(
        num_scalar_prefetch=2,
        grid=(NT_MOE,),
        in_specs=[row,
                  pl.BlockSpec((None, D_MODEL, D_FF), lambda i, te, tv: (te[i], 0, 0)),
                  pl.BlockSpec((None, D_MODEL, D_FF), lambda i, te, tv: (te[i], 0, 1)),
                  pl.BlockSpec((None, D_FF, D_MODEL), lambda i, te, tv: (te[i], 0, 0))],
        out_specs=row,
    )
    return pl.pallas_call(
        _ffn_grouped_kernel,
        grid_spec=grid_spec,
        out_shape=jax.ShapeDtypeStruct((NT_MOE * TMG * TOKEN_TILE_ROWS, LANES), F32),
        compiler_params=_params("arbitrary"),
        name="ffn_grouped",
    )(tile_expert, tile_valid, xs, w_gu, w_gu, w_down)


META_E0, META_E1, META_W0, META_W1, META_R0, META_R1 = range(6)


def _router_kernel(a_ref, wo_ref, x_ref, gate1_ref, g1_ref, b1_ref, sh_ref, sc_ref, rw_ref, cnt_in_ref,
                   xo_ref, h_ref, meta_ref, cnt_ref, carry_scr):
    i = pl.program_id(0)

    @pl.when(i == 0)
    def _():
        carry_scr[...] = cnt_in_ref[...]

    x = _attn_residual(a_ref, wo_ref, x_ref, gate1_ref, g1_ref, b1_ref)
    xo_ref[...] = x
    h = x * (1.0 + sc_ref[...]) + sh_ref[...]
    _to_token_tiles(h_ref, h)
    w = rw_ref[...]
    h_hi = h.astype(BF16)
    w_hi = w.astype(BF16)
    h_lo = (h - h_hi.astype(F32)).astype(BF16)
    w_lo = (w - w_hi.astype(F32)).astype(BF16)
    logits = (jnp.dot(h_hi, w_hi, preferred_element_type=F32)
              + (jnp.dot(h_lo, w_hi, preferred_element_type=F32)
                 + jnp.dot(h_hi, w_lo, preferred_element_type=F32)))
    lane = lax.broadcasted_iota(jnp.int32, logits.shape, 1).astype(F32)
    lg = jnp.where(lane < N_EXPERTS, logits, -jnp.inf)
    m1 = jnp.max(lg, axis=-1, keepdims=True)
    i1 = jnp.min(jnp.where(lg == m1, lane, float(LANES)), axis=-1, keepdims=True)
    lg2 = jnp.where(lane == i1, -jnp.inf, lg)
    m2 = jnp.max(lg2, axis=-1, keepdims=True)
    i2 = jnp.min(jnp.where(lg2 == m2, lane, float(LANES)), axis=-1, keepdims=True)
    e = jnp.exp(m2 - m1)
    w1 = 1.0 / (1.0 + e)
    w2 = e / (1.0 + e)

    sel1 = lane == i1
    sel2 = lane == i2
    onehot = jnp.where(sel1 | sel2, 1.0, 0.0)
    rr = lax.broadcasted_iota(jnp.int32, (TM, TM), 0)
    cc = lax.broadcasted_iota(jnp.int32, (TM, TM), 1)
    lower = jnp.where(cc < rr, 1.0, 0.0).astype(BF16)
    before = jnp.dot(lower, onehot.astype(BF16), preferred_element_type=F32) + carry_scr[0:1, :]
    r1 = jnp.sum(jnp.where(sel1, before, 0.0), axis=-1, keepdims=True)
    r2 = jnp.sum(jnp.where(sel2, before, 0.0), axis=-1, keepdims=True)
    carry_scr[...] = carry_scr[...] + jnp.sum(onehot, axis=0, keepdims=True)
    cnt_ref[...] = carry_scr[...]

    cols = [i1, i2, w1, w2, r1, r2]
    meta = jnp.zeros(logits.shape, F32)
    for c, val in enumerate(cols):
        meta = jnp.where(lane == c, val, meta)
    meta_ref[...] = meta


def _router(stream, attn, w_o, x, gate1, ln1_g, ln1_b, shift, scale, rw_pad, cnt_in):
    cnt_spec = _full_spec((SUBLANES, LANES))
    return pl.pallas_call(
        _router_kernel,
        grid=(stream.n // TM,),
        in_specs=_attn_residual_specs(stream, TM) + [
            _mod_spec(stream, TM), _mod_spec(stream, TM), _full_spec((D_MODEL, LANES)), cnt_spec],
        out_specs=[_row_spec(TM, D_MODEL), _row_spec(TM * TOKEN_TILE_ROWS, LANES), _row_spec(TM, LANES), cnt_spec],
        out_shape=[jax.ShapeDtypeStruct((stream.n, D_MODEL), F32),
                   jax.ShapeDtypeStruct((stream.n * TOKEN_TILE_ROWS, LANES), F32),
                   jax.ShapeDtypeStruct((stream.n, LANES), F32),
                   jax.ShapeDtypeStruct((SUBLANES, LANES), F32)],
        scratch_shapes=[pltpu.VMEM((SUBLANES, LANES), F32)],
        compiler_params=_params("arbitrary"),
        name="wo_moe_router",
    )(attn, w_o, x, gate1, ln1_g, ln1_b, shift, scale, rw_pad, cnt_in)


def _tile_rows(token, count=1):
    start = token * TOKEN_TILE_ROWS
    if not isinstance(start, int):
        start = pl.multiple_of(start, TOKEN_TILE_ROWS)
    return pl.ds(start, count * TOKEN_TILE_ROWS)


def _to_token_tiles(ref, x):
    for j in range(TOKEN_TILE_ROWS):
        ref[pl.ds(j, x.shape[0], stride=TOKEN_TILE_ROWS), :] = x[:, j * LANES:(j + 1) * LANES]


def _from_token_tiles(ref, tokens):
    return jnp.concatenate([ref[pl.ds(j, tokens, stride=TOKEN_TILE_ROWS), :] for j in range(TOKEN_TILE_ROWS)], axis=1)


def _row_copy(src_ref, src_row, dst_ref, dst_row, sem):
    return pltpu.make_async_copy(src_ref.at[_tile_rows(src_row)], dst_ref.at[_tile_rows(dst_row)], sem)


def _scatter_rows(d0_ref, d1_ref, h_ref, buf_ref, sem):
    base = pl.program_id(0) * TM
    for r in range(TM):
        _row_copy(h_ref, r, buf_ref, d0_ref[base + r], sem).start(priority=0)
        _row_copy(h_ref, r, buf_ref, d1_ref[base + r], sem).start(priority=1)


def _scatter_wait(h_ref, buf_ref, sem, n_rows):
    pltpu.make_async_copy(h_ref.at[_tile_rows(0, n_rows)], buf_ref.at[_tile_rows(0, n_rows)], sem).wait()


def _scatter_kernel(d0_ref, d1_ref, pad_ref, hc_ref, hl_ref, buf_ref, zero_scr, sem, *, ctx_steps, pads_per_step):
    i = pl.program_id(0)

    @pl.when(i < ctx_steps)
    def _():
        _scatter_rows(d0_ref, d1_ref, hc_ref, buf_ref, sem)
        zero_scr[...] = jnp.zeros_like(zero_scr)
        pbase = i * pads_per_step
        for r in range(pads_per_step):
            _row_copy(zero_scr, 0, buf_ref, pad_ref[pbase + r], sem).start(priority=r % 2)
        _scatter_wait(hc_ref, buf_ref, sem, TM)
        _scatter_wait(hc_ref, buf_ref, sem, TM)
        _scatter_wait(hc_ref, buf_ref, sem, pads_per_step)

    @pl.when(i >= ctx_steps)
    def _():
        _scatter_rows(d0_ref, d1_ref, hl_ref, buf_ref, sem)
        _scatter_wait(hl_ref, buf_ref, sem, TM)
        _scatter_wait(hl_ref, buf_ref, sem, TM)


def _scatter(d0, d1, pad_rows, h_ctx, h_lat):
    ctx_steps = CTX.n // TM
    steps = ctx_steps + LAT.n // TM
    assert N_PAD_ROWS % ctx_steps == 0
    grid_spec = pltpu.PrefetchScalarGridSpec(
        num_scalar_prefetch=3,
        grid=(steps,),
        in_specs=[pl.BlockSpec((TM * TOKEN_TILE_ROWS, LANES), lambda i, a, b, c: (jnp.minimum(i, ctx_steps - 1), 0)),
                  pl.BlockSpec((TM * TOKEN_TILE_ROWS, LANES), lambda i, a, b, c: (jnp.maximum(i - ctx_steps, 0), 0))],
        out_specs=pl.BlockSpec(memory_space=pl.ANY),
        scratch_shapes=[pltpu.VMEM((TOKEN_TILE_ROWS, LANES), F32), pltpu.SemaphoreType.DMA],
    )
    return pl.pallas_call(
        functools.partial(_scatter_kernel, ctx_steps=ctx_steps, pads_per_step=N_PAD_ROWS // ctx_steps),
        grid_spec=grid_spec,
        out_shape=jax.ShapeDtypeStruct((NT_MOE * TMG * TOKEN_TILE_ROWS, LANES), F32),
        compiler_params=_params("arbitrary"),
        name="moe_scatter",
    )(d0, d1, pad_rows, h_ctx, h_lat)


def _gather_start(d0_ref, d1_ref, ys_ref, rows_scr, sems, tile, slot):
    base = tile * TM
    for r in range(TM):
        _row_copy(ys_ref, d0_ref[base + r], rows_scr.at[slot, 0], r, sems.at[slot]).start(priority=0)
        _row_copy(ys_ref, d1_ref[base + r], rows_scr.at[slot, 1], r, sems.at[slot]).start(priority=1)


def _gather_wait(ys_ref, rows_scr, sems, slot):
    for s in range(2):
        pltpu.make_async_copy(ys_ref.at[_tile_rows(0, TM)], rows_scr.at[slot, s], sems.at[slot]).wait()


def _combine_kernel(d0_ref, d1_ref, ys_ref, meta_ref, x_ref, gate_ref, g_ref, b_ref, o_ref, rows_scr, sems,
                    *, n_tiles):
    i = pl.program_id(0)
    slot = i % 2

    @pl.when(i == 0)
    def _():
        _gather_start(d0_ref, d1_ref, ys_ref, rows_scr, sems, 0, 0)

    _gather_start(d0_ref, d1_ref, ys_ref, rows_scr, sems, jnp.minimum(i + 1, n_tiles - 1), 1 - slot)
    _gather_wait(ys_ref, rows_scr, sems, slot)

    meta = meta_ref[...]
    w0 = meta[:, META_W0:META_W0 + 1]
    w1 = meta[:, META_W1:META_W1 + 1]
    f = w0 * _from_token_tiles(rows_scr.at[slot, 0], TM) + w1 * _from_token_tiles(rows_scr.at[slot, 1], TM)
    y = ALPHA * x_ref[...] + gate_ref[...] * f
    o_ref[...] = _layer_norm(y, g_ref[...], b_ref[...])

    @pl.when(i == n_tiles - 1)
    def _():
        _gather_wait(ys_ref, rows_scr, sems, 1 - slot)


def _combine(stream, d0, d1, ys, meta, x, gate, ln_g, ln_b):
    vec = pl.BlockSpec((1, D_MODEL), lambda i, a, b: (0, 0))
    grid_spec = pltpu.PrefetchScalarGridSpec(
        num_scalar_prefetch=2,
        grid=(stream.n // TM,),
        in_specs=[pl.BlockSpec(memory_space=pl.ANY), _row_spec(TM, LANES), _row_spec(TM, D_MODEL),
                  _mod_spec(stream, TM), vec, vec],
        out_specs=_row_spec(TM, D_MODEL),
        scratch_shapes=[pltpu.VMEM((2, 2, TM * TOKEN_TILE_ROWS, LANES), F32), pltpu.SemaphoreType.DMA((2,))],
    )
    return pl.pallas_call(
        functools.partial(_combine_kernel, n_tiles=stream.n // TM),
        grid_spec=grid_spec,
        out_shape=jax.ShapeDtypeStruct((stream.n, D_MODEL), F32),
        compiler_params=_params("arbitrary"),
        name="moe_combine",
    )(d0, d1, ys, meta, x, gate, ln_g, ln_b)


def _rope_tables():
    t = jnp.arange(DEC_SEQ)
    row = (t // GRID_W).astype(F32)
    col = (t % GRID_W).astype(F32)
    freqs = ROPE_THETA ** (-jnp.arange(ROT_FREQS, dtype=F32) / ROT_FREQS)
    ar = row[:, None] * freqs
    ac = col[:, None] * freqs
    cos = jnp.concatenate([jnp.cos(ar), jnp.cos(ar), jnp.cos(ac), jnp.cos(ac)], axis=1)
    sin = jnp.concatenate([-jnp.sin(ar), jnp.sin(ar), -jnp.sin(ac), jnp.sin(ac)], axis=1)
    return cos, sin


def _routing_plan(metas, counts):
    cnt = counts[0, :N_EXPERTS].astype(jnp.int32)
    tiles_e = (cnt + TMG - 1) // TMG
    tile_end = jnp.cumsum(tiles_e)
    tile_start = tile_end - tiles_e
    offs = tile_start * TMG
    dests = []
    for meta in metas:
        pair = []
        for ecol, rcol in ((META_E0, META_R0), (META_E1, META_R1)):
            e = meta[:, ecol].astype(jnp.int32)
            d = meta[:, rcol].astype(jnp.int32)
            for k in range(N_EXPERTS):
                d = d + jnp.where(e == k, offs[k], 0)
            pair.append(d)
        dests.append(tuple(pair))
    tid = jnp.arange(NT_MOE, dtype=jnp.int32)
    te = jnp.minimum(jnp.sum((tid[:, None] >= tile_end[None, :]).astype(jnp.int32), axis=1), N_EXPERTS - 1)
    total = tile_end[-1]
    valid = tid < total
    te_last = jnp.max(jnp.where(valid, te, 0))
    te = jnp.where(valid, te, te_last)
    pad_cnt = tiles_e * TMG - cnt
    pad_end = jnp.cumsum(pad_cnt)
    pad_start = pad_end - pad_cnt
    k = jnp.arange(N_PAD_ROWS, dtype=jnp.int32)
    grp = jnp.sum((k[:, None] >= pad_end[None, :]).astype(jnp.int32), axis=1)
    pad_rows = total * TMG + (k - pad_end[-1])
    for e in range(N_EXPERTS):
        pad_rows = jnp.where(grp == e, offs[e] + cnt[e] + (k - pad_start[e]), pad_rows)
    return dests, te, valid.astype(jnp.int32), pad_rows


def kernel(x_prompt, x_sample, cache_k_a, cache_v_a, cache_k_b, cache_v_b, c, c_ctx, ada_w, ada_b, ln_attn_g, ln_attn_b, ln_ffn_g, ln_ffn_b, wqkv_a, qnorm_a, knorm_a, wo_a, wqkv_b, rpb_b, wo_b, ffn_w_gu, ffn_w_down, router_w, moe_w_gu, moe_w_down):
    streams = (CTX, LAT)
    xs = [x_prompt.reshape(N_CTX, D_MODEL), x_sample.reshape(N_LAT, D_MODEL)]

    cond = jnp.zeros((GROUP_PAD, D_MODEL), F32).at[0].set(c_ctx).at[1:N_GROUPS].set(c)
    mods = _ada_mods(cond, ada_w, ada_b)
    vec = lambda a, l: a[l].reshape(1, D_MODEL)

    m = mods[0]
    w_qkv = wqkv_a[0].astype(BF16)
    w_o = wo_a[0].astype(BF16)
    w_gu = ffn_w_gu[0].astype(BF16)
    w_dn = ffn_w_down[0].astype(BF16)
    gains = (qnorm_a[0].reshape(1, HEAD_DIM_A), knorm_a[0].reshape(1, HEAD_DIM_A))
    nq, nk = N_HEADS_A * HEAD_DIM_A, N_KV_A * HEAD_DIM_A
    qscale = HEAD_DIM_A ** -0.5

    qp, kp, vp, kp32, vp32 = _qkv(CTX, xs[0], m[0], m[1], w_qkv, nq=nq, nk=nk, qscale=qscale, gains=gains,
                                  emit_f32=True)
    new_k_a = kp32.reshape(BATCH, 1, SEQ, N_KV_A, HEAD_DIM_A)
    new_v_a = vp32.reshape(BATCH, 1, SEQ, N_KV_A, HEAD_DIM_A)
    ql, kl, vlt = _qkv(LAT, xs[1], m[0], m[1], w_qkv, nq=nq, nk=nk, qscale=qscale * LOG2_E, gains=gains,
                       rope_tables=_rope_tables(), v_layout="t")
    cache_k = cache_k_a[:, 0].reshape(DEC_BATCH, PAST_LEN, nk).astype(BF16)
    cache_vt = jnp.swapaxes(cache_v_a[:, 0].reshape(DEC_BATCH, PAST_LEN, nk), 1, 2).astype(BF16)
    attn = [_gqa_ctx(qp, kp, vp), _gqa_lat_t(ql, kl, vlt, cache_k, cache_vt)]
    xs = [_layer_dense(s, a, w_o, x, m[2], vec(ln_attn_g, 0), vec(ln_attn_b, 0),
                       m[3], m[4], w_gu, w_dn, m[5], vec(ln_ffn_g, 0), vec(ln_ffn_b, 0))
          for s, a, x in zip(streams, attn, xs)]

    m = mods[1]
    w_qkv = wqkv_b[0].astype(BF16)
    w_o = wo_b[0].astype(BF16)
    qscale = HEAD_DIM_B ** -0.5
    qp, kp, vp, kp32, vp32 = _qkv(CTX, xs[0], m[0], m[1], w_qkv, nq=D_MODEL, nk=D_MODEL, qscale=qscale,
                                  emit_f32=True)
    new_k_b = kp32.reshape(BATCH, 1, SEQ, N_HEADS_B, HEAD_DIM_B)
    new_v_b = vp32.reshape(BATCH, 1, SEQ, N_HEADS_B, HEAD_DIM_B)
    ql, kl, vlt = _qkv(LAT, xs[1], m[0], m[1], w_qkv, nq=D_MODEL, nk=D_MODEL, qscale=qscale * LOG2_E,
                       v_layout="t_blocked")
    cache_k = cache_k_b[:, 0].reshape(DEC_BATCH, PAST_LEN, D_MODEL).astype(BF16)
    cache_vt = jnp.swapaxes(cache_v_b[:, 0].reshape(DEC_BATCH, PAST_LEN, D_MODEL), 1, 2).astype(BF16)
    attn = [_mha_ctx(qp, kp, vp), _natten(ql, kl, vlt, cache_k, cache_vt, _bias_table(rpb_b[0]))]

    rw_pad = jnp.zeros((D_MODEL, LANES), F32).at[:, :N_EXPERTS].set(router_w[0])
    counts = jnp.zeros((SUBLANES, LANES), F32)
    x_mid, hs, metas = [], [], []
    for s, a, x in zip(streams, attn, xs):
        x1, h, meta, counts = _router(s, a, w_o, x, m[2], vec(ln_attn_g, 1), vec(ln_attn_b, 1),
                                      m[3], m[4], rw_pad, counts)
        x_mid.append(x1)
        hs.append(h)
        metas.append(meta)
    xs = x_mid
    dests, tile_expert, tile_valid, pad_rows = _routing_plan(metas, counts)
    d0_all = jnp.concatenate([dests[0][0], dests[1][0]])
    d1_all = jnp.concatenate([dests[0][1], dests[1][1]])
    sorted_rows = _scatter(d0_all, d1_all, pad_rows, hs[0], hs[1])
    ys = _ffn_grouped(tile_expert, tile_valid, sorted_rows, moe_w_gu[0].astype(BF16), moe_w_down[0].astype(BF16))
    outs = [_combine(s, d0, d1, ys, meta, x, m[5], vec(ln_ffn_g, 1), vec(ln_ffn_b, 1))
            for s, (d0, d1), meta, x in zip(streams, dests, metas, xs)]

    y_prompt = outs[0].reshape(BATCH, SEQ, D_MODEL)
    y_sample = outs[1].reshape(DEC_BATCH, DEC_SEQ, D_MODEL)
    return (y_prompt, y_sample, new_k_a, new_v_a, new_k_b, new_v_b)
```

```python
import functools
import math

import jax
import jax.numpy as jnp
from jax import lax
from jax.experimental import pallas as pl
from jax.experimental.pallas import tpu as pltpu

F32 = jnp.float32
BF16 = jnp.bfloat16
HIGHEST = lax.Precision.HIGHEST

D_MODEL = 1024
BATCH, SEQ = 32, 256
DEC_BATCH, DEC_SEQ = 4, 4096
PAST_LEN = 256
DEPTH = 2
GRID_W = 64
GRID_H = DEC_SEQ // GRID_W
N_HEADS_A, N_KV_A, HEAD_DIM_A = 8, 2, 128
ROT_FREQS = HEAD_DIM_A // 4
ROPE_THETA = 10000.0
N_HEADS_B, HEAD_DIM_B = 16, 64
WIN_H, WIN_W = 8, 16
D_FF = 2816
N_EXPERTS = 8
EPS = 1e-6
NEG_INF = -1e30
ALPHA = (2.0 * DEPTH) ** 0.25
LOG2_E = math.log2(math.e)

N_CTX = BATCH * SEQ
N_LAT = DEC_BATCH * DEC_SEQ
N_GROUPS = 1 + DEC_BATCH
GROUP_PAD = 8

LANES = 128
SUBLANES = 8
VMEM_LIMIT = 56 * 2**20

TM = 512
MXU_DIM = 256
FF_CHUNK = 4 * MXU_DIM
FF_CHUNKS = tuple((lo, min(lo + FF_CHUNK, D_FF)) for lo in range(0, D_FF, FF_CHUNK))
assert D_FF % MXU_DIM == 0
TM_FFN = 512
TMG = 512
TQ_A = 256
UNIT_A = 2
TN_ADA = 1536
N_PAIRS = 2 * (N_CTX + N_LAT)
NT_MOE = N_PAIRS // TMG + N_EXPERTS
N_PAD_ROWS = NT_MOE * TMG - N_PAIRS
TOKEN_TILE_ROWS = D_MODEL // LANES
assert TOKEN_TILE_ROWS == SUBLANES


class Stream:
    def __init__(self, n_rows, group_offset, rows_per_group):
        self.n = n_rows
        self.goff = group_offset
        self.rpg = rows_per_group

    def group_map(self, tm):
        tiles_per_group = self.rpg // tm
        goff = self.goff
        return lambda i, *_: (goff + i // tiles_per_group, 0, 0)


CTX = Stream(N_CTX, 0, N_CTX)
LAT = Stream(N_LAT, 1, DEC_SEQ)


def _params(*sem):
    return pltpu.CompilerParams(dimension_semantics=sem, vmem_limit_bytes=VMEM_LIMIT)


def _mod_spec(stream, tm):
    return pl.BlockSpec((None, 1, D_MODEL), stream.group_map(tm))


def _row_spec(tm, width):
    return pl.BlockSpec((tm, width), lambda i, *_: (i, 0))


def _full_spec(shape):
    nd = len(shape)
    return pl.BlockSpec(shape, lambda *_: (0,) * nd)


def _layer_norm(y, g, b):
    mu = jnp.mean(y, axis=-1, keepdims=True)
    d = y - mu
    var = jnp.mean(d * d, axis=-1, keepdims=True)
    return d * lax.rsqrt(var + EPS) * g + b


def _ada_kernel(c_ref, w_ref, b_ref, o_ref):
    c = c_ref[...]
    s = c * jax.nn.sigmoid(c)
    o_ref[...] = jnp.dot(s, w_ref[...], preferred_element_type=F32, precision=HIGHEST) + b_ref[...]


def _ada_mods(cond, ada_w, ada_b):
    n_out = 6 * D_MODEL
    out = pl.pallas_call(
        _ada_kernel,
        grid=(DEPTH, n_out // TN_ADA),
        in_specs=[
            pl.BlockSpec((GROUP_PAD, D_MODEL), lambda l, n: (0, 0)),
            pl.BlockSpec((None, D_MODEL, TN_ADA), lambda l, n: (l, 0, n)),
            pl.BlockSpec((None, 1, TN_ADA), lambda l, n: (l, 0, n)),
        ],
        out_specs=pl.BlockSpec((None, GROUP_PAD, TN_ADA), lambda l, n: (l, 0, n)),
        out_shape=jax.ShapeDtypeStruct((DEPTH, GROUP_PAD, n_out), F32),
        compiler_params=_params("arbitrary", "arbitrary"),
        name="ada_mods",
    )(cond, ada_w, ada_b.reshape(DEPTH, 1, n_out))
    out = out.reshape(DEPTH, GROUP_PAD, 6, D_MODEL).transpose(0, 2, 1, 3)
    return out[:, :, :, None, :]


def _swap_halves(t):
    lane = lax.broadcasted_iota(jnp.int32, t.shape, 1)
    fwd = pltpu.roll(t, LANES - ROT_FREQS, 1)
    bwd = pltpu.roll(t, ROT_FREQS, 1)
    return jnp.where((lane % (2 * ROT_FREQS)) < ROT_FREQS, fwd, bwd)


def _qkv_kernel(*refs, nq, nk, norm, rope, emit_f32, v_layout, qscale):
    refs = list(refs)
    x_ref, sh_ref, sc_ref, w_ref = refs[:4]
    pos = 4
    if norm:
        qg_ref, kg_ref = refs[pos:pos + 2]
        pos += 2
    if rope:
        cos_ref, sin_ref = refs[pos:pos + 2]
        pos += 2
    q_ref, k_ref, v_ref = refs[pos:pos + 3]
    pos += 3
    if emit_f32:
        kf_ref, vf_ref = refs[pos:pos + 2]

    h = (x_ref[...] * (1.0 + sc_ref[...]) + sh_ref[...]).astype(BF16)
    qkv = jnp.dot(h, w_ref[...], preferred_element_type=F32)
    if norm:
        n_heads = (nq + nk) // HEAD_DIM_A
        for hd in range(n_heads):
            lo = hd * HEAD_DIM_A
            t = qkv[:, lo:lo + HEAD_DIM_A]
            ms = jnp.mean(t * t, axis=-1, keepdims=True)
            gain = qg_ref[...] if lo < nq else kg_ref[...]
            t = t * lax.rsqrt(ms + EPS) * gain
            if rope:
                t = t * cos_ref[...] + _swap_halves(t) * sin_ref[...]
            if lo < nq:
                q_ref[:, lo:lo + HEAD_DIM_A] = (t * qscale).astype(BF16)
            else:
                k_ref[:, lo - nq:lo - nq + HEAD_DIM_A] = t.astype(BF16)
                if emit_f32:
                    kf_ref[:, lo - nq:lo - nq + HEAD_DIM_A] = t
    else:
        q_ref[...] = (qkv[:, :nq] * qscale).astype(BF16)
        k = qkv[:, nq:nq + nk]
        k_ref[...] = k.astype(BF16)
        if emit_f32:
            kf_ref[...] = k
    v = qkv[:, nq + nk:]
    if v_layout == "rows":
        v_ref[...] = v.astype(BF16)
    elif v_layout == "t":
        v_ref[...] = v.T.astype(BF16)
    else:
        vt = v.T.astype(BF16)
        for j in range(vt.shape[1] // LANES):
            v_ref[j] = vt[:, j * LANES:(j + 1) * LANES]
    if emit_f32:
        vf_ref[...] = v


def _qkv(stream, x, shift, scale, w, *, nq, nk, qscale, gains=None, rope_tables=None, emit_f32=False,
         v_layout="rows"):
    n = stream.n
    nw = w.shape[1]
    norm = gains is not None
    rope = rope_tables is not None
    in_specs = [_row_spec(TM, D_MODEL), _mod_spec(stream, TM), _mod_spec(stream, TM), _full_spec((D_MODEL, nw))]
    args = [x, shift, scale, w]
    if norm:
        in_specs += [_full_spec((1, HEAD_DIM_A))] * 2
        args += list(gains)
    if rope:
        tiles_per_seq = DEC_SEQ // TM
        tbl_spec = pl.BlockSpec((TM, HEAD_DIM_A), lambda i: (i % tiles_per_seq, 0))
        in_specs += [tbl_spec, tbl_spec]
        args += list(rope_tables)
    out_specs = [_row_spec(TM, nq), _row_spec(TM, nk), _row_spec(TM, nk)]
    out_shape = [jax.ShapeDtypeStruct((n, nq), BF16), jax.ShapeDtypeStruct((n, nk), BF16),
                 jax.ShapeDtypeStruct((n, nk), BF16)]
    if v_layout == "t":
        out_specs[2] = pl.BlockSpec((nk, TM), lambda i: (0, i))
        out_shape[2] = jax.ShapeDtypeStruct((nk, n), BF16)
    elif v_layout == "t_blocked":
        out_specs[2] = pl.BlockSpec((TM // LANES, nk, LANES), lambda i: (i, 0, 0))
        out_shape[2] = jax.ShapeDtypeStruct((n // LANES, nk, LANES), BF16)
    if emit_f32:
        out_specs += [_row_spec(TM, nk), _row_spec(TM, nk)]
        out_shape += [jax.ShapeDtypeStruct((n, nk), F32)] * 2
    return pl.pallas_call(
        functools.partial(_qkv_kernel, nq=nq, nk=nk, norm=norm, rope=rope, emit_f32=emit_f32,
                          v_layout=v_layout, qscale=qscale),
        grid=(n // TM,),
        in_specs=in_specs,
        out_specs=out_specs,
        out_shape=out_shape,
        compiler_params=_params("arbitrary"),
        name="qkv_norm_rope" if norm else "qkv",
    )(*args)


def _attend(q, kvs, biases=None):
    scores = []
    for idx, (k, _) in enumerate(kvs):
        s = lax.dot_general(q, k, (((1,), (1,)), ((), ())), preferred_element_type=F32)
        if biases is not None and biases[idx] is not None:
            s = s + biases[idx]
        scores.append(s)
    m = jnp.max(scores[0], axis=-1, keepdims=True)
    for s in scores[1:]:
        m = jnp.maximum(m, jnp.max(s, axis=-1, keepdims=True))
    denom = None
    out = None
    for s, (_, v) in zip(scores, kvs):
        p = jnp.exp(s - m)
        part = jnp.sum(p, axis=-1, keepdims=True)
        pv = jnp.dot(p.astype(BF16), v, preferred_element_type=F32)
        denom = part if denom is None else denom + part
        out = pv if out is None else out + pv
    return out * (1.0 / denom)


def _gqa_ctx_kernel(q_ref, k_ref, v_ref, o_ref):
    d = HEAD_DIM_A
    group = N_HEADS_A // N_KV_A
    for g in range(N_KV_A):
        heads = [g * group + j for j in range(group)]
        qs = jnp.concatenate([q_ref[:, h * d:(h + 1) * d] for h in heads], axis=0)
        o = _attend(qs, [(k_ref[:, g * d:(g + 1) * d], v_ref[:, g * d:(g + 1) * d])])
        for j, h in enumerate(heads):
            o_ref[:, h * d:(h + 1) * d] = o[j * SEQ:(j + 1) * SEQ].astype(o_ref.dtype)


def _gqa_ctx(q, k, v):
    nq, nk = N_HEADS_A * HEAD_DIM_A, N_KV_A * HEAD_DIM_A
    return pl.pallas_call(
        _gqa_ctx_kernel,
        grid=(BATCH,),
        in_specs=[_row_spec(SEQ, nq), _row_spec(SEQ, nk), _row_spec(SEQ, nk)],
        out_specs=_row_spec(SEQ, nq),
        out_shape=jax.ShapeDtypeStruct((N_CTX, nq), BF16),
        compiler_params=_params("arbitrary"),
        name="gqa_ctx",
    )(q, k, v)


ONES_ROWS = 16


def _scores_t(q, ks):
    return [lax.dot_general(k, q, (((1,), (1,)), ((), ())), preferred_element_type=F32) for k in ks]


def _softmax_pv_t(scores, vts):
    m = jnp.max(scores[0], axis=0, keepdims=True)
    for s in scores[1:]:
        m = jnp.maximum(m, jnp.max(s, axis=0, keepdims=True))
    acc = None
    for s, vt in zip(scores, vts):
        p = jnp.exp2(s - m).astype(BF16)
        vt_ones = jnp.concatenate([vt, jnp.ones((ONES_ROWS, vt.shape[1]), BF16)], axis=0)
        pv = jnp.dot(vt_ones, p, preferred_element_type=F32)
        acc = pv if acc is None else acc + pv
    d = LANES
    return (acc[:d] * (1.0 / acc[d:d + 1])).T


def _gqa_t_kernel(q_ref, k_ref, vt_ref, kc_ref, vct_ref, o_ref, *, n_kv, group, unit, tq):
    d = HEAD_DIM_A
    units = [(g, [g * group + u0 + j for j in range(unit)]) for g in range(n_kv) for u0 in range(0, group, unit)]
    def unit_scores(g, heads):
        ds = slice(g * d, (g + 1) * d)
        qs = jnp.concatenate([q_ref[:, h * d:(h + 1) * d] for h in heads], axis=0)
        return _scores_t(qs, [k_ref[:, ds], kc_ref[:, ds]])

    nxt = unit_scores(*units[0])
    for idx, (g, heads) in enumerate(units):
        sc = nxt
        if idx + 1 < len(units):
            nxt = unit_scores(*units[idx + 1])
        ds = slice(g * d, (g + 1) * d)
        o = _softmax_pv_t(sc, [vt_ref[ds, :], vct_ref[ds, :]])
        for j, h in enumerate(heads):
            o_ref[:, h * d:(h + 1) * d] = o[j * tq:(j + 1) * tq].astype(o_ref.dtype)


def _gqa_lat_t(q, k, vt, kc, vct):
    nq, nk = N_HEADS_A * HEAD_DIM_A, N_KV_A * HEAD_DIM_A
    tiles = DEC_SEQ // TQ_A
    q_spec = pl.BlockSpec((TQ_A, nq), lambda b, t: (b * tiles + t, 0))
    return pl.pallas_call(
        functools.partial(_gqa_t_kernel, n_kv=N_KV_A, group=N_HEADS_A // N_KV_A, unit=UNIT_A, tq=TQ_A),
        grid=(DEC_BATCH, tiles),
        in_specs=[q_spec,
                  pl.BlockSpec((DEC_SEQ, nk), lambda b, t: (b, 0)),
                  pl.BlockSpec((nk, DEC_SEQ), lambda b, t: (0, b)),
                  pl.BlockSpec((None, PAST_LEN, nk), lambda b, t: (b, 0, 0)),
                  pl.BlockSpec((None, nk, PAST_LEN), lambda b, t: (b, 0, 0))],
        out_specs=q_spec,
        out_shape=jax.ShapeDtypeStruct((N_LAT, nq), BF16),
        compiler_params=_params("arbitrary", "arbitrary"),
        name="gqa_lat",
    )(q, k, vt, kc, vct)


def _split_pair(qp):
    lo = lax.broadcasted_iota(jnp.int32, qp.shape, 1) < HEAD_DIM_B
    zero = jnp.zeros_like(qp)
    return jnp.concatenate([jnp.where(lo, qp, zero), jnp.where(lo, zero, qp)], axis=0)


def _merge_pair(o2, rows):
    lo = lax.broadcasted_iota(jnp.int32, (rows, LANES), 1) < HEAD_DIM_B
    return jnp.where(lo, o2[:rows], o2[rows:])


def _mha_ctx_kernel(q_ref, k_ref, v_ref, o_ref):
    for p in range(N_HEADS_B // 2):
        sl = slice(p * LANES, (p + 1) * LANES)
        o2 = _attend(_split_pair(q_ref[:, sl]), [(k_ref[:, sl], v_ref[:, sl])])
        o_ref[:, sl] = _merge_pair(o2, SEQ).astype(o_ref.dtype)


def _mha_ctx(q, k, v):
    return pl.pallas_call(
        _mha_ctx_kernel,
        grid=(BATCH,),
        in_specs=[_row_spec(SEQ, D_MODEL)] * 3,
        out_specs=_row_spec(SEQ, D_MODEL),
        out_shape=jax.ShapeDtypeStruct((N_CTX, D_MODEL), BF16),
        compiler_params=_params("arbitrary"),
        name="mha_ctx",
    )(q, k, v)


NAT_HALF = D_MODEL
NAT_PAIRS = NAT_HALF // LANES
NAT_R = 4
NAT_WROWS = 12
N_DROW = 2 * WIN_H - 1
MASKED_TILE = N_DROW


def _natten_kernel(q_ref, k_ref, vt_ref, kc_ref, vct_ref, tbl_ref, o_ref):
    r0 = pl.program_id(2) * NAT_R
    ws = jnp.clip(r0 - WIN_H // 2, 0, GRID_H - NAT_WROWS)
    row0 = pl.multiple_of(ws * GRID_W, LANES)
    blk0 = ws * GRID_W // LANES
    n_loc = NAT_WROWS * GRID_W

    def tile_index(a, w):
        r = r0 + a
        start = jnp.clip(r - WIN_H // 2, 0, GRID_H - WIN_H)
        wr = ws + w
        valid = (wr >= start) & (wr < start + WIN_H)
        return jnp.where(valid, wr - r + (WIN_H - 1), MASKED_TILE)

    idx = [[tile_index(a, w) for w in range(NAT_WROWS)] for a in range(NAT_R)]

    def pair_scores(p):
        sl = slice(p * LANES, (p + 1) * LANES)
        q2 = jnp.concatenate([_split_pair(q_ref[a * GRID_W:(a + 1) * GRID_W, sl]) for a in range(NAT_R)], axis=0)
        bias_t = jnp.concatenate(
            [jnp.concatenate([tbl_ref[idx[a][w], p] for w in range(NAT_WROWS)], axis=0) for a in range(NAT_R)],
            axis=1)
        s_loc, s_ctx = _scores_t(q2, [k_ref[pl.ds(row0, n_loc), sl], kc_ref[:, sl]])
        return [s_loc + bias_t, s_ctx]

    nxt = pair_scores(0)
    for p in range(NAT_PAIRS):
        sl = slice(p * LANES, (p + 1) * LANES)
        sc = nxt
        if p + 1 < NAT_PAIRS:
            nxt = pair_scores(p + 1)
        vt_win = jnp.concatenate([vt_ref[blk0 + j, sl, :] for j in range(n_loc // LANES)], axis=1)
        o2 = _softmax_pv_t(sc, [vt_win, vct_ref[sl, :]])
        for a in range(NAT_R):
            o_ref[a * GRID_W:(a + 1) * GRID_W, sl] = _merge_pair(
                o2[a * LANES:(a + 1) * LANES], GRID_W).astype(o_ref.dtype)


def _natten(q, k, vt, kc, vct, tbl):
    n_half = D_MODEL // NAT_HALF
    blocks = GRID_H // NAT_R
    seq_blocks = DEC_SEQ // LANES
    q_spec = pl.BlockSpec((NAT_R * GRID_W, NAT_HALF), lambda b, hh, r: (b * blocks + r, hh))
    return pl.pallas_call(
        _natten_kernel,
        grid=(DEC_BATCH, n_half, blocks),
        in_specs=[q_spec,
                  pl.BlockSpec((DEC_SEQ, NAT_HALF), lambda b, hh, r: (b, hh)),
                  pl.BlockSpec((seq_blocks, NAT_HALF, LANES), lambda b, hh, r: (b, hh, 0)),
                  pl.BlockSpec((None, PAST_LEN, NAT_HALF), lambda b, hh, r: (b, 0, hh)),
                  pl.BlockSpec((None, NAT_HALF, PAST_LEN), lambda b, hh, r: (b, hh, 0)),
                  pl.BlockSpec((N_DROW + 1, NAT_PAIRS, GRID_W, LANES), lambda b, hh, r: (0, hh, 0, 0))],
        out_specs=q_spec,
        out_shape=jax.ShapeDtypeStruct((N_LAT, D_MODEL), BF16),
        compiler_params=_params("arbitrary", "arbitrary", "arbitrary"),
        name="natten",
    )(q, k, vt, kc, vct, tbl)


def _bias_table_kernel(rpb_ref, o_ref):
    dr = pl.program_id(0)
    row = lax.broadcasted_iota(jnp.int32, (GRID_W, LANES), 0)
    col = lax.broadcasted_iota(jnp.int32, (GRID_W, LANES), 1)
    sub = col >= GRID_W
    qc = col % GRID_W
    kc = row
    dc = jnp.clip(kc - qc, -(WIN_W - 1), WIN_W - 1) + (WIN_W - 1)
    col_start = jnp.clip(qc - WIN_W // 2, 0, GRID_W - WIN_W)
    in_win = (kc >= col_start) & (kc < col_start + WIN_W) & (dr < N_DROW)
    n_dc = 2 * WIN_W - 1
    drc = jnp.minimum(dr, N_DROW - 1)
    for pr in range(N_HEADS_B // 2):
        b0 = ((2 * pr) * N_DROW + drc) * n_dc
        b1 = ((2 * pr + 1) * N_DROW + drc) * n_dc

        def body(j, acc, b0=b0, b1=b1):
            return jnp.where(dc == j, jnp.where(sub, rpb_ref[b1 + j], rpb_ref[b0 + j]), acc)

        acc = lax.fori_loop(0, n_dc, body, jnp.zeros((GRID_W, LANES), F32))
        o_ref[pr] = jnp.where(in_win, acc * LOG2_E, NEG_INF)


def _bias_table(rpb):
    return pl.pallas_call(
        _bias_table_kernel,
        grid=(N_DROW + 1,),
        in_specs=[pl.BlockSpec(memory_space=pltpu.SMEM)],
        out_specs=pl.BlockSpec((None, N_HEADS_B // 2, GRID_W, LANES), lambda d: (d, 0, 0, 0)),
        out_shape=jax.ShapeDtypeStruct((N_DROW + 1, N_HEADS_B // 2, GRID_W, LANES), F32),
        compiler_params=_params("arbitrary"),
        name="natten_bias_table",
    )(rpb.reshape(-1))


def _attn_residual(a_ref, wo_ref, x_ref, gate_ref, g_ref, b_ref):
    o = jnp.dot(a_ref[...], wo_ref[...], preferred_element_type=F32)
    return _layer_norm(ALPHA * x_ref[...] + gate_ref[...] * o, g_ref[...], b_ref[...])


def _attn_residual_specs(stream, tm):
    vec = _full_spec((1, D_MODEL))
    return [_row_spec(tm, D_MODEL), _full_spec((D_MODEL, D_MODEL)), _row_spec(tm, D_MODEL),
            _mod_spec(stream, tm), vec, vec]


def _wo_ln_kernel(a_ref, wo_ref, x_ref, gate_ref, g_ref, b_ref, o_ref):
    o_ref[...] = _attn_residual(a_ref, wo_ref, x_ref, gate_ref, g_ref, b_ref)


def _wo_ln(stream, attn, w_o, x, gate, ln_g, ln_b):
    return pl.pallas_call(
        _wo_ln_kernel,
        grid=(stream.n // TM,),
        in_specs=_attn_residual_specs(stream, TM),
        out_specs=_row_spec(TM, D_MODEL),
        out_shape=jax.ShapeDtypeStruct((stream.n, D_MODEL), F32),
        compiler_params=_params("arbitrary"),
        name="wo_postnorm",
    )(attn, w_o, x, gate, ln_g, ln_b)


def _swiglu(h, wg_ref, wu_ref, wd_ref):
    acc = None
    for lo, hi in FF_CHUNKS:
        g = jnp.dot(h, wg_ref[:, lo:hi], preferred_element_type=F32)
        u = jnp.dot(h, wu_ref[:, lo:hi], preferred_element_type=F32)
        a = (g * jax.nn.sigmoid(g) * u).astype(BF16)
        part = jnp.dot(a, wd_ref[lo:hi, :], preferred_element_type=F32)
        acc = part if acc is None else acc + part
    return acc


def _layer_dense_kernel(a_ref, wo_ref, x_ref, gate1_ref, g1_ref, b1_ref,
                        sh_ref, sc_ref, wg_ref, wu_ref, wd_ref, gate_ref, g_ref, b_ref, o_ref):
    x = _attn_residual(a_ref, wo_ref, x_ref, gate1_ref, g1_ref, b1_ref)
    h = (x * (1.0 + sc_ref[...]) + sh_ref[...]).astype(BF16)
    y = ALPHA * x + gate_ref[...] * _swiglu(h, wg_ref, wu_ref, wd_ref)
    o_ref[...] = _layer_norm(y, g_ref[...], b_ref[...])


def _layer_dense(stream, attn, w_o, x, gate1, ln1_g, ln1_b, shift, scale, w_gu, w_down, gate, ln_g, ln_b):
    vec = _full_spec((1, D_MODEL))
    mod = _mod_spec(stream, TM_FFN)
    return pl.pallas_call(
        _layer_dense_kernel,
        grid=(stream.n // TM_FFN,),
        in_specs=_attn_residual_specs(stream, TM_FFN) + [
            mod, mod,
            pl.BlockSpec((D_MODEL, D_FF), lambda i: (0, 0)),
            pl.BlockSpec((D_MODEL, D_FF), lambda i: (0, 1)),
            _full_spec((D_FF, D_MODEL)),
            mod, vec, vec],
        out_specs=_row_spec(TM_FFN, D_MODEL),
        out_shape=jax.ShapeDtypeStruct((stream.n, D_MODEL), F32),
        compiler_params=_params("arbitrary"),
        name="wo_ffn_dense",
    )(attn, w_o, x, gate1, ln1_g, ln1_b, shift, scale, w_gu, w_gu, w_down, gate, ln_g, ln_b)


def _ffn_grouped_kernel(te_ref, tv_ref, x_ref, wg_ref, wu_ref, wd_ref, o_ref):
    i = pl.program_id(0)

    @pl.when(tv_ref[i] != 0)
    def _():
        x = _from_token_tiles(x_ref, TMG).astype(BF16)
        _to_token_tiles(o_ref, _swiglu(x, wg_ref, wu_ref, wd_ref))

    @pl.when(tv_ref[i] == 0)
    def _():
        o_ref[...] = jnp.zeros_like(o_ref)


def _ffn_grouped(tile_expert, tile_valid, xs, w_gu, w_down):
    row = pl.BlockSpec((TMG * TOKEN_TILE_ROWS, LANES), lambda i, te, tv: (i, 0))
    grid_spec = pltpu.PrefetchScalarGridSpec(
        num_scalar_prefetch=2,
        grid=(NT_MOE,),
        in_specs=[row,
                  pl.BlockSpec((None, D_MODEL, D_FF), lambda i, te, tv: (te[i], 0, 0)),
                  pl.BlockSpec((None, D_MODEL, D_FF), lambda i, te, tv: (te[i], 0, 1)),
                  pl.BlockSpec((None, D_FF, D_MODEL), lambda i, te, tv: (te[i], 0, 0))],
        out_specs=row,
    )
    return pl.pallas_call(
        _ffn_grouped_kernel,
        grid_spec=grid_spec,
        out_shape=jax.ShapeDtypeStruct((NT_MOE * TMG * TOKEN_TILE_ROWS, LANES), F32),
        compiler_params=_params("arbitrary"),
        name="ffn_grouped",
    )(tile_expert, tile_valid, xs, w_gu, w_gu, w_down)


META_E0, META_E1, META_W0, META_W1, META_R0, META_R1 = range(6)


def _router_kernel(x_ref, sh_ref, sc_ref, rw_ref, cnt_in_ref, h_ref, meta_ref, cnt_ref, carry_scr):
    i = pl.program_id(0)

    @pl.when(i == 0)
    def _():
        carry_scr[...] = cnt_in_ref[...]

    h = x_ref[...] * (1.0 + sc_ref[...]) + sh_ref[...]
    _to_token_tiles(h_ref, h)
    w = rw_ref[...]
    h_hi = h.astype(BF16)
    w_hi = w.astype(BF16)
    h_lo = (h - h_hi.astype(F32)).astype(BF16)
    w_lo = (w - w_hi.astype(F32)).astype(BF16)
    logits = (jnp.dot(h_hi, w_hi, preferred_element_type=F32)
              + (jnp.dot(h_lo, w_hi, preferred_element_type=F32)
                 + jnp.dot(h_hi, w_lo, preferred_element_type=F32)))
    lane = lax.broadcasted_iota(jnp.int32, logits.shape, 1).astype(F32)
    lg = jnp.where(lane < N_EXPERTS, logits, -jnp.inf)
    m1 = jnp.max(lg, axis=-1, keepdims=True)
    i1 = jnp.min(jnp.where(lg == m1, lane, float(LANES)), axis=-1, keepdims=True)
    lg2 = jnp.where(lane == i1, -jnp.inf, lg)
    m2 = jnp.max(lg2, axis=-1, keepdims=True)
    i2 = jnp.min(jnp.where(lg2 == m2, lane, float(LANES)), axis=-1, keepdims=True)
    e = jnp.exp(m2 - m1)
    w1 = 1.0 / (1.0 + e)
    w2 = e / (1.0 + e)

    sel1 = lane == i1
    sel2 = lane == i2
    onehot = jnp.where(sel1 | sel2, 1.0, 0.0)
    rr = lax.broadcasted_iota(jnp.int32, (TM, TM), 0)
    cc = lax.broadcasted_iota(jnp.int32, (TM, TM), 1)
    lower = jnp.where(cc < rr, 1.0, 0.0).astype(BF16)
    before = jnp.dot(lower, onehot.astype(BF16), preferred_element_type=F32) + carry_scr[0:1, :]
    r1 = jnp.sum(jnp.where(sel1, before, 0.0), axis=-1, keepdims=True)
    r2 = jnp.sum(jnp.where(sel2, before, 0.0), axis=-1, keepdims=True)
    carry_scr[...] = carry_scr[...] + jnp.sum(onehot, axis=0, keepdims=True)
    cnt_ref[...] = carry_scr[...]

    cols = [i1, i2, w1, w2, r1, r2]
    meta = jnp.zeros(logits.shape, F32)
    for c, val in enumerate(cols):
        meta = jnp.where(lane == c, val, meta)
    meta_ref[...] = meta


def _router(stream, x, shift, scale, rw_pad, cnt_in):
    cnt_spec = _full_spec((SUBLANES, LANES))
    return pl.pallas_call(
        _router_kernel,
        grid=(stream.n // TM,),
        in_specs=[_row_spec(TM, D_MODEL), _mod_spec(stream, TM), _mod_spec(stream, TM),
                  _full_spec((D_MODEL, LANES)), cnt_spec],
        out_specs=[_row_spec(TM * TOKEN_TILE_ROWS, LANES), _row_spec(TM, LANES), cnt_spec],
        out_shape=[jax.ShapeDtypeStruct((stream.n * TOKEN_TILE_ROWS, LANES), F32),
                   jax.ShapeDtypeStruct((stream.n, LANES), F32),
                   jax.ShapeDtypeStruct((SUBLANES, LANES), F32)],
        scratch_shapes=[pltpu.VMEM((SUBLANES, LANES), F32)],
        compiler_params=_params("arbitrary"),
        name="moe_router",
    )(x, shift, scale, rw_pad, cnt_in)


def _tile_rows(token, count=1):
    start = token * TOKEN_TILE_ROWS
    if not isinstance(start, int):
        start = pl.multiple_of(start, TOKEN_TILE_ROWS)
    return pl.ds(start, count * TOKEN_TILE_ROWS)


def _to_token_tiles(ref, x):
    for j in range(TOKEN_TILE_ROWS):
        ref[pl.ds(j, x.shape[0], stride=TOKEN_TILE_ROWS), :] = x[:, j * LANES:(j + 1) * LANES]


def _from_token_tiles(ref, tokens):
    return jnp.concatenate([ref[pl.ds(j, tokens, stride=TOKEN_TILE_ROWS), :] for j in range(TOKEN_TILE_ROWS)], axis=1)


def _row_copy(src_ref, src_row, dst_ref, dst_row, sem):
    return pltpu.make_async_copy(src_ref.at[_tile_rows(src_row)], dst_ref.at[_tile_rows(dst_row)], sem)


def _scatter_rows(d0_ref, d1_ref, h_ref, buf_ref, sem):
    base = pl.program_id(0) * TM
    for r in range(TM):
        _row_copy(h_ref, r, buf_ref, d0_ref[base + r], sem).start(priority=0)
        _row_copy(h_ref, r, buf_ref, d1_ref[base + r], sem).start(priority=1)


def _scatter_wait(h_ref, buf_ref, sem, n_rows):
    pltpu.make_async_copy(h_ref.at[_tile_rows(0, n_rows)], buf_ref.at[_tile_rows(0, n_rows)], sem).wait()


def _scatter_kernel(d0_ref, d1_ref, pad_ref, hc_ref, hl_ref, buf_ref, zero_scr, sem, *, ctx_steps, pads_per_step):
    i = pl.program_id(0)

    @pl.when(i < ctx_steps)
    def _():
        _scatter_rows(d0_ref, d1_ref, hc_ref, buf_ref, sem)
        zero_scr[...] = jnp.zeros_like(zero_scr)
        pbase = i * pads_per_step
        for r in range(pads_per_step):
            _row_copy(zero_scr, 0, buf_ref, pad_ref[pbase + r], sem).start(priority=r % 2)
        _scatter_wait(hc_ref, buf_ref, sem, TM)
        _scatter_wait(hc_ref, buf_ref, sem, TM)
        _scatter_wait(hc_ref, buf_ref, sem, pads_per_step)

    @pl.when(i >= ctx_steps)
    def _():
        _scatter_rows(d0_ref, d1_ref, hl_ref, buf_ref, sem)
        _scatter_wait(hl_ref, buf_ref, sem, TM)
        _scatter_wait(hl_ref, buf_ref, sem, TM)


def _scatter(d0, d1, pad_rows, h_ctx, h_lat):
    ctx_steps = CTX.n // TM
    steps = ctx_steps + LAT.n // TM
    assert N_PAD_ROWS % ctx_steps == 0
    grid_spec = pltpu.PrefetchScalarGridSpec(
        num_scalar_prefetch=3,
        grid=(steps,),
        in_specs=[pl.BlockSpec((TM * TOKEN_TILE_ROWS, LANES), lambda i, a, b, c: (jnp.minimum(i, ctx_steps - 1), 0)),
                  pl.BlockSpec((TM * TOKEN_TILE_ROWS, LANES), lambda i, a, b, c: (jnp.maximum(i - ctx_steps, 0), 0))],
        out_specs=pl.BlockSpec(memory_space=pl.ANY),
        scratch_shapes=[pltpu.VMEM((TOKEN_TILE_ROWS, LANES), F32), pltpu.SemaphoreType.DMA],
    )
    return pl.pallas_call(
        functools.partial(_scatter_kernel, ctx_steps=ctx_steps, pads_per_step=N_PAD_ROWS // ctx_steps),
        grid_spec=grid_spec,
        out_shape=jax.ShapeDtypeStruct((NT_MOE * TMG * TOKEN_TILE_ROWS, LANES), F32),
        compiler_params=_params("arbitrary"),
        name="moe_scatter",
    )(d0, d1, pad_rows, h_ctx, h_lat)


def _gather_start(d0_ref, d1_ref, ys_ref, rows_scr, sems, tile, slot):
    base = tile * TM
    for r in range(TM):
        _row_copy(ys_ref, d0_ref[base + r], rows_scr.at[slot, 0], r, sems.at[slot]).start(priority=0)
        _row_copy(ys_ref, d1_ref[base + r], rows_scr.at[slot, 1], r, sems.at[slot]).start(priority=1)


def _gather_wait(ys_ref, rows_scr, sems, slot):
    for s in range(2):
        pltpu.make_async_copy(ys_ref.at[_tile_rows(0, TM)], rows_scr.at[slot, s], sems.at[slot]).wait()


def _combine_kernel(d0_ref, d1_ref, ys_ref, meta_ref, x_ref, gate_ref, g_ref, b_ref, o_ref, rows_scr, sems,
                    *, n_tiles):
    i = pl.program_id(0)
    slot = i % 2

    @pl.when(i == 0)
    def _():
        _gather_start(d0_ref, d1_ref, ys_ref, rows_scr, sems, 0, 0)

    _gather_start(d0_ref, d1_ref, ys_ref, rows_scr, sems, jnp.minimum(i + 1, n_tiles - 1), 1 - slot)
    _gather_wait(ys_ref, rows_scr, sems, slot)

    meta = meta_ref[...]
    w0 = meta[:, META_W0:META_W0 + 1]
    w1 = meta[:, META_W1:META_W1 + 1]
    f = w0 * _from_token_tiles(rows_scr.at[slot, 0], TM) + w1 * _from_token_tiles(rows_scr.at[slot, 1], TM)
    y = ALPHA * x_ref[...] + gate_ref[...] * f
    o_ref[...] = _layer_norm(y, g_ref[...], b_ref[...])

    @pl.when(i == n_tiles - 1)
    def _():
        _gather_wait(ys_ref, rows_scr, sems, 1 - slot)


def _combine(stream, d0, d1, ys, meta, x, gate, ln_g, ln_b):
    vec = pl.BlockSpec((1, D_MODEL), lambda i, a, b: (0, 0))
    grid_spec = pltpu.PrefetchScalarGridSpec(
        num_scalar_prefetch=2,
        grid=(stream.n // TM,),
        in_specs=[pl.BlockSpec(memory_space=pl.ANY), _row_spec(TM, LANES), _row_spec(TM, D_MODEL),
                  _mod_spec(stream, TM), vec, vec],
        out_specs=_row_spec(TM, D_MODEL),
        scratch_shapes=[pltpu.VMEM((2, 2, TM * TOKEN_TILE_ROWS, LANES), F32), pltpu.SemaphoreType.DMA((2,))],
    )
    return pl.pallas_call(
        functools.partial(_combine_kernel, n_tiles=stream.n // TM),
        grid_spec=grid_spec,
        out_shape=jax.ShapeDtypeStruct((stream.n, D_MODEL), F32),
        compiler_params=_params("arbitrary"),
        name="moe_combine",
    )(d0, d1, ys, meta, x, gate, ln_g, ln_b)


def _rope_tables():
    t = jnp.arange(DEC_SEQ)
    row = (t // GRID_W).astype(F32)
    col = (t % GRID_W).astype(F32)
    freqs = ROPE_THETA ** (-jnp.arange(ROT_FREQS, dtype=F32) / ROT_FREQS)
    ar = row[:, None] * freqs
    ac = col[:, None] * freqs
    cos = jnp.concatenate([jnp.cos(ar), jnp.cos(ar), jnp.cos(ac), jnp.cos(ac)], axis=1)
    sin = jnp.concatenate([-jnp.sin(ar), jnp.sin(ar), -jnp.sin(ac), jnp.sin(ac)], axis=1)
    return cos, sin


def _routing_plan(metas, counts):
    cnt = counts[0, :N_EXPERTS].astype(jnp.int32)
    tiles_e = (cnt + TMG - 1) // TMG
    tile_end = jnp.cumsum(tiles_e)
    tile_start = tile_end - tiles_e
    offs = tile_start * TMG
    dests = []
    for meta in metas:
        pair = []
        for ecol, rcol in ((META_E0, META_R0), (META_E1, META_R1)):
            e = meta[:, ecol].astype(jnp.int32)
            d = meta[:, rcol].astype(jnp.int32)
            for k in range(N_EXPERTS):
                d = d + jnp.where(e == k, offs[k], 0)
            pair.append(d)
        dests.append(tuple(pair))
    tid = jnp.arange(NT_MOE, dtype=jnp.int32)
    te = jnp.minimum(jnp.sum((tid[:, None] >= tile_end[None, :]).astype(jnp.int32), axis=1), N_EXPERTS - 1)
    total = tile_end[-1]
    valid = tid < total
    te_last = jnp.max(jnp.where(valid, te, 0))
    te = jnp.where(valid, te, te_last)
    pad_cnt = tiles_e * TMG - cnt
    pad_end = jnp.cumsum(pad_cnt)
    pad_start = pad_end - pad_cnt
    k = jnp.arange(N_PAD_ROWS, dtype=jnp.int32)
    grp = jnp.sum((k[:, None] >= pad_end[None, :]).astype(jnp.int32), axis=1)
    pad_rows = total * TMG + (k - pad_end[-1])
    for e in range(N_EXPERTS):
        pad_rows = jnp.where(grp == e, offs[e] + cnt[e] + (k - pad_start[e]), pad_rows)
    return dests, te, valid.astype(jnp.int32), pad_rows


def kernel(x_prompt, x_sample, cache_k_a, cache_v_a, cache_k_b, cache_v_b, c, c_ctx, ada_w, ada_b, ln_attn_g, ln_attn_b, ln_ffn_g, ln_ffn_b, wqkv_a, qnorm_a, knorm_a, wo_a, wqkv_b, rpb_b, wo_b, ffn_w_gu, ffn_w_down, router_w, moe_w_gu, moe_w_down):
    streams = (CTX, LAT)
    xs = [x_prompt.reshape(N_CTX, D_MODEL), x_sample.reshape(N_LAT, D_MODEL)]

    cond = jnp.zeros((GROUP_PAD, D_MODEL), F32).at[0].set(c_ctx).at[1:N_GROUPS].set(c)
    mods = _ada_mods(cond, ada_w, ada_b)
    vec = lambda a, l: a[l].reshape(1, D_MODEL)

    m = mods[0]
    w_qkv = wqkv_a[0].astype(BF16)
    w_o = wo_a[0].astype(BF16)
    w_gu = ffn_w_gu[0].astype(BF16)
    w_dn = ffn_w_down[0].astype(BF16)
    gains = (qnorm_a[0].reshape(1, HEAD_DIM_A), knorm_a[0].reshape(1, HEAD_DIM_A))
    nq, nk = N_HEADS_A * HEAD_DIM_A, N_KV_A * HEAD_DIM_A
    qscale = HEAD_DIM_A ** -0.5

    qp, kp, vp, kp32, vp32 = _qkv(CTX, xs[0], m[0], m[1], w_qkv, nq=nq, nk=nk, qscale=qscale, gains=gains,
                                  emit_f32=True)
    new_k_a = kp32.reshape(BATCH, 1, SEQ, N_KV_A, HEAD_DIM_A)
    new_v_a = vp32.reshape(BATCH, 1, SEQ, N_KV_A, HEAD_DIM_A)
    ql, kl, vlt = _qkv(LAT, xs[1], m[0], m[1], w_qkv, nq=nq, nk=nk, qscale=qscale * LOG2_E, gains=gains,
                       rope_tables=_rope_tables(), v_layout="t")
    cache_k = cache_k_a[:, 0].reshape(DEC_BATCH, PAST_LEN, nk).astype(BF16)
    cache_vt = jnp.swapaxes(cache_v_a[:, 0].reshape(DEC_BATCH, PAST_LEN, nk), 1, 2).astype(BF16)
    attn = [_gqa_ctx(qp, kp, vp), _gqa_lat_t(ql, kl, vlt, cache_k, cache_vt)]
    xs = [_layer_dense(s, a, w_o, x, m[2], vec(ln_attn_g, 0), vec(ln_attn_b, 0),
                       m[3], m[4], w_gu, w_dn, m[5], vec(ln_ffn_g, 0), vec(ln_ffn_b, 0))
          for s, a, x in zip(streams, attn, xs)]

    m = mods[1]
    w_qkv = wqkv_b[0].astype(BF16)
    w_o = wo_b[0].astype(BF16)
    qscale = HEAD_DIM_B ** -0.5
    qp, kp, vp, kp32, vp32 = _qkv(CTX, xs[0], m[0], m[1], w_qkv, nq=D_MODEL, nk=D_MODEL, qscale=qscale,
                                  emit_f32=True)
    new_k_b = kp32.reshape(BATCH, 1, SEQ, N_HEADS_B, HEAD_DIM_B)
    new_v_b = vp32.reshape(BATCH, 1, SEQ, N_HEADS_B, HEAD_DIM_B)
    ql, kl, vlt = _qkv(LAT, xs[1], m[0], m[1], w_qkv, nq=D_MODEL, nk=D_MODEL, qscale=qscale * LOG2_E,
                       v_layout="t_blocked")
    cache_k = cache_k_b[:, 0].reshape(DEC_BATCH, PAST_LEN, D_MODEL).astype(BF16)
    cache_vt = jnp.swapaxes(cache_v_b[:, 0].reshape(DEC_BATCH, PAST_LEN, D_MODEL), 1, 2).astype(BF16)
    attn = [_mha_ctx(qp, kp, vp), _natten(ql, kl, vlt, cache_k, cache_vt, _bias_table(rpb_b[0]))]
    xs = [_wo_ln(s, a, w_o, x, m[2], vec(ln_attn_g, 1), vec(ln_attn_b, 1)) for s, a, x in zip(streams, attn, xs)]

    rw_pad = jnp.zeros((D_MODEL, LANES), F32).at[:, :N_EXPERTS].set(router_w[0])
    counts = jnp.zeros((SUBLANES, LANES), F32)
    hs, metas = [], []
    for s, x in zip(streams, xs):
        h, meta, counts = _router(s, x, m[3], m[4], rw_pad, counts)
        hs.append(h)
        metas.append(meta)
    dests, tile_expert, tile_valid, pad_rows = _routing_plan(metas, counts)
    d0_all = jnp.concatenate([dests[0][0], dests[1][0]])
    d1_all = jnp.concatenate([dests[0][1], dests[1][1]])
    sorted_rows = _scatter(d0_all, d1_all, pad_rows, hs[0], hs[1])
    ys = _ffn_grouped(tile_expert, tile_valid, sorted_rows, moe_w_gu[0].astype(BF16), moe_w_down[0].astype(BF16))
    outs = [_combine(s, d0, d1, ys, meta, x, m[5], vec(ln_ffn_g, 1), vec(ln_ffn_b, 1))
            for s, (d0, d1), meta, x in zip(streams, dests, metas, xs)]

    y_prompt = outs[0].reshape(BATCH, SEQ, D_MODEL)
    y_sample = outs[1].reshape(DEC_BATCH, DEC_SEQ, D_MODEL)
    return (y_prompt, y_sample, new_k_a, new_v_a, new_k_b, new_v_b)
```

```python
import functools
import math

import jax
import jax.numpy as jnp
from jax import lax
from jax.experimental import pallas as pl
from jax.experimental.pallas import tpu as pltpu

F32 = jnp.float32
BF16 = jnp.bfloat16
HIGHEST = lax.Precision.HIGHEST

D_MODEL = 1024
BATCH, SEQ = 32, 256
DEC_BATCH, DEC_SEQ = 4, 4096
PAST_LEN = 256
DEPTH = 2
GRID_W = 64
GRID_H = DEC_SEQ // GRID_W
N_HEADS_A, N_KV_A, HEAD_DIM_A = 8, 2, 128
ROT_FREQS = HEAD_DIM_A // 4
ROPE_THETA = 10000.0
N_HEADS_B, HEAD_DIM_B = 16, 64
WIN_H, WIN_W = 8, 16
D_FF = 2816
N_EXPERTS = 8
EPS = 1e-6
NEG_INF = -1e30
ALPHA = (2.0 * DEPTH) ** 0.25
LOG2_E = math.log2(math.e)

N_CTX = BATCH * SEQ
N_LAT = DEC_BATCH * DEC_SEQ
N_GROUPS = 1 + DEC_BATCH
GROUP_PAD = 8

LANES = 128
SUBLANES = 8
VMEM_LIMIT = 56 * 2**20

TM = 512
MXU_DIM = 256
FF_CHUNK = 4 * MXU_DIM
FF_CHUNKS = tuple((lo, min(lo + FF_CHUNK, D_FF)) for lo in range(0, D_FF, FF_CHUNK))
assert D_FF % MXU_DIM == 0
TM_FFN = 512
TMG = 512
TQ_A = 512
UNIT_A = 2
TN_ADA = 1536
N_PAIRS = 2 * (N_CTX + N_LAT)
NT_MOE = N_PAIRS // TMG + N_EXPERTS
N_PAD_ROWS = NT_MOE * TMG - N_PAIRS
TOKEN_TILE_ROWS = D_MODEL // LANES
assert TOKEN_TILE_ROWS == SUBLANES


class Stream:
    def __init__(self, n_rows, group_offset, rows_per_group):
        self.n = n_rows
        self.goff = group_offset
        self.rpg = rows_per_group

    def group_map(self, tm):
        tiles_per_group = self.rpg // tm
        goff = self.goff
        return lambda i, *_: (goff + i // tiles_per_group, 0, 0)


CTX = Stream(N_CTX, 0, N_CTX)
LAT = Stream(N_LAT, 1, DEC_SEQ)


def _params(*sem):
    return pltpu.CompilerParams(dimension_semantics=sem, vmem_limit_bytes=VMEM_LIMIT)


def _mod_spec(stream, tm):
    return pl.BlockSpec((None, 1, D_MODEL), stream.group_map(tm))


def _row_spec(tm, width):
    return pl.BlockSpec((tm, width), lambda i, *_: (i, 0))


def _full_spec(shape):
    nd = len(shape)
    return pl.BlockSpec(shape, lambda *_: (0,) * nd)


def _layer_norm(y, g, b):
    mu = jnp.mean(y, axis=-1, keepdims=True)
    d = y - mu
    var = jnp.mean(d * d, axis=-1, keepdims=True)
    return d * lax.rsqrt(var + EPS) * g + b


def _ada_kernel(c_ref, w_ref, b_ref, o_ref):
    c = c_ref[...]
    s = c * jax.nn.sigmoid(c)
    o_ref[...] = jnp.dot(s, w_ref[...], preferred_element_type=F32, precision=HIGHEST) + b_ref[...]


def _ada_mods(cond, ada_w, ada_b):
    n_out = 6 * D_MODEL
    out = pl.pallas_call(
        _ada_kernel,
        grid=(DEPTH, n_out // TN_ADA),
        in_specs=[
            pl.BlockSpec((GROUP_PAD, D_MODEL), lambda l, n: (0, 0)),
            pl.BlockSpec((None, D_MODEL, TN_ADA), lambda l, n: (l, 0, n)),
            pl.BlockSpec((None, 1, TN_ADA), lambda l, n: (l, 0, n)),
        ],
        out_specs=pl.BlockSpec((None, GROUP_PAD, TN_ADA), lambda l, n: (l, 0, n)),
        out_shape=jax.ShapeDtypeStruct((DEPTH, GROUP_PAD, n_out), F32),
        compiler_params=_params("arbitrary", "arbitrary"),
        name="ada_mods",
    )(cond, ada_w, ada_b.reshape(DEPTH, 1, n_out))
    out = out.reshape(DEPTH, GROUP_PAD, 6, D_MODEL).transpose(0, 2, 1, 3)
    return out[:, :, :, None, :]


def _swap_halves(t):
    lane = lax.broadcasted_iota(jnp.int32, t.shape, 1)
    fwd = pltpu.roll(t, LANES - ROT_FREQS, 1)
    bwd = pltpu.roll(t, ROT_FREQS, 1)
    return jnp.where((lane % (2 * ROT_FREQS)) < ROT_FREQS, fwd, bwd)


def _qkv_kernel(*refs, nq, nk, norm, rope, emit_f32, v_layout, qscale):
    refs = list(refs)
    x_ref, sh_ref, sc_ref, w_ref = refs[:4]
    pos = 4
    if norm:
        qg_ref, kg_ref = refs[pos:pos + 2]
        pos += 2
    if rope:
        cos_ref, sin_ref = refs[pos:pos + 2]
        pos += 2
    q_ref, k_ref, v_ref = refs[pos:pos + 3]
    pos += 3
    if emit_f32:
        kf_ref, vf_ref = refs[pos:pos + 2]

    h = (x_ref[...] * (1.0 + sc_ref[...]) + sh_ref[...]).astype(BF16)
    qkv = jnp.dot(h, w_ref[...], preferred_element_type=F32)
    if norm:
        n_heads = (nq + nk) // HEAD_DIM_A
        for hd in range(n_heads):
            lo = hd * HEAD_DIM_A
            t = qkv[:, lo:lo + HEAD_DIM_A]
            ms = jnp.mean(t * t, axis=-1, keepdims=True)
            gain = qg_ref[...] if lo < nq else kg_ref[...]
            t = t * lax.rsqrt(ms + EPS) * gain
            if rope:
                t = t * cos_ref[...] + _swap_halves(t) * sin_ref[...]
            if lo < nq:
                q_ref[:, lo:lo + HEAD_DIM_A] = (t * qscale).astype(BF16)
            else:
                k_ref[:, lo - nq:lo - nq + HEAD_DIM_A] = t.astype(BF16)
                if emit_f32:
                    kf_ref[:, lo - nq:lo - nq + HEAD_DIM_A] = t
    else:
        q_ref[...] = (qkv[:, :nq] * qscale).astype(BF16)
        k = qkv[:, nq:nq + nk]
        k_ref[...] = k.astype(BF16)
        if emit_f32:
            kf_ref[...] = k
    v = qkv[:, nq + nk:]
    if v_layout == "rows":
        v_ref[...] = v.astype(BF16)
    elif v_layout == "t":
        v_ref[...] = v.T.astype(BF16)
    else:
        vt = v.T.astype(BF16)
        for j in range(vt.shape[1] // LANES):
            v_ref[j] = vt[:, j * LANES:(j + 1) * LANES]
    if emit_f32:
        vf_ref[...] = v


def _qkv(stream, x, shift, scale, w, *, nq, nk, qscale, gains=None, rope_tables=None, emit_f32=False,
         v_layout="rows"):
    n = stream.n
    nw = w.shape[1]
    norm = gains is not None
    rope = rope_tables is not None
    in_specs = [_row_spec(TM, D_MODEL), _mod_spec(stream, TM), _mod_spec(stream, TM), _full_spec((D_MODEL, nw))]
    args = [x, shift, scale, w]
    if norm:
        in_specs += [_full_spec((1, HEAD_DIM_A))] * 2
        args += list(gains)
    if rope:
        tiles_per_seq = DEC_SEQ // TM
        tbl_spec = pl.BlockSpec((TM, HEAD_DIM_A), lambda i: (i % tiles_per_seq, 0))
        in_specs += [tbl_spec, tbl_spec]
        args += list(rope_tables)
    out_specs = [_row_spec(TM, nq), _row_spec(TM, nk), _row_spec(TM, nk)]
    out_shape = [jax.ShapeDtypeStruct((n, nq), BF16), jax.ShapeDtypeStruct((n, nk), BF16),
                 jax.ShapeDtypeStruct((n, nk), BF16)]
    if v_layout == "t":
        out_specs[2] = pl.BlockSpec((nk, TM), lambda i: (0, i))
        out_shape[2] = jax.ShapeDtypeStruct((nk, n), BF16)
    elif v_layout == "t_blocked":
        out_specs[2] = pl.BlockSpec((TM // LANES, nk, LANES), lambda i: (i, 0, 0))
        out_shape[2] = jax.ShapeDtypeStruct((n // LANES, nk, LANES), BF16)
    if emit_f32:
        out_specs += [_row_spec(TM, nk), _row_spec(TM, nk)]
        out_shape += [jax.ShapeDtypeStruct((n, nk), F32)] * 2
    return pl.pallas_call(
        functools.partial(_qkv_kernel, nq=nq, nk=nk, norm=norm, rope=rope, emit_f32=emit_f32,
                          v_layout=v_layout, qscale=qscale),
        grid=(n // TM,),
        in_specs=in_specs,
        out_specs=out_specs,
        out_shape=out_shape,
        compiler_params=_params("arbitrary"),
        name="qkv_norm_rope" if norm else "qkv",
    )(*args)


def _attend(q, kvs, biases=None):
    scores = []
    for idx, (k, _) in enumerate(kvs):
        s = lax.dot_general(q, k, (((1,), (1,)), ((), ())), preferred_element_type=F32)
        if biases is not None and biases[idx] is not None:
            s = s + biases[idx]
        scores.append(s)
    m = jnp.max(scores[0], axis=-1, keepdims=True)
    for s in scores[1:]:
        m = jnp.maximum(m, jnp.max(s, axis=-1, keepdims=True))
    denom = None
    out = None
    for s, (_, v) in zip(scores, kvs):
        p = jnp.exp(s - m)
        part = jnp.sum(p, axis=-1, keepdims=True)
        pv = jnp.dot(p.astype(BF16), v, preferred_element_type=F32)
        denom = part if denom is None else denom + part
        out = pv if out is None else out + pv
    return out * (1.0 / denom)


def _gqa_ctx_kernel(q_ref, k_ref, v_ref, o_ref):
    d = HEAD_DIM_A
    group = N_HEADS_A // N_KV_A
    for g in range(N_KV_A):
        heads = [g * group + j for j in range(group)]
        qs = jnp.concatenate([q_ref[:, h * d:(h + 1) * d] for h in heads], axis=0)
        o = _attend(qs, [(k_ref[:, g * d:(g + 1) * d], v_ref[:, g * d:(g + 1) * d])])
        for j, h in enumerate(heads):
            o_ref[:, h * d:(h + 1) * d] = o[j * SEQ:(j + 1) * SEQ].astype(o_ref.dtype)


def _gqa_ctx(q, k, v):
    nq, nk = N_HEADS_A * HEAD_DIM_A, N_KV_A * HEAD_DIM_A
    return pl.pallas_call(
        _gqa_ctx_kernel,
        grid=(BATCH,),
        in_specs=[_row_spec(SEQ, nq), _row_spec(SEQ, nk), _row_spec(SEQ, nk)],
        out_specs=_row_spec(SEQ, nq),
        out_shape=jax.ShapeDtypeStruct((N_CTX, nq), BF16),
        compiler_params=_params("arbitrary"),
        name="gqa_ctx",
    )(q, k, v)


ONES_ROWS = 16


def _scores_t(q, ks):
    return [lax.dot_general(k, q, (((1,), (1,)), ((), ())), preferred_element_type=F32) for k in ks]


def _softmax_pv_t(scores, vts):
    m = jnp.max(scores[0], axis=0, keepdims=True)
    for s in scores[1:]:
        m = jnp.maximum(m, jnp.max(s, axis=0, keepdims=True))
    acc = None
    for s, vt in zip(scores, vts):
        p = jnp.exp2(s - m).astype(BF16)
        vt_ones = jnp.concatenate([vt, jnp.ones((ONES_ROWS, vt.shape[1]), BF16)], axis=0)
        pv = jnp.dot(vt_ones, p, preferred_element_type=F32)
        acc = pv if acc is None else acc + pv
    d = LANES
    return (acc[:d] * (1.0 / acc[d:d + 1])).T


def _gqa_t_kernel(q_ref, k_ref, vt_ref, kc_ref, vct_ref, o_ref, *, n_kv, group, unit, tq):
    d = HEAD_DIM_A
    units = [(g, [g * group + u0 + j for j in range(unit)]) for g in range(n_kv) for u0 in range(0, group, unit)]
    def unit_scores(g, heads):
        ds = slice(g * d, (g + 1) * d)
        qs = jnp.concatenate([q_ref[:, h * d:(h + 1) * d] for h in heads], axis=0)
        return _scores_t(qs, [k_ref[:, ds], kc_ref[:, ds]])

    nxt = unit_scores(*units[0])
    for idx, (g, heads) in enumerate(units):
        sc = nxt
        if idx + 1 < len(units):
            nxt = unit_scores(*units[idx + 1])
        ds = slice(g * d, (g + 1) * d)
        o = _softmax_pv_t(sc, [vt_ref[ds, :], vct_ref[ds, :]])
        for j, h in enumerate(heads):
            o_ref[:, h * d:(h + 1) * d] = o[j * tq:(j + 1) * tq].astype(o_ref.dtype)


def _gqa_lat_t(q, k, vt, kc, vct):
    nq, nk = N_HEADS_A * HEAD_DIM_A, N_KV_A * HEAD_DIM_A
    tiles = DEC_SEQ // TQ_A
    q_spec = pl.BlockSpec((TQ_A, nq), lambda b, t: (b * tiles + t, 0))
    return pl.pallas_call(
        functools.partial(_gqa_t_kernel, n_kv=N_KV_A, group=N_HEADS_A // N_KV_A, unit=UNIT_A, tq=TQ_A),
        grid=(DEC_BATCH, tiles),
        in_specs=[q_spec,
                  pl.BlockSpec((DEC_SEQ, nk), lambda b, t: (b, 0)),
                  pl.BlockSpec((nk, DEC_SEQ), lambda b, t: (0, b)),
                  pl.BlockSpec((None, PAST_LEN, nk), lambda b, t: (b, 0, 0)),
                  pl.BlockSpec((None, nk, PAST_LEN), lambda b, t: (b, 0, 0))],
        out_specs=q_spec,
        out_shape=jax.ShapeDtypeStruct((N_LAT, nq), BF16),
        compiler_params=_params("arbitrary", "arbitrary"),
        name="gqa_lat",
    )(q, k, vt, kc, vct)


def _split_pair(qp):
    lo = lax.broadcasted_iota(jnp.int32, qp.shape, 1) < HEAD_DIM_B
    zero = jnp.zeros_like(qp)
    return jnp.concatenate([jnp.where(lo, qp, zero), jnp.where(lo, zero, qp)], axis=0)


def _merge_pair(o2, rows):
    lo = lax.broadcasted_iota(jnp.int32, (rows, LANES), 1) < HEAD_DIM_B
    return jnp.where(lo, o2[:rows], o2[rows:])


def _mha_ctx_kernel(q_ref, k_ref, v_ref, o_ref):
    for p in range(N_HEADS_B // 2):
        sl = slice(p * LANES, (p + 1) * LANES)
        o2 = _attend(_split_pair(q_ref[:, sl]), [(k_ref[:, sl], v_ref[:, sl])])
        o_ref[:, sl] = _merge_pair(o2, SEQ).astype(o_ref.dtype)


def _mha_ctx(q, k, v):
    return pl.pallas_call(
        _mha_ctx_kernel,
        grid=(BATCH,),
        in_specs=[_row_spec(SEQ, D_MODEL)] * 3,
        out_specs=_row_spec(SEQ, D_MODEL),
        out_shape=jax.ShapeDtypeStruct((N_CTX, D_MODEL), BF16),
        compiler_params=_params("arbitrary"),
        name="mha_ctx",
    )(q, k, v)


NAT_HALF = D_MODEL
NAT_PAIRS = NAT_HALF // LANES
NAT_R = 4
NAT_WROWS = 12
N_DROW = 2 * WIN_H - 1
MASKED_TILE = N_DROW


def _natten_kernel(q_ref, k_ref, vt_ref, kc_ref, vct_ref, tbl_ref, o_ref):
    r0 = pl.program_id(2) * NAT_R
    ws = jnp.clip(r0 - WIN_H // 2, 0, GRID_H - NAT_WROWS)
    row0 = pl.multiple_of(ws * GRID_W, LANES)
    blk0 = ws * GRID_W // LANES
    n_loc = NAT_WROWS * GRID_W

    def tile_index(a, w):
        r = r0 + a
        start = jnp.clip(r - WIN_H // 2, 0, GRID_H - WIN_H)
        wr = ws + w
        valid = (wr >= start) & (wr < start + WIN_H)
        return jnp.where(valid, wr - r + (WIN_H - 1), MASKED_TILE)

    idx = [[tile_index(a, w) for w in range(NAT_WROWS)] for a in range(NAT_R)]

    def pair_scores(p):
        sl = slice(p * LANES, (p + 1) * LANES)
        q2 = jnp.concatenate([_split_pair(q_ref[a * GRID_W:(a + 1) * GRID_W, sl]) for a in range(NAT_R)], axis=0)
        bias_t = jnp.concatenate(
            [jnp.concatenate([tbl_ref[idx[a][w], p] for w in range(NAT_WROWS)], axis=0) for a in range(NAT_R)],
            axis=1)
        s_loc, s_ctx = _scores_t(q2, [k_ref[pl.ds(row0, n_loc), sl], kc_ref[:, sl]])
        return [s_loc + bias_t, s_ctx]

    nxt = pair_scores(0)
    for p in range(NAT_PAIRS):
        sl = slice(p * LANES, (p + 1) * LANES)
        sc = nxt
        if p + 1 < NAT_PAIRS:
            nxt = pair_scores(p + 1)
        vt_win = jnp.concatenate([vt_ref[blk0 + j, sl, :] for j in range(n_loc // LANES)], axis=1)
        o2 = _softmax_pv_t(sc, [vt_win, vct_ref[sl, :]])
        for a in range(NAT_R):
            o_ref[a * GRID_W:(a + 1) * GRID_W, sl] = _merge_pair(
                o2[a * LANES:(a + 1) * LANES], GRID_W).astype(o_ref.dtype)


def _natten(q, k, vt, kc, vct, tbl):
    n_half = D_MODEL // NAT_HALF
    blocks = GRID_H // NAT_R
    seq_blocks = DEC_SEQ // LANES
    q_spec = pl.BlockSpec((NAT_R * GRID_W, NAT_HALF), lambda b, hh, r: (b * blocks + r, hh))
    return pl.pallas_call(
        _natten_kernel,
        grid=(DEC_BATCH, n_half, blocks),
        in_specs=[q_spec,
                  pl.BlockSpec((DEC_SEQ, NAT_HALF), lambda b, hh, r: (b, hh)),
                  pl.BlockSpec((seq_blocks, NAT_HALF, LANES), lambda b, hh, r: (b, hh, 0)),
                  pl.BlockSpec((None, PAST_LEN, NAT_HALF), lambda b, hh, r: (b, 0, hh)),
                  pl.BlockSpec((None, NAT_HALF, PAST_LEN), lambda b, hh, r: (b, hh, 0)),
                  pl.BlockSpec((N_DROW + 1, NAT_PAIRS, GRID_W, LANES), lambda b, hh, r: (0, hh, 0, 0))],
        out_specs=q_spec,
        out_shape=jax.ShapeDtypeStruct((N_LAT, D_MODEL), BF16),
        compiler_params=_params("arbitrary", "arbitrary", "arbitrary"),
        name="natten",
    )(q, k, vt, kc, vct, tbl)


def _bias_table_kernel(rpb_ref, o_ref):
    dr = pl.program_id(0)
    row = lax.broadcasted_iota(jnp.int32, (GRID_W, LANES), 0)
    col = lax.broadcasted_iota(jnp.int32, (GRID_W, LANES), 1)
    sub = col >= GRID_W
    qc = col % GRID_W
    kc = row
    dc = jnp.clip(kc - qc, -(WIN_W - 1), WIN_W - 1) + (WIN_W - 1)
    col_start = jnp.clip(qc - WIN_W // 2, 0, GRID_W - WIN_W)
    in_win = (kc >= col_start) & (kc < col_start + WIN_W) & (dr < N_DROW)
    n_dc = 2 * WIN_W - 1
    drc = jnp.minimum(dr, N_DROW - 1)
    for pr in range(N_HEADS_B // 2):
        b0 = ((2 * pr) * N_DROW + drc) * n_dc
        b1 = ((2 * pr + 1) * N_DROW + drc) * n_dc

        def body(j, acc, b0=b0, b1=b1):
            return jnp.where(dc == j, jnp.where(sub, rpb_ref[b1 + j], rpb_ref[b0 + j]), acc)

        acc = lax.fori_loop(0, n_dc, body, jnp.zeros((GRID_W, LANES), F32))
        o_ref[pr] = jnp.where(in_win, acc * LOG2_E, NEG_INF)


def _bias_table(rpb):
    return pl.pallas_call(
        _bias_table_kernel,
        grid=(N_DROW + 1,),
        in_specs=[pl.BlockSpec(memory_space=pltpu.SMEM)],
        out_specs=pl.BlockSpec((None, N_HEADS_B // 2, GRID_W, LANES), lambda d: (d, 0, 0, 0)),
        out_shape=jax.ShapeDtypeStruct((N_DROW + 1, N_HEADS_B // 2, GRID_W, LANES), F32),
        compiler_params=_params("arbitrary"),
        name="natten_bias_table",
    )(rpb.reshape(-1))


def _attn_residual(a_ref, wo_ref, x_ref, gate_ref, g_ref, b_ref):
    o = jnp.dot(a_ref[...], wo_ref[...], preferred_element_type=F32)
    return _layer_norm(ALPHA * x_ref[...] + gate_ref[...] * o, g_ref[...], b_ref[...])


def _attn_residual_specs(stream, tm):
    vec = _full_spec((1, D_MODEL))
    return [_row_spec(tm, D_MODEL), _full_spec((D_MODEL, D_MODEL)), _row_spec(tm, D_MODEL),
            _mod_spec(stream, tm), vec, vec]


def _wo_ln_kernel(a_ref, wo_ref, x_ref, gate_ref, g_ref, b_ref, o_ref):
    o_ref[...] = _attn_residual(a_ref, wo_ref, x_ref, gate_ref, g_ref, b_ref)


def _wo_ln(stream, attn, w_o, x, gate, ln_g, ln_b):
    return pl.pallas_call(
        _wo_ln_kernel,
        grid=(stream.n // TM,),
        in_specs=_attn_residual_specs(stream, TM),
        out_specs=_row_spec(TM, D_MODEL),
        out_shape=jax.ShapeDtypeStruct((stream.n, D_MODEL), F32),
        compiler_params=_params("arbitrary"),
        name="wo_postnorm",
    )(attn, w_o, x, gate, ln_g, ln_b)


def _swiglu(h, wg_ref, wu_ref, wd_ref):
    acc = None
    for lo, hi in FF_CHUNKS:
        g = jnp.dot(h, wg_ref[:, lo:hi], preferred_element_type=F32)
        u = jnp.dot(h, wu_ref[:, lo:hi], preferred_element_type=F32)
        a = (g * jax.nn.sigmoid(g) * u).astype(BF16)
        part = jnp.dot(a, wd_ref[lo:hi, :], preferred_element_type=F32)
        acc = part if acc is None else acc + part
    return acc


def _layer_dense_kernel(a_ref, wo_ref, x_ref, gate1_ref, g1_ref, b1_ref,
                        sh_ref, sc_ref, wg_ref, wu_ref, wd_ref, gate_ref, g_ref, b_ref, o_ref):
    x = _attn_residual(a_ref, wo_ref, x_ref, gate1_ref, g1_ref, b1_ref)
    h = (x * (1.0 + sc_ref[...]) + sh_ref[...]).astype(BF16)
    y = ALPHA * x + gate_ref[...] * _swiglu(h, wg_ref, wu_ref, wd_ref)
    o_ref[...] = _layer_norm(y, g_ref[...], b_ref[...])


def _layer_dense(stream, attn, w_o, x, gate1, ln1_g, ln1_b, shift, scale, w_gu, w_down, gate, ln_g, ln_b):
    vec = _full_spec((1, D_MODEL))
    mod = _mod_spec(stream, TM_FFN)
    return pl.pallas_call(
        _layer_dense_kernel,
        grid=(stream.n // TM_FFN,),
        in_specs=_attn_residual_specs(stream, TM_FFN) + [
            mod, mod,
            pl.BlockSpec((D_MODEL, D_FF), lambda i: (0, 0)),
            pl.BlockSpec((D_MODEL, D_FF), lambda i: (0, 1)),
            _full_spec((D_FF, D_MODEL)),
            mod, vec, vec],
        out_specs=_row_spec(TM_FFN, D_MODEL),
        out_shape=jax.ShapeDtypeStruct((stream.n, D_MODEL), F32),
        compiler_params=_params("arbitrary"),
        name="wo_ffn_dense",
    )(attn, w_o, x, gate1, ln1_g, ln1_b, shift, scale, w_gu, w_gu, w_down, gate, ln_g, ln_b)


def _ffn_grouped_kernel(te_ref, tv_ref, x_ref, wg_ref, wu_ref, wd_ref, o_ref):
    i = pl.program_id(0)

    @pl.when(tv_ref[i] != 0)
    def _():
        x = _from_token_tiles(x_ref, TMG).astype(BF16)
        _to_token_tiles(o_ref, _swiglu(x, wg_ref, wu_ref, wd_ref))

    @pl.when(tv_ref[i] == 0)
    def _():
        o_ref[...] = jnp.zeros_like(o_ref)


def _ffn_grouped(tile_expert, tile_valid, xs, w_gu, w_down):
    row = pl.BlockSpec((TMG * TOKEN_TILE_ROWS, LANES), lambda i, te, tv: (i, 0))
    grid_spec = pltpu.PrefetchScalarGridSpec(
        num_scalar_prefetch=2,
        grid=(NT_MOE,),
        in_specs=[row,
                  pl.BlockSpec((None, D_MODEL, D_FF), lambda i, te, tv: (te[i], 0, 0)),
                  pl.BlockSpec((None, D_MODEL, D_FF), lambda i, te, tv: (te[i], 0, 1)),
                  pl.BlockSpec((None, D_FF, D_MODEL), lambda i, te, tv: (te[i], 0, 0))],
        out_specs=row,
    )
    return pl.pallas_call(
        _ffn_grouped_kernel,
        grid_spec=grid_spec,
        out_shape=jax.ShapeDtypeStruct((NT_MOE * TMG * TOKEN_TILE_ROWS, LANES), F32),
        compiler_params=_params("arbitrary"),
        name="ffn_grouped",
    )(tile_expert, tile_valid, xs, w_gu, w_gu, w_down)


META_E0, META_E1, META_W0, META_W1, META_R0, META_R1 = range(6)


def _router_kernel(x_ref, sh_ref, sc_ref, rw_ref, cnt_in_ref, h_ref, meta_ref, cnt_ref, carry_scr):
    i = pl.program_id(0)

    @pl.when(i == 0)
    def _():
        carry_scr[...] = cnt_in_ref[...]

    h = x_ref[...] * (1.0 + sc_ref[...]) + sh_ref[...]
    _to_token_tiles(h_ref, h)
    w = rw_ref[...]
    h_hi = h.astype(BF16)
    w_hi = w.astype(BF16)
    h_lo = (h - h_hi.astype(F32)).astype(BF16)
    w_lo = (w - w_hi.astype(F32)).astype(BF16)
    logits = (jnp.dot(h_hi, w_hi, preferred_element_type=F32)
              + (jnp.dot(h_lo, w_hi, preferred_element_type=F32)
                 + jnp.dot(h_hi, w_lo, preferred_element_type=F32)))
    lane = lax.broadcasted_iota(jnp.int32, logits.shape, 1).astype(F32)
    lg = jnp.where(lane < N_EXPERTS, logits, -jnp.inf)
    m1 = jnp.max(lg, axis=-1, keepdims=True)
    i1 = jnp.min(jnp.where(lg == m1, lane, float(LANES)), axis=-1, keepdims=True)
    lg2 = jnp.where(lane == i1, -jnp.inf, lg)
    m2 = jnp.max(lg2, axis=-1, keepdims=True)
    i2 = jnp.min(jnp.where(lg2 == m2, lane, float(LANES)), axis=-1, keepdims=True)
    e = jnp.exp(m2 - m1)
    w1 = 1.0 / (1.0 + e)
    w2 = e / (1.0 + e)

    sel1 = lane == i1
    sel2 = lane == i2
    onehot = jnp.where(sel1 | sel2, 1.0, 0.0)
    rr = lax.broadcasted_iota(jnp.int32, (TM, TM), 0)
    cc = lax.broadcasted_iota(jnp.int32, (TM, TM), 1)
    lower = jnp.where(cc < rr, 1.0, 0.0).astype(BF16)
    before = jnp.dot(lower, onehot.astype(BF16), preferred_element_type=F32) + carry_scr[0:1, :]
    r1 = jnp.sum(jnp.where(sel1, before, 0.0), axis=-1, keepdims=True)
    r2 = jnp.sum(jnp.where(sel2, before, 0.0), axis=-1, keepdims=True)
    carry_scr[...] = carry_scr[...] + jnp.sum(onehot, axis=0, keepdims=True)
    cnt_ref[...] = carry_scr[...]

    cols = [i1, i2, w1, w2, r1, r2]
    meta = jnp.zeros(logits.shape, F32)
    for c, val in enumerate(cols):
        meta = jnp.where(lane == c, val, meta)
    meta_ref[...] = meta


def _router(stream, x, shift, scale, rw_pad, cnt_in):
    cnt_spec = _full_spec((SUBLANES, LANES))
    return pl.pallas_call(
        _router_kernel,
        grid=(stream.n // TM,),
        in_specs=[_row_spec(TM, D_MODEL), _mod_spec(stream, TM), _mod_spec(stream, TM),
                  _full_spec((D_MODEL, LANES)), cnt_spec],
        out_specs=[_row_spec(TM * TOKEN_TILE_ROWS, LANES), _row_spec(TM, LANES), cnt_spec],
        out_shape=[jax.ShapeDtypeStruct((stream.n * TOKEN_TILE_ROWS, LANES), F32),
                   jax.ShapeDtypeStruct((stream.n, LANES), F32),
                   jax.ShapeDtypeStruct((SUBLANES, LANES), F32)],
        scratch_shapes=[pltpu.VMEM((SUBLANES, LANES), F32)],
        compiler_params=_params("arbitrary"),
        name="moe_router",
    )(x, shift, scale, rw_pad, cnt_in)


def _tile_rows(token, count=1):
    start = token * TOKEN_TILE_ROWS
    if not isinstance(start, int):
        start = pl.multiple_of(start, TOKEN_TILE_ROWS)
    return pl.ds(start, count * TOKEN_TILE_ROWS)


def _to_token_tiles(ref, x):
    for j in range(TOKEN_TILE_ROWS):
        ref[pl.ds(j, x.shape[0], stride=TOKEN_TILE_ROWS), :] = x[:, j * LANES:(j + 1) * LANES]


def _from_token_tiles(ref, tokens):
    return jnp.concatenate([ref[pl.ds(j, tokens, stride=TOKEN_TILE_ROWS), :] for j in range(TOKEN_TILE_ROWS)], axis=1)


def _row_copy(src_ref, src_row, dst_ref, dst_row, sem):
    return pltpu.make_async_copy(src_ref.at[_tile_rows(src_row)], dst_ref.at[_tile_rows(dst_row)], sem)


def _scatter_rows(d0_ref, d1_ref, h_ref, buf_ref, sem):
    base = pl.program_id(0) * TM
    for r in range(TM):
        _row_copy(h_ref, r, buf_ref, d0_ref[base + r], sem).start(priority=0)
        _row_copy(h_ref, r, buf_ref, d1_ref[base + r], sem).start(priority=1)


def _scatter_wait(h_ref, buf_ref, sem, n_rows):
    pltpu.make_async_copy(h_ref.at[_tile_rows(0, n_rows)], buf_ref.at[_tile_rows(0, n_rows)], sem).wait()


def _scatter_kernel(d0_ref, d1_ref, pad_ref, hc_ref, hl_ref, buf_ref, zero_scr, sem, *, ctx_steps, pads_per_step):
    i = pl.program_id(0)

    @pl.when(i < ctx_steps)
    def _():
        _scatter_rows(d0_ref, d1_ref, hc_ref, buf_ref, sem)
        zero_scr[...] = jnp.zeros_like(zero_scr)
        pbase = i * pads_per_step
        for r in range(pads_per_step):
            _row_copy(zero_scr, 0, buf_ref, pad_ref[pbase + r], sem).start(priority=r % 2)
        _scatter_wait(hc_ref, buf_ref, sem, TM)
        _scatter_wait(hc_ref, buf_ref, sem, TM)
        _scatter_wait(hc_ref, buf_ref, sem, pads_per_step)

    @pl.when(i >= ctx_steps)
    def _():
        _scatter_rows(d0_ref, d1_ref, hl_ref, buf_ref, sem)
        _scatter_wait(hl_ref, buf_ref, sem, TM)
        _scatter_wait(hl_ref, buf_ref, sem, TM)


def _scatter(d0, d1, pad_rows, h_ctx, h_lat):
    ctx_steps = CTX.n // TM
    steps = ctx_steps + LAT.n // TM
    assert N_PAD_ROWS % ctx_steps == 0
    grid_spec = pltpu.PrefetchScalarGridSpec(
        num_scalar_prefetch=3,
        grid=(steps,),
        in_specs=[pl.BlockSpec((TM * TOKEN_TILE_ROWS, LANES), lambda i, a, b, c: (jnp.minimum(i, ctx_steps - 1), 0)),
                  pl.BlockSpec((TM * TOKEN_TILE_ROWS, LANES), lambda i, a, b, c: (jnp.maximum(i - ctx_steps, 0), 0))],
        out_specs=pl.BlockSpec(memory_space=pl.ANY),
        scratch_shapes=[pltpu.VMEM((TOKEN_TILE_ROWS, LANES), F32), pltpu.SemaphoreType.DMA],
    )
    return pl.pallas_call(
        functools.partial(_scatter_kernel, ctx_steps=ctx_steps, pads_per_step=N_PAD_ROWS // ctx_steps),
        grid_spec=grid_spec,
        out_shape=jax.ShapeDtypeStruct((NT_MOE * TMG * TOKEN_TILE_ROWS, LANES), F32),
        compiler_params=_params("arbitrary"),
        name="moe_scatter",
    )(d0, d1, pad_rows, h_ctx, h_lat)


def _gather_start(d0_ref, d1_ref, ys_ref, rows_scr, sems, tile, slot):
    base = tile * TM
    for r in range(TM):
        _row_copy(ys_ref, d0_ref[base + r], rows_scr.at[slot, 0], r, sems.at[slot]).start(priority=0)
        _row_copy(ys_ref, d1_ref[base + r], rows_scr.at[slot, 1], r, sems.at[slot]).start(priority=1)


def _gather_wait(ys_ref, rows_scr, sems, slot):
    for s in range(2):
        pltpu.make_async_copy(ys_ref.at[_tile_rows(0, TM)], rows_scr.at[slot, s], sems.at[slot]).wait()


def _combine_kernel(d0_ref, d1_ref, ys_ref, meta_ref, x_ref, gate_ref, g_ref, b_ref, o_ref, rows_scr, sems,
                    *, n_tiles):
    i = pl.program_id(0)
    slot = i % 2

    @pl.when(i == 0)
    def _():
        _gather_start(d0_ref, d1_ref, ys_ref, rows_scr, sems, 0, 0)

    _gather_start(d0_ref, d1_ref, ys_ref, rows_scr, sems, jnp.minimum(i + 1, n_tiles - 1), 1 - slot)
    _gather_wait(ys_ref, rows_scr, sems, slot)

    meta = meta_ref[...]
    w0 = meta[:, META_W0:META_W0 + 1]
    w1 = meta[:, META_W1:META_W1 + 1]
    f = w0 * _from_token_tiles(rows_scr.at[slot, 0], TM) + w1 * _from_token_tiles(rows_scr.at[slot, 1], TM)
    y = ALPHA * x_ref[...] + gate_ref[...] * f
    o_ref[...] = _layer_norm(y, g_ref[...], b_ref[...])

    @pl.when(i == n_tiles - 1)
    def _():
        _gather_wait(ys_ref, rows_scr, sems, 1 - slot)


def _combine(stream, d0, d1, ys, meta, x, gate, ln_g, ln_b):
    vec = pl.BlockSpec((1, D_MODEL), lambda i, a, b: (0, 0))
    grid_spec = pltpu.PrefetchScalarGridSpec(
        num_scalar_prefetch=2,
        grid=(stream.n // TM,),
        in_specs=[pl.BlockSpec(memory_space=pl.ANY), _row_spec(TM, LANES), _row_spec(TM, D_MODEL),
                  _mod_spec(stream, TM), vec, vec],
        out_specs=_row_spec(TM, D_MODEL),
        scratch_shapes=[pltpu.VMEM((2, 2, TM * TOKEN_TILE_ROWS, LANES), F32), pltpu.SemaphoreType.DMA((2,))],
    )
    return pl.pallas_call(
        functools.partial(_combine_kernel, n_tiles=stream.n // TM),
        grid_spec=grid_spec,
        out_shape=jax.ShapeDtypeStruct((stream.n, D_MODEL), F32),
        compiler_params=_params("arbitrary"),
        name="moe_combine",
    )(d0, d1, ys, meta, x, gate, ln_g, ln_b)


def _rope_tables():
    t = jnp.arange(DEC_SEQ)
    row = (t // GRID_W).astype(F32)
    col = (t % GRID_W).astype(F32)
    freqs = ROPE_THETA ** (-jnp.arange(ROT_FREQS, dtype=F32) / ROT_FREQS)
    ar = row[:, None] * freqs
    ac = col[:, None] * freqs
    cos = jnp.concatenate([jnp.cos(ar), jnp.cos(ar), jnp.cos(ac), jnp.cos(ac)], axis=1)
    sin = jnp.concatenate([-jnp.sin(ar), jnp.sin(ar), -jnp.sin(ac), jnp.sin(ac)], axis=1)
    return cos, sin


def _routing_plan(metas, counts):
    cnt = counts[0, :N_EXPERTS].astype(jnp.int32)
    tiles_e = (cnt + TMG - 1) // TMG
    tile_end = jnp.cumsum(tiles_e)
    tile_start = tile_end - tiles_e
    offs = tile_start * TMG
    dests = []
    for meta in metas:
        pair = []
        for ecol, rcol in ((META_E0, META_R0), (META_E1, META_R1)):
            e = meta[:, ecol].astype(jnp.int32)
            d = meta[:, rcol].astype(jnp.int32)
            for k in range(N_EXPERTS):
                d = d + jnp.where(e == k, offs[k], 0)
            pair.append(d)
        dests.append(tuple(pair))
    tid = jnp.arange(NT_MOE, dtype=jnp.int32)
    te = jnp.minimum(jnp.sum((tid[:, None] >= tile_end[None, :]).astype(jnp.int32), axis=1), N_EXPERTS - 1)
    total = tile_end[-1]
    valid = tid < total
    te_last = jnp.max(jnp.where(valid, te, 0))
    te = jnp.where(valid, te, te_last)
    pad_cnt = tiles_e * TMG - cnt
    pad_end = jnp.cumsum(pad_cnt)
    pad_start = pad_end - pad_cnt
    k = jnp.arange(N_PAD_ROWS, dtype=jnp.int32)
    grp = jnp.sum((k[:, None] >= pad_end[None, :]).astype(jnp.int32), axis=1)
    pad_rows = total * TMG + (k - pad_end[-1])
    for e in range(N_EXPERTS):
        pad_rows = jnp.where(grp == e, offs[e] + cnt[e] + (k - pad_start[e]), pad_rows)
    return dests, te, valid.astype(jnp.int32), pad_rows


def kernel(x_prompt, x_sample, cache_k_a, cache_v_a, cache_k_b, cache_v_b, c, c_ctx, ada_w, ada_b, ln_attn_g, ln_attn_b, ln_ffn_g, ln_ffn_b, wqkv_a, qnorm_a, knorm_a, wo_a, wqkv_b, rpb_b, wo_b, ffn_w_gu, ffn_w_down, router_w, moe_w_gu, moe_w_down):
    streams = (CTX, LAT)
    xs = [x_prompt.reshape(N_CTX, D_MODEL), x_sample.reshape(N_LAT, D_MODEL)]

    cond = jnp.zeros((GROUP_PAD, D_MODEL), F32).at[0].set(c_ctx).at[1:N_GROUPS].set(c)
    mods = _ada_mods(cond, ada_w, ada_b)
    vec = lambda a, l: a[l].reshape(1, D_MODEL)

    m = mods[0]
    w_qkv = wqkv_a[0].astype(BF16)
    w_o = wo_a[0].astype(BF16)
    w_gu = ffn_w_gu[0].astype(BF16)
    w_dn = ffn_w_down[0].astype(BF16)
    gains = (qnorm_a[0].reshape(1, HEAD_DIM_A), knorm_a[0].reshape(1, HEAD_DIM_A))
    nq, nk = N_HEADS_A * HEAD_DIM_A, N_KV_A * HEAD_DIM_A
    qscale = HEAD_DIM_A ** -0.5

    qp, kp, vp, kp32, vp32 = _qkv(CTX, xs[0], m[0], m[1], w_qkv, nq=nq, nk=nk, qscale=qscale, gains=gains,
                                  emit_f32=True)
    new_k_a = kp32.reshape(BATCH, 1, SEQ, N_KV_A, HEAD_DIM_A)
    new_v_a = vp32.reshape(BATCH, 1, SEQ, N_KV_A, HEAD_DIM_A)
    ql, kl, vlt = _qkv(LAT, xs[1], m[0], m[1], w_qkv, nq=nq, nk=nk, qscale=qscale * LOG2_E, gains=gains,
                       rope_tables=_rope_tables(), v_layout="t")
    cache_k = cache_k_a[:, 0].reshape(DEC_BATCH, PAST_LEN, nk).astype(BF16)
    cache_vt = jnp.swapaxes(cache_v_a[:, 0].reshape(DEC_BATCH, PAST_LEN, nk), 1, 2).astype(BF16)
    attn = [_gqa_ctx(qp, kp, vp), _gqa_lat_t(ql, kl, vlt, cache_k, cache_vt)]
    xs = [_layer_dense(s, a, w_o, x, m[2], vec(ln_attn_g, 0), vec(ln_attn_b, 0),
                       m[3], m[4], w_gu, w_dn, m[5], vec(ln_ffn_g, 0), vec(ln_ffn_b, 0))
          for s, a, x in zip(streams, attn, xs)]

    m = mods[1]
    w_qkv = wqkv_b[0].astype(BF16)
    w_o = wo_b[0].astype(BF16)
    qscale = HEAD_DIM_B ** -0.5
    qp, kp, vp, kp32, vp32 = _qkv(CTX, xs[0], m[0], m[1], w_qkv, nq=D_MODEL, nk=D_MODEL, qscale=qscale,
                                  emit_f32=True)
    new_k_b = kp32.reshape(BATCH, 1, SEQ, N_HEADS_B, HEAD_DIM_B)
    new_v_b = vp32.reshape(BATCH, 1, SEQ, N_HEADS_B, HEAD_DIM_B)
    ql, kl, vlt = _qkv(LAT, xs[1], m[0], m[1], w_qkv, nq=D_MODEL, nk=D_MODEL, qscale=qscale * LOG2_E,
                       v_layout="t_blocked")
    cache_k = cache_k_b[:, 0].reshape(DEC_BATCH, PAST_LEN, D_MODEL).astype(BF16)
    cache_vt = jnp.swapaxes(cache_v_b[:, 0].reshape(DEC_BATCH, PAST_LEN, D_MODEL), 1, 2).astype(BF16)
    attn = [_mha_ctx(qp, kp, vp), _natten(ql, kl, vlt, cache_k, cache_vt, _bias_table(rpb_b[0]))]
    xs = [_wo_ln(s, a, w_o, x, m[2], vec(ln_attn_g, 1), vec(ln_attn_b, 1)) for s, a, x in zip(streams, attn, xs)]

    rw_pad = jnp.zeros((D_MODEL, LANES), F32).at[:, :N_EXPERTS].set(router_w[0])
    counts = jnp.zeros((SUBLANES, LANES), F32)
    hs, metas = [], []
    for s, x in zip(streams, xs):
        h, meta, counts = _router(s, x, m[3], m[4], rw_pad, counts)
        hs.append(h)
        metas.append(meta)
    dests, tile_expert, tile_valid, pad_rows = _routing_plan(metas, counts)
    d0_all = jnp.concatenate([dests[0][0], dests[1][0]])
    d1_all = jnp.concatenate([dests[0][1], dests[1][1]])
    sorted_rows = _scatter(d0_all, d1_all, pad_rows, hs[0], hs[1])
    ys = _ffn_grouped(tile_expert, tile_valid, sorted_rows, moe_w_gu[0].astype(BF16), moe_w_down[0].astype(BF16))
    outs = [_combine(s, d0, d1, ys, meta, x, m[5], vec(ln_ffn_g, 1), vec(ln_ffn_b, 1))
            for s, (d0, d1), meta, x in zip(streams, dests, metas, xs)]

    y_prompt = outs[0].reshape(BATCH, SEQ, D_MODEL)
    y_sample = outs[1].reshape(DEC_BATCH, DEC_SEQ, D_MODEL)
    return (y_prompt, y_sample, new_k_a, new_v_a, new_k_b, new_v_b)
```

```python
import functools
import math

import jax
import jax.numpy as jnp
from jax import lax
from jax.experimental import pallas as pl
from jax.experimental.pallas import tpu as pltpu

F32 = jnp.float32
BF16 = jnp.bfloat16
HIGHEST = lax.Precision.HIGHEST

D_MODEL = 1024
BATCH, SEQ = 32, 256
DEC_BATCH, DEC_SEQ = 4, 4096
PAST_LEN = 256
DEPTH = 2
GRID_W = 64
GRID_H = DEC_SEQ // GRID_W
N_HEADS_A, N_KV_A, HEAD_DIM_A = 8, 2, 128
ROT_FREQS = HEAD_DIM_A // 4
ROPE_THETA = 10000.0
N_HEADS_B, HEAD_DIM_B = 16, 64
WIN_H, WIN_W = 8, 16
D_FF = 2816
N_EXPERTS = 8
EPS = 1e-6
NEG_INF = -1e30
ALPHA = (2.0 * DEPTH) ** 0.25
LOG2_E = math.log2(math.e)

N_CTX = BATCH * SEQ
N_LAT = DEC_BATCH * DEC_SEQ
N_GROUPS = 1 + DEC_BATCH
GROUP_PAD = 8

LANES = 128
SUBLANES = 8
VMEM_LIMIT = 56 * 2**20

TM = 512
MXU_DIM = 256
FF_CHUNK = 4 * MXU_DIM
FF_CHUNKS = tuple((lo, min(lo + FF_CHUNK, D_FF)) for lo in range(0, D_FF, FF_CHUNK))
assert D_FF % MXU_DIM == 0
TM_FFN = 512
TMG = 512
TQ_A = 512
UNIT_A = 2
TN_ADA = 1536
N_PAIRS = 2 * (N_CTX + N_LAT)
NT_MOE = N_PAIRS // TMG + N_EXPERTS
N_PAD_ROWS = NT_MOE * TMG - N_PAIRS
TOKEN_TILE_ROWS = D_MODEL // LANES
assert TOKEN_TILE_ROWS == SUBLANES


class Stream:
    def __init__(self, n_rows, group_offset, rows_per_group):
        self.n = n_rows
        self.goff = group_offset
        self.rpg = rows_per_group

    def group_map(self, tm):
        tiles_per_group = self.rpg // tm
        goff = self.goff
        return lambda i, *_: (goff + i // tiles_per_group, 0, 0)


CTX = Stream(N_CTX, 0, N_CTX)
LAT = Stream(N_LAT, 1, DEC_SEQ)


def _params(*sem):
    return pltpu.CompilerParams(dimension_semantics=sem, vmem_limit_bytes=VMEM_LIMIT)


def _mod_spec(stream, tm):
    return pl.BlockSpec((None, 1, D_MODEL), stream.group_map(tm))


def _row_spec(tm, width):
    return pl.BlockSpec((tm, width), lambda i, *_: (i, 0))


def _full_spec(shape):
    nd = len(shape)
    return pl.BlockSpec(shape, lambda *_: (0,) * nd)


def _layer_norm(y, g, b):
    mu = jnp.mean(y, axis=-1, keepdims=True)
    d = y - mu
    var = jnp.mean(d * d, axis=-1, keepdims=True)
    return d * lax.rsqrt(var + EPS) * g + b


def _ada_kernel(c_ref, w_ref, b_ref, o_ref):
    c = c_ref[...]
    s = c * jax.nn.sigmoid(c)
    o_ref[...] = jnp.dot(s, w_ref[...], preferred_element_type=F32, precision=HIGHEST) + b_ref[...]


def _ada_mods(cond, ada_w, ada_b):
    n_out = 6 * D_MODEL
    out = pl.pallas_call(
        _ada_kernel,
        grid=(DEPTH, n_out // TN_ADA),
        in_specs=[
            pl.BlockSpec((GROUP_PAD, D_MODEL), lambda l, n: (0, 0)),
            pl.BlockSpec((None, D_MODEL, TN_ADA), lambda l, n: (l, 0, n)),
            pl.BlockSpec((None, 1, TN_ADA), lambda l, n: (l, 0, n)),
        ],
        out_specs=pl.BlockSpec((None, GROUP_PAD, TN_ADA), lambda l, n: (l, 0, n)),
        out_shape=jax.ShapeDtypeStruct((DEPTH, GROUP_PAD, n_out), F32),
        compiler_params=_params("arbitrary", "arbitrary"),
        name="ada_mods",
    )(cond, ada_w, ada_b.reshape(DEPTH, 1, n_out))
    out = out.reshape(DEPTH, GROUP_PAD, 6, D_MODEL).transpose(0, 2, 1, 3)
    return out[:, :, :, None, :]


def _swap_halves(t):
    lane = lax.broadcasted_iota(jnp.int32, t.shape, 1)
    fwd = pltpu.roll(t, LANES - ROT_FREQS, 1)
    bwd = pltpu.roll(t, ROT_FREQS, 1)
    return jnp.where((lane % (2 * ROT_FREQS)) < ROT_FREQS, fwd, bwd)


def _qkv_kernel(*refs, nq, nk, norm, rope, emit_f32, v_layout, qscale):
    refs = list(refs)
    x_ref, sh_ref, sc_ref, w_ref = refs[:4]
    pos = 4
    if norm:
        qg_ref, kg_ref = refs[pos:pos + 2]
        pos += 2
    if rope:
        cos_ref, sin_ref = refs[pos:pos + 2]
        pos += 2
    q_ref, k_ref, v_ref = refs[pos:pos + 3]
    pos += 3
    if emit_f32:
        kf_ref, vf_ref = refs[pos:pos + 2]

    h = (x_ref[...] * (1.0 + sc_ref[...]) + sh_ref[...]).astype(BF16)
    qkv = jnp.dot(h, w_ref[...], preferred_element_type=F32)
    if norm:
        n_heads = (nq + nk) // HEAD_DIM_A
        for hd in range(n_heads):
            lo = hd * HEAD_DIM_A
            t = qkv[:, lo:lo + HEAD_DIM_A]
            ms = jnp.mean(t * t, axis=-1, keepdims=True)
            gain = qg_ref[...] if lo < nq else kg_ref[...]
            t = t * lax.rsqrt(ms + EPS) * gain
            if rope:
                t = t * cos_ref[...] + _swap_halves(t) * sin_ref[...]
            if lo < nq:
                q_ref[:, lo:lo + HEAD_DIM_A] = (t * qscale).astype(BF16)
            else:
                k_ref[:, lo - nq:lo - nq + HEAD_DIM_A] = t.astype(BF16)
                if emit_f32:
                    kf_ref[:, lo - nq:lo - nq + HEAD_DIM_A] = t
    else:
        q_ref[...] = (qkv[:, :nq] * qscale).astype(BF16)
        k = qkv[:, nq:nq + nk]
        k_ref[...] = k.astype(BF16)
        if emit_f32:
            kf_ref[...] = k
    v = qkv[:, nq + nk:]
    if v_layout == "rows":
        v_ref[...] = v.astype(BF16)
    elif v_layout == "t":
        v_ref[...] = v.T.astype(BF16)
    else:
        vt = v.T.astype(BF16)
        for j in range(vt.shape[1] // LANES):
            v_ref[j] = vt[:, j * LANES:(j + 1) * LANES]
    if emit_f32:
        vf_ref[...] = v


def _qkv(stream, x, shift, scale, w, *, nq, nk, qscale, gains=None, rope_tables=None, emit_f32=False,
         v_layout="rows"):
    n = stream.n
    nw = w.shape[1]
    norm = gains is not None
    rope = rope_tables is not None
    in_specs = [_row_spec(TM, D_MODEL), _mod_spec(stream, TM), _mod_spec(stream, TM), _full_spec((D_MODEL, nw))]
    args = [x, shift, scale, w]
    if norm:
        in_specs += [_full_spec((1, HEAD_DIM_A))] * 2
        args += list(gains)
    if rope:
        tiles_per_seq = DEC_SEQ // TM
        tbl_spec = pl.BlockSpec((TM, HEAD_DIM_A), lambda i: (i % tiles_per_seq, 0))
        in_specs += [tbl_spec, tbl_spec]
        args += list(rope_tables)
    out_specs = [_row_spec(TM, nq), _row_spec(TM, nk), _row_spec(TM, nk)]
    out_shape = [jax.ShapeDtypeStruct((n, nq), BF16), jax.ShapeDtypeStruct((n, nk), BF16),
                 jax.ShapeDtypeStruct((n, nk), BF16)]
    if v_layout == "t":
        out_specs[2] = pl.BlockSpec((nk, TM), lambda i: (0, i))
        out_shape[2] = jax.ShapeDtypeStruct((nk, n), BF16)
    elif v_layout == "t_blocked":
        out_specs[2] = pl.BlockSpec((TM // LANES, nk, LANES), lambda i: (i, 0, 0))
        out_shape[2] = jax.ShapeDtypeStruct((n // LANES, nk, LANES), BF16)
    if emit_f32:
        out_specs += [_row_spec(TM, nk), _row_spec(TM, nk)]
        out_shape += [jax.ShapeDtypeStruct((n, nk), F32)] * 2
    return pl.pallas_call(
        functools.partial(_qkv_kernel, nq=nq, nk=nk, norm=norm, rope=rope, emit_f32=emit_f32,
                          v_layout=v_layout, qscale=qscale),
        grid=(n // TM,),
        in_specs=in_specs,
        out_specs=out_specs,
        out_shape=out_shape,
        compiler_params=_params("arbitrary"),
        name="qkv_norm_rope" if norm else "qkv",
    )(*args)


def _attend(q, kvs, biases=None):
    scores = []
    for idx, (k, _) in enumerate(kvs):
        s = lax.dot_general(q, k, (((1,), (1,)), ((), ())), preferred_element_type=F32)
        if biases is not None and biases[idx] is not None:
            s = s + biases[idx]
        scores.append(s)
    m = jnp.max(scores[0], axis=-1, keepdims=True)
    for s in scores[1:]:
        m = jnp.maximum(m, jnp.max(s, axis=-1, keepdims=True))
    denom = None
    out = None
    for s, (_, v) in zip(scores, kvs):
        p = jnp.exp(s - m)
        part = jnp.sum(p, axis=-1, keepdims=True)
        pv = jnp.dot(p.astype(BF16), v, preferred_element_type=F32)
        denom = part if denom is None else denom + part
        out = pv if out is None else out + pv
    return out * (1.0 / denom)


def _gqa_ctx_kernel(q_ref, k_ref, v_ref, o_ref):
    d = HEAD_DIM_A
    group = N_HEADS_A // N_KV_A
    for g in range(N_KV_A):
        heads = [g * group + j for j in range(group)]
        qs = jnp.concatenate([q_ref[:, h * d:(h + 1) * d] for h in heads], axis=0)
        o = _attend(qs, [(k_ref[:, g * d:(g + 1) * d], v_ref[:, g * d:(g + 1) * d])])
        for j, h in enumerate(heads):
            o_ref[:, h * d:(h + 1) * d] = o[j * SEQ:(j + 1) * SEQ].astype(o_ref.dtype)


def _gqa_ctx(q, k, v):
    nq, nk = N_HEADS_A * HEAD_DIM_A, N_KV_A * HEAD_DIM_A
    return pl.pallas_call(
        _gqa_ctx_kernel,
        grid=(BATCH,),
        in_specs=[_row_spec(SEQ, nq), _row_spec(SEQ, nk), _row_spec(SEQ, nk)],
        out_specs=_row_spec(SEQ, nq),
        out_shape=jax.ShapeDtypeStruct((N_CTX, nq), BF16),
        compiler_params=_params("arbitrary"),
        name="gqa_ctx",
    )(q, k, v)


ONES_ROWS = 16


def _scores_t(q, ks):
    return [lax.dot_general(k, q, (((1,), (1,)), ((), ())), preferred_element_type=F32) for k in ks]


def _softmax_pv_t(scores, vts):
    m = jnp.max(scores[0], axis=0, keepdims=True)
    for s in scores[1:]:
        m = jnp.maximum(m, jnp.max(s, axis=0, keepdims=True))
    acc = None
    for s, vt in zip(scores, vts):
        p = jnp.exp2(s - m).astype(BF16)
        vt_ones = jnp.concatenate([vt, jnp.ones((ONES_ROWS, vt.shape[1]), BF16)], axis=0)
        pv = jnp.dot(vt_ones, p, preferred_element_type=F32)
        acc = pv if acc is None else acc + pv
    d = LANES
    return (acc[:d] * (1.0 / acc[d:d + 1])).T


def _gqa_t_kernel(q_ref, k_ref, vt_ref, kc_ref, vct_ref, o_ref, *, n_kv, group, unit, tq):
    d = HEAD_DIM_A
    units = [(g, [g * group + u0 + j for j in range(unit)]) for g in range(n_kv) for u0 in range(0, group, unit)]
    def unit_scores(g, heads):
        ds = slice(g * d, (g + 1) * d)
        qs = jnp.concatenate([q_ref[:, h * d:(h + 1) * d] for h in heads], axis=0)
        return _scores_t(qs, [k_ref[:, ds], kc_ref[:, ds]])

    nxt = unit_scores(*units[0])
    for idx, (g, heads) in enumerate(units):
        sc = nxt
        if idx + 1 < len(units):
            nxt = unit_scores(*units[idx + 1])
        ds = slice(g * d, (g + 1) * d)
        o = _softmax_pv_t(sc, [vt_ref[ds, :], vct_ref[ds, :]])
        for j, h in enumerate(heads):
            o_ref[:, h * d:(h + 1) * d] = o[j * tq:(j + 1) * tq].astype(o_ref.dtype)


def _gqa_lat_t(q, k, vt, kc, vct):
    nq, nk = N_HEADS_A * HEAD_DIM_A, N_KV_A * HEAD_DIM_A
    tiles = DEC_SEQ // TQ_A
    q_spec = pl.BlockSpec((TQ_A, nq), lambda b, t: (b * tiles + t, 0))
    return pl.pallas_call(
        functools.partial(_gqa_t_kernel, n_kv=N_KV_A, group=N_HEADS_A // N_KV_A, unit=UNIT_A, tq=TQ_A),
        grid=(DEC_BATCH, tiles),
        in_specs=[q_spec,
                  pl.BlockSpec((DEC_SEQ, nk), lambda b, t: (b, 0)),
                  pl.BlockSpec((nk, DEC_SEQ), lambda b, t: (0, b)),
                  pl.BlockSpec((None, PAST_LEN, nk), lambda b, t: (b, 0, 0)),
                  pl.BlockSpec((None, nk, PAST_LEN), lambda b, t: (b, 0, 0))],
        out_specs=q_spec,
        out_shape=jax.ShapeDtypeStruct((N_LAT, nq), BF16),
        compiler_params=_params("arbitrary", "arbitrary"),
        name="gqa_lat",
    )(q, k, vt, kc, vct)


def _split_pair(qp):
    lo = lax.broadcasted_iota(jnp.int32, qp.shape, 1) < HEAD_DIM_B
    zero = jnp.zeros_like(qp)
    return jnp.concatenate([jnp.where(lo, qp, zero), jnp.where(lo, zero, qp)], axis=0)


def _merge_pair(o2, rows):
    lo = lax.broadcasted_iota(jnp.int32, (rows, LANES), 1) < HEAD_DIM_B
    return jnp.where(lo, o2[:rows], o2[rows:])


def _mha_ctx_kernel(q_ref, k_ref, v_ref, o_ref):
    for p in range(N_HEADS_B // 2):
        sl = slice(p * LANES, (p + 1) * LANES)
        o2 = _attend(_split_pair(q_ref[:, sl]), [(k_ref[:, sl], v_ref[:, sl])])
        o_ref[:, sl] = _merge_pair(o2, SEQ).astype(o_ref.dtype)


def _mha_ctx(q, k, v):
    return pl.pallas_call(
        _mha_ctx_kernel,
        grid=(BATCH,),
        in_specs=[_row_spec(SEQ, D_MODEL)] * 3,
        out_specs=_row_spec(SEQ, D_MODEL),
        out_shape=jax.ShapeDtypeStruct((N_CTX, D_MODEL), BF16),
        compiler_params=_params("arbitrary"),
        name="mha_ctx",
    )(q, k, v)


NAT_HALF = D_MODEL
NAT_PAIRS = NAT_HALF // LANES
NAT_R = 4
NAT_WROWS = 12
N_DROW = 2 * WIN_H - 1
MASKED_TILE = N_DROW


def _natten_kernel(q_ref, k_ref, vt_ref, kc_ref, vct_ref, tbl_ref, o_ref):
    r0 = pl.program_id(2) * NAT_R
    ws = jnp.clip(r0 - WIN_H // 2, 0, GRID_H - NAT_WROWS)
    row0 = pl.multiple_of(ws * GRID_W, LANES)
    blk0 = ws * GRID_W // LANES
    n_loc = NAT_WROWS * GRID_W

    def tile_index(a, w):
        r = r0 + a
        start = jnp.clip(r - WIN_H // 2, 0, GRID_H - WIN_H)
        wr = ws + w
        valid = (wr >= start) & (wr < start + WIN_H)
        return jnp.where(valid, wr - r + (WIN_H - 1), MASKED_TILE)

    idx = [[tile_index(a, w) for w in range(NAT_WROWS)] for a in range(NAT_R)]

    def pair_scores(p):
        sl = slice(p * LANES, (p + 1) * LANES)
        q2 = jnp.concatenate([_split_pair(q_ref[a * GRID_W:(a + 1) * GRID_W, sl]) for a in range(NAT_R)], axis=0)
        bias_t = jnp.concatenate(
            [jnp.concatenate([tbl_ref[idx[a][w], p] for w in range(NAT_WROWS)], axis=0) for a in range(NAT_R)],
            axis=1)
        s_loc, s_ctx = _scores_t(q2, [k_ref[pl.ds(row0, n_loc), sl], kc_ref[:, sl]])
        return [s_loc + bias_t, s_ctx]

    nxt = pair_scores(0)
    for p in range(NAT_PAIRS):
        sl = slice(p * LANES, (p + 1) * LANES)
        sc = nxt
        if p + 1 < NAT_PAIRS:
            nxt = pair_scores(p + 1)
        vt_win = jnp.concatenate([vt_ref[blk0 + j, sl, :] for j in range(n_loc // LANES)], axis=1)
        o2 = _softmax_pv_t(sc, [vt_win, vct_ref[sl, :]])
        for a in range(NAT_R):
            o_ref[a * GRID_W:(a + 1) * GRID_W, sl] = _merge_pair(
                o2[a * LANES:(a + 1) * LANES], GRID_W).astype(o_ref.dtype)


def _natten(q, k, vt, kc, vct, tbl):
    n_half = D_MODEL // NAT_HALF
    blocks = GRID_H // NAT_R
    seq_blocks = DEC_SEQ // LANES
    q_spec = pl.BlockSpec((NAT_R * GRID_W, NAT_HALF), lambda b, hh, r: (b * blocks + r, hh))
    return pl.pallas_call(
        _natten_kernel,
        grid=(DEC_BATCH, n_half, blocks),
        in_specs=[q_spec,
                  pl.BlockSpec((DEC_SEQ, NAT_HALF), lambda b, hh, r: (b, hh)),
                  pl.BlockSpec((seq_blocks, NAT_HALF, LANES), lambda b, hh, r: (b, hh, 0)),
                  pl.BlockSpec((None, PAST_LEN, NAT_HALF), lambda b, hh, r: (b, 0, hh)),
                  pl.BlockSpec((None, NAT_HALF, PAST_LEN), lambda b, hh, r: (b, hh, 0)),
                  pl.BlockSpec((N_DROW + 1, NAT_PAIRS, GRID_W, LANES), lambda b, hh, r: (0, hh, 0, 0))],
        out_specs=q_spec,
        out_shape=jax.ShapeDtypeStruct((N_LAT, D_MODEL), BF16),
        compiler_params=_params("arbitrary", "arbitrary", "arbitrary"),
        name="natten",
    )(q, k, vt, kc, vct, tbl)


def _bias_table_kernel(rpb_ref, o_ref):
    dr = pl.program_id(0)
    row = lax.broadcasted_iota(jnp.int32, (GRID_W, LANES), 0)
    col = lax.broadcasted_iota(jnp.int32, (GRID_W, LANES), 1)
    sub = col >= GRID_W
    qc = col % GRID_W
    kc = row
    dc = jnp.clip(kc - qc, -(WIN_W - 1), WIN_W - 1) + (WIN_W - 1)
    col_start = jnp.clip(qc - WIN_W // 2, 0, GRID_W - WIN_W)
    in_win = (kc >= col_start) & (kc < col_start + WIN_W) & (dr < N_DROW)
    n_dc = 2 * WIN_W - 1
    drc = jnp.minimum(dr, N_DROW - 1)
    for pr in range(N_HEADS_B // 2):
        b0 = ((2 * pr) * N_DROW + drc) * n_dc
        b1 = ((2 * pr + 1) * N_DROW + drc) * n_dc

        def body(j, acc, b0=b0, b1=b1):
            return jnp.where(dc == j, jnp.where(sub, rpb_ref[b1 + j], rpb_ref[b0 + j]), acc)

        acc = lax.fori_loop(0, n_dc, body, jnp.zeros((GRID_W, LANES), F32))
        o_ref[pr] = jnp.where(in_win, acc * LOG2_E, NEG_INF)


def _bias_table(rpb):
    return pl.pallas_call(
        _bias_table_kernel,
        grid=(N_DROW + 1,),
        in_specs=[pl.BlockSpec(memory_space=pltpu.SMEM)],
        out_specs=pl.BlockSpec((None, N_HEADS_B // 2, GRID_W, LANES), lambda d: (d, 0, 0, 0)),
        out_shape=jax.ShapeDtypeStruct((N_DROW + 1, N_HEADS_B // 2, GRID_W, LANES), F32),
        compiler_params=_params("arbitrary"),
        name="natten_bias_table",
    )(rpb.reshape(-1))


def _attn_residual(a_ref, wo_ref, x_ref, gate_ref, g_ref, b_ref):
    o = jnp.dot(a_ref[...], wo_ref[...], preferred_element_type=F32)
    return _layer_norm(ALPHA * x_ref[...] + gate_ref[...] * o, g_ref[...], b_ref[...])


def _attn_residual_specs(stream, tm):
    vec = _full_spec((1, D_MODEL))
    return [_row_spec(tm, D_MODEL), _full_spec((D_MODEL, D_MODEL)), _row_spec(tm, D_MODEL),
            _mod_spec(stream, tm), vec, vec]


def _wo_ln_kernel(a_ref, wo_ref, x_ref, gate_ref, g_ref, b_ref, o_ref):
    o_ref[...] = _attn_residual(a_ref, wo_ref, x_ref, gate_ref, g_ref, b_ref)


def _wo_ln(stream, attn, w_o, x, gate, ln_g, ln_b):
    return pl.pallas_call(
        _wo_ln_kernel,
        grid=(stream.n // TM,),
        in_specs=_attn_residual_specs(stream, TM),
        out_specs=_row_spec(TM, D_MODEL),
        out_shape=jax.ShapeDtypeStruct((stream.n, D_MODEL), F32),
        compiler_params=_params("arbitrary"),
        name="wo_postnorm",
    )(attn, w_o, x, gate, ln_g, ln_b)


def _swiglu(h, wg_ref, wu_ref, wd_ref):
    acc = None
    for lo, hi in FF_CHUNKS:
        g = jnp.dot(h, wg_ref[:, lo:hi], preferred_element_type=F32)
        u = jnp.dot(h, wu_ref[:, lo:hi], preferred_element_type=F32)
        a = (g * jax.nn.sigmoid(g) * u).astype(BF16)
        part = jnp.dot(a, wd_ref[lo:hi, :], preferred_element_type=F32)
        acc = part if acc is None else acc + part
    return acc


def _layer_dense_kernel(a_ref, wo_ref, x_ref, gate1_ref, g1_ref, b1_ref,
                        sh_ref, sc_ref, wg_ref, wu_ref, wd_ref, gate_ref, g_ref, b_ref, o_ref):
    x = _attn_residual(a_ref, wo_ref, x_ref, gate1_ref, g1_ref, b1_ref)
    h = (x * (1.0 + sc_ref[...]) + sh_ref[...]).astype(BF16)
    y = ALPHA * x + gate_ref[...] * _swiglu(h, wg_ref, wu_ref, wd_ref)
    o_ref[...] = _layer_norm(y, g_ref[...], b_ref[...])


def _layer_dense(stream, attn, w_o, x, gate1, ln1_g, ln1_b, shift, scale, w_gu, w_down, gate, ln_g, ln_b):
    vec = _full_spec((1, D_MODEL))
    mod = _mod_spec(stream, TM_FFN)
    return pl.pallas_call(
        _layer_dense_kernel,
        grid=(stream.n // TM_FFN,),
        in_specs=_attn_residual_specs(stream, TM_FFN) + [
            mod, mod,
            pl.BlockSpec((D_MODEL, D_FF), lambda i: (0, 0)),
            pl.BlockSpec((D_MODEL, D_FF), lambda i: (0, 1)),
            _full_spec((D_FF, D_MODEL)),
            mod, vec, vec],
        out_specs=_row_spec(TM_FFN, D_MODEL),
        out_shape=jax.ShapeDtypeStruct((stream.n, D_MODEL), F32),
        compiler_params=_params("arbitrary"),
        name="wo_ffn_dense",
    )(attn, w_o, x, gate1, ln1_g, ln1_b, shift, scale, w_gu, w_gu, w_down, gate, ln_g, ln_b)


def _ffn_grouped_kernel(te_ref, tv_ref, x_ref, wg_ref, wu_ref, wd_ref, o_ref):
    i = pl.program_id(0)

    @pl.when(tv_ref[i] != 0)
    def _():
        x = _from_token_tiles(x_ref, TMG).astype(BF16)
        _to_token_tiles(o_ref, _swiglu(x, wg_ref, wu_ref, wd_ref))

    @pl.when(tv_ref[i] == 0)
    def _():
        o_ref[...] = jnp.zeros_like(o_ref)


def _ffn_grouped(tile_expert, tile_valid, xs, w_gu, w_down):
    row = pl.BlockSpec((TMG * TOKEN_TILE_ROWS, LANES), lambda i, te, tv: (i, 0))
    grid_spec = pltpu.PrefetchScalarGridSpec(
        num_scalar_prefetch=2,
        grid=(NT_MOE,),
        in_specs=[row,
                  pl.BlockSpec((None, D_MODEL, D_FF), lambda i, te, tv: (te[i], 0, 0)),
                  pl.BlockSpec((None, D_MODEL, D_FF), lambda i, te, tv: (te[i], 0, 1)),
                  pl.BlockSpec((None, D_FF, D_MODEL), lambda i, te, tv: (te[i], 0, 0))],
        out_specs=row,
    )
    return pl.pallas_call(
        _ffn_grouped_kernel,
        grid_spec=grid_spec,
        out_shape=jax.ShapeDtypeStruct((NT_MOE * TMG * TOKEN_TILE_ROWS, LANES), F32),
        compiler_params=_params("arbitrary"),
        name="ffn_grouped",
    )(tile_expert, tile_valid, xs, w_gu, w_gu, w_down)


META_E0, META_E1, META_W0, META_W1, META_R0, META_R1 = range(6)


def _router_kernel(x_ref, sh_ref, sc_ref, rw_ref, cnt_in_ref, h_ref, meta_ref, rec_ref, cnt_ref, carry_scr):
    i = pl.program_id(0)

    @pl.when(i == 0)
    def _():
        carry_scr[...] = cnt_in_ref[...]

    h = x_ref[...] * (1.0 + sc_ref[...]) + sh_ref[...]
    _to_token_tiles(h_ref, h)
    w = rw_ref[...]
    h_hi = h.astype(BF16)
    w_hi = w.astype(BF16)
    h_lo = (h - h_hi.astype(F32)).astype(BF16)
    w_lo = (w - w_hi.astype(F32)).astype(BF16)
    logits = (jnp.dot(h_hi, w_hi, preferred_element_type=F32)
              + (jnp.dot(h_lo, w_hi, preferred_element_type=F32)
                 + jnp.dot(h_hi, w_lo, preferred_element_type=F32)))
    lane = lax.broadcasted_iota(jnp.int32, logits.shape, 1).astype(F32)
    lg = jnp.where(lane < N_EXPERTS, logits, -jnp.inf)
    m1 = jnp.max(lg, axis=-1, keepdims=True)
    i1 = jnp.min(jnp.where(lg == m1, lane, float(LANES)), axis=-1, keepdims=True)
    lg2 = jnp.where(lane == i1, -jnp.inf, lg)
    m2 = jnp.max(lg2, axis=-1, keepdims=True)
    i2 = jnp.min(jnp.where(lg2 == m2, lane, float(LANES)), axis=-1, keepdims=True)
    e = jnp.exp(m2 - m1)
    w1 = 1.0 / (1.0 + e)
    w2 = e / (1.0 + e)

    sel1 = lane == i1
    sel2 = lane == i2
    onehot = jnp.where(sel1 | sel2, 1.0, 0.0)
    rr = lax.broadcasted_iota(jnp.int32, (TM, TM), 0)
    cc = lax.broadcasted_iota(jnp.int32, (TM, TM), 1)
    lower = jnp.where(cc < rr, 1.0, 0.0).astype(BF16)
    before = jnp.dot(lower, onehot.astype(BF16), preferred_element_type=F32) + carry_scr[0:1, :]
    r1 = jnp.sum(jnp.where(sel1, before, 0.0), axis=-1, keepdims=True)
    r2 = jnp.sum(jnp.where(sel2, before, 0.0), axis=-1, keepdims=True)
    carry_scr[...] = carry_scr[...] + jnp.sum(onehot, axis=0, keepdims=True)
    cnt_ref[...] = carry_scr[...]

    cols = [i1, i2, w1, w2, r1, r2]
    meta = jnp.zeros(logits.shape, F32)
    for c, val in enumerate(cols):
        meta = jnp.where(lane == c, val, meta)
    meta_ref[...] = meta
    rec_ref[...] = meta.T[:SUBLANES]


def _router(stream, x, shift, scale, rw_pad, cnt_in):
    cnt_spec = _full_spec((SUBLANES, LANES))
    return pl.pallas_call(
        _router_kernel,
        grid=(stream.n // TM,),
        in_specs=[_row_spec(TM, D_MODEL), _mod_spec(stream, TM), _mod_spec(stream, TM),
                  _full_spec((D_MODEL, LANES)), cnt_spec],
        out_specs=[_row_spec(TM * TOKEN_TILE_ROWS, LANES), _row_spec(TM, LANES),
                   pl.BlockSpec((SUBLANES, TM), lambda i: (0, i)), cnt_spec],
        out_shape=[jax.ShapeDtypeStruct((stream.n * TOKEN_TILE_ROWS, LANES), F32),
                   jax.ShapeDtypeStruct((stream.n, LANES), F32),
                   jax.ShapeDtypeStruct((SUBLANES, stream.n), F32),
                   jax.ShapeDtypeStruct((SUBLANES, LANES), F32)],
        scratch_shapes=[pltpu.VMEM((SUBLANES, LANES), F32)],
        compiler_params=_params("arbitrary"),
        name="moe_router",
    )(x, shift, scale, rw_pad, cnt_in)


def _tile_rows(token, count=1):
    start = token * TOKEN_TILE_ROWS
    if not isinstance(start, int):
        start = pl.multiple_of(start, TOKEN_TILE_ROWS)
    return pl.ds(start, count * TOKEN_TILE_ROWS)


def _to_token_tiles(ref, x):
    for j in range(TOKEN_TILE_ROWS):
        ref[pl.ds(j, x.shape[0], stride=TOKEN_TILE_ROWS), :] = x[:, j * LANES:(j + 1) * LANES]


def _from_token_tiles(ref, tokens):
    return jnp.concatenate([ref[pl.ds(j, tokens, stride=TOKEN_TILE_ROWS), :] for j in range(TOKEN_TILE_ROWS)], axis=1)


def _row_copy(src_ref, src_row, dst_ref, dst_row, sem):
    return pltpu.make_async_copy(src_ref.at[_tile_rows(src_row)], dst_ref.at[_tile_rows(dst_row)], sem)


def _scatter_rows(d0_ref, d1_ref, h_ref, buf_ref, sem):
    base = pl.program_id(0) * TM
    for r in range(TM):
        _row_copy(h_ref, r, buf_ref, d0_ref[base + r], sem).start(priority=0)
        _row_copy(h_ref, r, buf_ref, d1_ref[base + r], sem).start(priority=1)


def _scatter_wait(h_ref, buf_ref, sem, n_rows):
    pltpu.make_async_copy(h_ref.at[_tile_rows(0, n_rows)], buf_ref.at[_tile_rows(0, n_rows)], sem).wait()


def _scatter_kernel(d0_ref, d1_ref, pad_ref, hc_ref, hl_ref, buf_ref, zero_scr, sem, *, ctx_steps, pads_per_step):
    i = pl.program_id(0)

    @pl.when(i < ctx_steps)
    def _():
        _scatter_rows(d0_ref, d1_ref, hc_ref, buf_ref, sem)
        zero_scr[...] = jnp.zeros_like(zero_scr)
        pbase = i * pads_per_step
        for r in range(pads_per_step):
            _row_copy(zero_scr, 0, buf_ref, pad_ref[pbase + r], sem).start(priority=r % 2)
        _scatter_wait(hc_ref, buf_ref, sem, TM)
        _scatter_wait(hc_ref, buf_ref, sem, TM)
        _scatter_wait(hc_ref, buf_ref, sem, pads_per_step)

    @pl.when(i >= ctx_steps)
    def _():
        _scatter_rows(d0_ref, d1_ref, hl_ref, buf_ref, sem)
        _scatter_wait(hl_ref, buf_ref, sem, TM)
        _scatter_wait(hl_ref, buf_ref, sem, TM)


def _scatter(d0, d1, pad_rows, h_ctx, h_lat):
    ctx_steps = CTX.n // TM
    steps = ctx_steps + LAT.n // TM
    assert N_PAD_ROWS % ctx_steps == 0
    grid_spec = pltpu.PrefetchScalarGridSpec(
        num_scalar_prefetch=3,
        grid=(steps,),
        in_specs=[pl.BlockSpec((TM * TOKEN_TILE_ROWS, LANES), lambda i, a, b, c: (jnp.minimum(i, ctx_steps - 1), 0)),
                  pl.BlockSpec((TM * TOKEN_TILE_ROWS, LANES), lambda i, a, b, c: (jnp.maximum(i - ctx_steps, 0), 0))],
        out_specs=pl.BlockSpec(memory_space=pl.ANY),
        scratch_shapes=[pltpu.VMEM((TOKEN_TILE_ROWS, LANES), F32), pltpu.SemaphoreType.DMA],
    )
    return pl.pallas_call(
        functools.partial(_scatter_kernel, ctx_steps=ctx_steps, pads_per_step=N_PAD_ROWS // ctx_steps),
        grid_spec=grid_spec,
        out_shape=jax.ShapeDtypeStruct((NT_MOE * TMG * TOKEN_TILE_ROWS, LANES), F32),
        compiler_params=_params("arbitrary"),
        name="moe_scatter",
    )(d0, d1, pad_rows, h_ctx, h_lat)


def _gather_start(d0_ref, d1_ref, ys_ref, rows_scr, sems, tile, slot):
    base = tile * TM
    for r in range(TM):
        _row_copy(ys_ref, d0_ref[base + r], rows_scr.at[slot, 0], r, sems.at[slot]).start(priority=0)
        _row_copy(ys_ref, d1_ref[base + r], rows_scr.at[slot, 1], r, sems.at[slot]).start(priority=1)


def _gather_wait(ys_ref, rows_scr, sems, slot):
    for s in range(2):
        pltpu.make_async_copy(ys_ref.at[_tile_rows(0, TM)], rows_scr.at[slot, s], sems.at[slot]).wait()


def _combine_kernel(d0_ref, d1_ref, ys_ref, meta_ref, x_ref, gate_ref, g_ref, b_ref, o_ref, rows_scr, sems,
                    *, n_tiles):
    i = pl.program_id(0)
    slot = i % 2

    @pl.when(i == 0)
    def _():
        _gather_start(d0_ref, d1_ref, ys_ref, rows_scr, sems, 0, 0)

    _gather_start(d0_ref, d1_ref, ys_ref, rows_scr, sems, jnp.minimum(i + 1, n_tiles - 1), 1 - slot)
    _gather_wait(ys_ref, rows_scr, sems, slot)

    meta = meta_ref[...]
    w0 = meta[:, META_W0:META_W0 + 1]
    w1 = meta[:, META_W1:META_W1 + 1]
    f = w0 * _from_token_tiles(rows_scr.at[slot, 0], TM) + w1 * _from_token_tiles(rows_scr.at[slot, 1], TM)
    y = ALPHA * x_ref[...] + gate_ref[...] * f
    o_ref[...] = _layer_norm(y, g_ref[...], b_ref[...])

    @pl.when(i == n_tiles - 1)
    def _():
        _gather_wait(ys_ref, rows_scr, sems, 1 - slot)


def _combine(stream, d0, d1, ys, meta, x, gate, ln_g, ln_b):
    vec = pl.BlockSpec((1, D_MODEL), lambda i, a, b: (0, 0))
    grid_spec = pltpu.PrefetchScalarGridSpec(
        num_scalar_prefetch=2,
        grid=(stream.n // TM,),
        in_specs=[pl.BlockSpec(memory_space=pl.ANY), _row_spec(TM, LANES), _row_spec(TM, D_MODEL),
                  _mod_spec(stream, TM), vec, vec],
        out_specs=_row_spec(TM, D_MODEL),
        scratch_shapes=[pltpu.VMEM((2, 2, TM * TOKEN_TILE_ROWS, LANES), F32), pltpu.SemaphoreType.DMA((2,))],
    )
    return pl.pallas_call(
        functools.partial(_combine_kernel, n_tiles=stream.n // TM),
        grid_spec=grid_spec,
        out_shape=jax.ShapeDtypeStruct((stream.n, D_MODEL), F32),
        compiler_params=_params("arbitrary"),
        name="moe_combine",
    )(d0, d1, ys, meta, x, gate, ln_g, ln_b)


def _rope_tables():
    t = jnp.arange(DEC_SEQ)
    row = (t // GRID_W).astype(F32)
    col = (t % GRID_W).astype(F32)
    freqs = ROPE_THETA ** (-jnp.arange(ROT_FREQS, dtype=F32) / ROT_FREQS)
    ar = row[:, None] * freqs
    ac = col[:, None] * freqs
    cos = jnp.concatenate([jnp.cos(ar), jnp.cos(ar), jnp.cos(ac), jnp.cos(ac)], axis=1)
    sin = jnp.concatenate([-jnp.sin(ar), jnp.sin(ar), -jnp.sin(ac), jnp.sin(ac)], axis=1)
    return cos, sin


def _routing_plan(recs, counts):
    cnt = counts[0, :N_EXPERTS].astype(jnp.int32)
    tiles_e = (cnt + TMG - 1) // TMG
    tile_end = jnp.cumsum(tiles_e)
    tile_start = tile_end - tiles_e
    offs = tile_start * TMG
    dests = []
    for rec in recs:
        pair = []
        for ecol, rcol in ((META_E0, META_R0), (META_E1, META_R1)):
            e = rec[ecol].astype(jnp.int32)
            d = rec[rcol].astype(jnp.int32)
            for k in range(N_EXPERTS):
                d = d + jnp.where(e == k, offs[k], 0)
            pair.append(d)
        dests.append(tuple(pair))
    tid = jnp.arange(NT_MOE, dtype=jnp.int32)
    te = jnp.minimum(jnp.sum((tid[:, None] >= tile_end[None, :]).astype(jnp.int32), axis=1), N_EXPERTS - 1)
    total = tile_end[-1]
    valid = tid < total
    te_last = jnp.max(jnp.where(valid, te, 0))
    te = jnp.where(valid, te, te_last)
    pad_cnt = tiles_e * TMG - cnt
    pad_end = jnp.cumsum(pad_cnt)
    pad_start = pad_end - pad_cnt
    k = jnp.arange(N_PAD_ROWS, dtype=jnp.int32)
    grp = jnp.sum((k[:, None] >= pad_end[None, :]).astype(jnp.int32), axis=1)
    pad_rows = total * TMG + (k - pad_end[-1])
    for e in range(N_EXPERTS):
        pad_rows = jnp.where(grp == e, offs[e] + cnt[e] + (k - pad_start[e]), pad_rows)
    return dests, te, valid.astype(jnp.int32), pad_rows


def kernel(x_prompt, x_sample, cache_k_a, cache_v_a, cache_k_b, cache_v_b, c, c_ctx, ada_w, ada_b, ln_attn_g, ln_attn_b, ln_ffn_g, ln_ffn_b, wqkv_a, qnorm_a, knorm_a, wo_a, wqkv_b, rpb_b, wo_b, ffn_w_gu, ffn_w_down, router_w, moe_w_gu, moe_w_down):
    streams = (CTX, LAT)
    xs = [x_prompt.reshape(N_CTX, D_MODEL), x_sample.reshape(N_LAT, D_MODEL)]

    cond = jnp.zeros((GROUP_PAD, D_MODEL), F32).at[0].set(c_ctx).at[1:N_GROUPS].set(c)
    mods = _ada_mods(cond, ada_w, ada_b)
    vec = lambda a, l: a[l].reshape(1, D_MODEL)

    m = mods[0]
    w_qkv = wqkv_a[0].astype(BF16)
    w_o = wo_a[0].astype(BF16)
    w_gu = ffn_w_gu[0].astype(BF16)
    w_dn = ffn_w_down[0].astype(BF16)
    gains = (qnorm_a[0].reshape(1, HEAD_DIM_A), knorm_a[0].reshape(1, HEAD_DIM_A))
    nq, nk = N_HEADS_A * HEAD_DIM_A, N_KV_A * HEAD_DIM_A
    qscale = HEAD_DIM_A ** -0.5

    qp, kp, vp, kp32, vp32 = _qkv(CTX, xs[0], m[0], m[1], w_qkv, nq=nq, nk=nk, qscale=qscale, gains=gains,
                                  emit_f32=True)
    new_k_a = kp32.reshape(BATCH, 1, SEQ, N_KV_A, HEAD_DIM_A)
    new_v_a = vp32.reshape(BATCH, 1, SEQ, N_KV_A, HEAD_DIM_A)
    ql, kl, vlt = _qkv(LAT, xs[1], m[0], m[1], w_qkv, nq=nq, nk=nk, qscale=qscale * LOG2_E, gains=gains,
                       rope_tables=_rope_tables(), v_layout="t")
    cache_k = cache_k_a[:, 0].reshape(DEC_BATCH, PAST_LEN, nk).astype(BF16)
    cache_vt = jnp.swapaxes(cache_v_a[:, 0].reshape(DEC_BATCH, PAST_LEN, nk), 1, 2).astype(BF16)
    attn = [_gqa_ctx(qp, kp, vp), _gqa_lat_t(ql, kl, vlt, cache_k, cache_vt)]
    xs = [_layer_dense(s, a, w_o, x, m[2], vec(ln_attn_g, 0), vec(ln_attn_b, 0),
                       m[3], m[4], w_gu, w_dn, m[5], vec(ln_ffn_g, 0), vec(ln_ffn_b, 0))
          for s, a, x in zip(streams, attn, xs)]

    m = mods[1]
    w_qkv = wqkv_b[0].astype(BF16)
    w_o = wo_b[0].astype(BF16)
    qscale = HEAD_DIM_B ** -0.5
    qp, kp, vp, kp32, vp32 = _qkv(CTX, xs[0], m[0], m[1], w_qkv, nq=D_MODEL, nk=D_MODEL, qscale=qscale,
                                  emit_f32=True)
    new_k_b = kp32.reshape(BATCH, 1, SEQ, N_HEADS_B, HEAD_DIM_B)
    new_v_b = vp32.reshape(BATCH, 1, SEQ, N_HEADS_B, HEAD_DIM_B)
    ql, kl, vlt = _qkv(LAT, xs[1], m[0], m[1], w_qkv, nq=D_MODEL, nk=D_MODEL, qscale=qscale * LOG2_E,
                       v_layout="t_blocked")
    cache_k = cache_k_b[:, 0].reshape(DEC_BATCH, PAST_LEN, D_MODEL).astype(BF16)
    cache_vt = jnp.swapaxes(cache_v_b[:, 0].reshape(DEC_BATCH, PAST_LEN, D_MODEL), 1, 2).astype(BF16)
    attn = [_mha_ctx(qp, kp, vp), _natten(ql, kl, vlt, cache_k, cache_vt, _bias_table(rpb_b[0]))]
    xs = [_wo_ln(s, a, w_o, x, m[2], vec(ln_attn_g, 1), vec(ln_attn_b, 1)) for s, a, x in zip(streams, attn, xs)]

    rw_pad = jnp.zeros((D_MODEL, LANES), F32).at[:, :N_EXPERTS].set(router_w[0])
    counts = jnp.zeros((SUBLANES, LANES), F32)
    hs, metas, recs = [], [], []
    for s, x in zip(streams, xs):
        h, meta, rec, counts = _router(s, x, m[3], m[4], rw_pad, counts)
        hs.append(h)
        metas.append(meta)
        recs.append(rec)
    dests, tile_expert, tile_valid, pad_rows = _routing_plan(recs, counts)
    d0_all = jnp.concatenate([dests[0][0], dests[1][0]])
    d1_all = jnp.concatenate([dests[0][1], dests[1][1]])
    sorted_rows = _scatter(d0_all, d1_all, pad_rows, hs[0], hs[1])
    ys = _ffn_grouped(tile_expert, tile_valid, sorted_rows, moe_w_gu[0].astype(BF16), moe_w_down[0].astype(BF16))
    outs = [_combine(s, d0, d1, ys, meta, x, m[5], vec(ln_ffn_g, 1), vec(ln_ffn_b, 1))
            for s, (d0, d1), meta, x in zip(streams, dests, metas, xs)]

    y_prompt = outs[0].reshape(BATCH, SEQ, D_MODEL)
    y_sample = outs[1].reshape(DEC_BATCH, DEC_SEQ, D_MODEL)
    return (y_prompt, y_sample, new_k_a, new_v_a, new_k_b, new_v_b)
```

```python
import functools
import math

import jax
import jax.numpy as jnp
from jax import lax
from jax.experimental import pallas as pl
from jax.experimental.pallas import tpu as pltpu

F32 = jnp.float32
BF16 = jnp.bfloat16
HIGHEST = lax.Precision.HIGHEST

D_MODEL = 1024
BATCH, SEQ = 32, 256
DEC_BATCH, DEC_SEQ = 4, 4096
PAST_LEN = 256
DEPTH = 2
GRID_W = 64
GRID_H = DEC_SEQ // GRID_W
N_HEADS_A, N_KV_A, HEAD_DIM_A = 8, 2, 128
ROT_FREQS = HEAD_DIM_A // 4
ROPE_THETA = 10000.0
N_HEADS_B, HEAD_DIM_B = 16, 64
WIN_H, WIN_W = 8, 16
D_FF = 2816
N_EXPERTS = 8
EPS = 1e-6
NEG_INF = -1e30
ALPHA = (2.0 * DEPTH) ** 0.25
LOG2_E = math.log2(math.e)

N_CTX = BATCH * SEQ
N_LAT = DEC_BATCH * DEC_SEQ
N_GROUPS = 1 + DEC_BATCH
GROUP_PAD = 8

LANES = 128
SUBLANES = 8
VMEM_LIMIT = 56 * 2**20

TM = 512
MXU_DIM = 256
FF_CHUNK = 4 * MXU_DIM
FF_CHUNKS = tuple((lo, min(lo + FF_CHUNK, D_FF)) for lo in range(0, D_FF, FF_CHUNK))
assert D_FF % MXU_DIM == 0
TM_FFN = 512
TMG = 512
TQ_A = 512
UNIT_A = 2
TN_ADA = 1536
N_PAIRS = 2 * (N_CTX + N_LAT)
NT_MOE = N_PAIRS // TMG + N_EXPERTS
N_PAD_ROWS = NT_MOE * TMG - N_PAIRS
TOKEN_TILE_ROWS = D_MODEL // LANES
assert TOKEN_TILE_ROWS == SUBLANES


class Stream:
    def __init__(self, n_rows, group_offset, rows_per_group):
        self.n = n_rows
        self.goff = group_offset
        self.rpg = rows_per_group

    def group_map(self, tm):
        tiles_per_group = self.rpg // tm
        goff = self.goff
        return lambda i, *_: (goff + i // tiles_per_group, 0, 0)


CTX = Stream(N_CTX, 0, N_CTX)
LAT = Stream(N_LAT, 1, DEC_SEQ)


def _params(*sem):
    return pltpu.CompilerParams(dimension_semantics=sem, vmem_limit_bytes=VMEM_LIMIT)


def _mod_spec(stream, tm):
    return pl.BlockSpec((None, 1, D_MODEL), stream.group_map(tm))


def _row_spec(tm, width):
    return pl.BlockSpec((tm, width), lambda i, *_: (i, 0))


def _full_spec(shape):
    nd = len(shape)
    return pl.BlockSpec(shape, lambda *_: (0,) * nd)


def _layer_norm(y, g, b):
    mu = jnp.mean(y, axis=-1, keepdims=True)
    d = y - mu
    var = jnp.mean(d * d, axis=-1, keepdims=True)
    return d * lax.rsqrt(var + EPS) * g + b


def _ada_kernel(c_ref, w_ref, b_ref, o_ref):
    c = c_ref[...]
    s = c * jax.nn.sigmoid(c)
    o_ref[...] = jnp.dot(s, w_ref[...], preferred_element_type=F32, precision=HIGHEST) + b_ref[...]


def _ada_mods(cond, ada_w, ada_b):
    n_out = 6 * D_MODEL
    out = pl.pallas_call(
        _ada_kernel,
        grid=(DEPTH, n_out // TN_ADA),
        in_specs=[
            pl.BlockSpec((GROUP_PAD, D_MODEL), lambda l, n: (0, 0)),
            pl.BlockSpec((None, D_MODEL, TN_ADA), lambda l, n: (l, 0, n)),
            pl.BlockSpec((None, 1, TN_ADA), lambda l, n: (l, 0, n)),
        ],
        out_specs=pl.BlockSpec((None, GROUP_PAD, TN_ADA), lambda l, n: (l, 0, n)),
        out_shape=jax.ShapeDtypeStruct((DEPTH, GROUP_PAD, n_out), F32),
        compiler_params=_params("arbitrary", "arbitrary"),
        name="ada_mods",
    )(cond, ada_w, ada_b.reshape(DEPTH, 1, n_out))
    out = out.reshape(DEPTH, GROUP_PAD, 6, D_MODEL).transpose(0, 2, 1, 3)
    return out[:, :, :, None, :]


def _swap_halves(t):
    lane = lax.broadcasted_iota(jnp.int32, t.shape, 1)
    fwd = pltpu.roll(t, LANES - ROT_FREQS, 1)
    bwd = pltpu.roll(t, ROT_FREQS, 1)
    return jnp.where((lane % (2 * ROT_FREQS)) < ROT_FREQS, fwd, bwd)


def _qkv_kernel(*refs, nq, nk, norm, rope, emit_f32, v_layout, qscale):
    refs = list(refs)
    x_ref, sh_ref, sc_ref, w_ref = refs[:4]
    pos = 4
    if norm:
        qg_ref, kg_ref = refs[pos:pos + 2]
        pos += 2
    if rope:
        cos_ref, sin_ref = refs[pos:pos + 2]
        pos += 2
    q_ref, k_ref, v_ref = refs[pos:pos + 3]
    pos += 3
    if emit_f32:
        kf_ref, vf_ref = refs[pos:pos + 2]

    h = (x_ref[...] * (1.0 + sc_ref[...]) + sh_ref[...]).astype(BF16)
    qkv = jnp.dot(h, w_ref[...], preferred_element_type=F32)
    if norm:
        n_heads = (nq + nk) // HEAD_DIM_A
        for hd in range(n_heads):
            lo = hd * HEAD_DIM_A
            t = qkv[:, lo:lo + HEAD_DIM_A]
            ms = jnp.mean(t * t, axis=-1, keepdims=True)
            gain = qg_ref[...] if lo < nq else kg_ref[...]
            t = t * lax.rsqrt(ms + EPS) * gain
            if rope:
                t = t * cos_ref[...] + _swap_halves(t) * sin_ref[...]
            if lo < nq:
                q_ref[:, lo:lo + HEAD_DIM_A] = (t * qscale).astype(BF16)
            else:
                k_ref[:, lo - nq:lo - nq + HEAD_DIM_A] = t.astype(BF16)
                if emit_f32:
                    kf_ref[:, lo - nq:lo - nq + HEAD_DIM_A] = t
    else:
        q_ref[...] = (qkv[:, :nq] * qscale).astype(BF16)
        k = qkv[:, nq:nq + nk]
        k_ref[...] = k.astype(BF16)
        if emit_f32:
            kf_ref[...] = k
    v = qkv[:, nq + nk:]
    if v_layout == "rows":
        v_ref[...] = v.astype(BF16)
    elif v_layout == "t":
        v_ref[...] = v.T.astype(BF16)
    else:
        vt = v.T.astype(BF16)
        for j in range(vt.shape[1] // LANES):
            v_ref[j] = vt[:, j * LANES:(j + 1) * LANES]
    if emit_f32:
        vf_ref[...] = v


def _qkv(stream, x, shift, scale, w, *, nq, nk, qscale, gains=None, rope_tables=None, emit_f32=False,
         v_layout="rows"):
    n = stream.n
    nw = w.shape[1]
    norm = gains is not None
    rope = rope_tables is not None
    in_specs = [_row_spec(TM, D_MODEL), _mod_spec(stream, TM), _mod_spec(stream, TM), _full_spec((D_MODEL, nw))]
    args = [x, shift, scale, w]
    if norm:
        in_specs += [_full_spec((1, HEAD_DIM_A))] * 2
        args += list(gains)
    if rope:
        tiles_per_seq = DEC_SEQ // TM
        tbl_spec = pl.BlockSpec((TM, HEAD_DIM_A), lambda i: (i % tiles_per_seq, 0))
        in_specs += [tbl_spec, tbl_spec]
        args += list(rope_tables)
    out_specs = [_row_spec(TM, nq), _row_spec(TM, nk), _row_spec(TM, nk)]
    out_shape = [jax.ShapeDtypeStruct((n, nq), BF16), jax.ShapeDtypeStruct((n, nk), BF16),
                 jax.ShapeDtypeStruct((n, nk), BF16)]
    if v_layout == "t":
        out_specs[2] = pl.BlockSpec((nk, TM), lambda i: (0, i))
        out_shape[2] = jax.ShapeDtypeStruct((nk, n), BF16)
    elif v_layout == "t_blocked":
        out_specs[2] = pl.BlockSpec((TM // LANES, nk, LANES), lambda i: (i, 0, 0))
        out_shape[2] = jax.ShapeDtypeStruct((n // LANES, nk, LANES), BF16)
    if emit_f32:
        out_specs += [_row_spec(TM, nk), _row_spec(TM, nk)]
        out_shape += [jax.ShapeDtypeStruct((n, nk), F32)] * 2
    return pl.pallas_call(
        functools.partial(_qkv_kernel, nq=nq, nk=nk, norm=norm, rope=rope, emit_f32=emit_f32,
                          v_layout=v_layout, qscale=qscale),
        grid=(n // TM,),
        in_specs=in_specs,
        out_specs=out_specs,
        out_shape=out_shape,
        compiler_params=_params("arbitrary"),
        name="qkv_norm_rope" if norm else "qkv",
    )(*args)


def _attend(q, kvs, biases=None):
    scores = []
    for idx, (k, _) in enumerate(kvs):
        s = lax.dot_general(q, k, (((1,), (1,)), ((), ())), preferred_element_type=F32)
        if biases is not None and biases[idx] is not None:
            s = s + biases[idx]
        scores.append(s)
    m = jnp.max(scores[0], axis=-1, keepdims=True)
    for s in scores[1:]:
        m = jnp.maximum(m, jnp.max(s, axis=-1, keepdims=True))
    denom = None
    out = None
    for s, (_, v) in zip(scores, kvs):
        p = jnp.exp(s - m)
        part = jnp.sum(p, axis=-1, keepdims=True)
        pv = jnp.dot(p.astype(BF16), v, preferred_element_type=F32)
        denom = part if denom is None else denom + part
        out = pv if out is None else out + pv
    return out * (1.0 / denom)


def _gqa_ctx_kernel(q_ref, k_ref, v_ref, o_ref):
    d = HEAD_DIM_A
    group = N_HEADS_A // N_KV_A
    for g in range(N_KV_A):
        heads = [g * group + j for j in range(group)]
        qs = jnp.concatenate([q_ref[:, h * d:(h + 1) * d] for h in heads], axis=0)
        o = _attend(qs, [(k_ref[:, g * d:(g + 1) * d], v_ref[:, g * d:(g + 1) * d])])
        for j, h in enumerate(heads):
            o_ref[:, h * d:(h + 1) * d] = o[j * SEQ:(j + 1) * SEQ].astype(o_ref.dtype)


def _gqa_ctx(q, k, v):
    nq, nk = N_HEADS_A * HEAD_DIM_A, N_KV_A * HEAD_DIM_A
    return pl.pallas_call(
        _gqa_ctx_kernel,
        grid=(BATCH,),
        in_specs=[_row_spec(SEQ, nq), _row_spec(SEQ, nk), _row_spec(SEQ, nk)],
        out_specs=_row_spec(SEQ, nq),
        out_shape=jax.ShapeDtypeStruct((N_CTX, nq), BF16),
        compiler_params=_params("arbitrary"),
        name="gqa_ctx",
    )(q, k, v)


ONES_ROWS = 16


def _scores_t(q, ks):
    return [lax.dot_general(k, q, (((1,), (1,)), ((), ())), preferred_element_type=F32) for k in ks]


def _softmax_pv_t(scores, vts):
    m = jnp.max(scores[0], axis=0, keepdims=True)
    for s in scores[1:]:
        m = jnp.maximum(m, jnp.max(s, axis=0, keepdims=True))
    acc = None
    for s, vt in zip(scores, vts):
        p = jnp.exp2(s - m).astype(BF16)
        vt_ones = jnp.concatenate([vt, jnp.ones((ONES_ROWS, vt.shape[1]), BF16)], axis=0)
        pv = jnp.dot(vt_ones, p, preferred_element_type=F32)
        acc = pv if acc is None else acc + pv
    d = LANES
    return (acc[:d] * (1.0 / acc[d:d + 1])).T


def _gqa_t_kernel(q_ref, k_ref, vt_ref, kc_ref, vct_ref, o_ref, *, n_kv, group, unit, tq):
    d = HEAD_DIM_A
    units = [(g, [g * group + u0 + j for j in range(unit)]) for g in range(n_kv) for u0 in range(0, group, unit)]
    def unit_scores(g, heads):
        ds = slice(g * d, (g + 1) * d)
        qs = jnp.concatenate([q_ref[:, h * d:(h + 1) * d] for h in heads], axis=0)
        return _scores_t(qs, [k_ref[:, ds], kc_ref[:, ds]])

    nxt = unit_scores(*units[0])
    for idx, (g, heads) in enumerate(units):
        sc = nxt
        if idx + 1 < len(units):
            nxt = unit_scores(*units[idx + 1])
        ds = slice(g * d, (g + 1) * d)
        o = _softmax_pv_t(sc, [vt_ref[ds, :], vct_ref[ds, :]])
        for j, h in enumerate(heads):
            o_ref[:, h * d:(h + 1) * d] = o[j * tq:(j + 1) * tq].astype(o_ref.dtype)


def _gqa_lat_t(q, k, vt, kc, vct):
    nq, nk = N_HEADS_A * HEAD_DIM_A, N_KV_A * HEAD_DIM_A
    tiles = DEC_SEQ // TQ_A
    q_spec = pl.BlockSpec((TQ_A, nq), lambda b, t: (b * tiles + t, 0))
    return pl.pallas_call(
        functools.partial(_gqa_t_kernel, n_kv=N_KV_A, group=N_HEADS_A // N_KV_A, unit=UNIT_A, tq=TQ_A),
        grid=(DEC_BATCH, tiles),
        in_specs=[q_spec,
                  pl.BlockSpec((DEC_SEQ, nk), lambda b, t: (b, 0)),
                  pl.BlockSpec((nk, DEC_SEQ), lambda b, t: (0, b)),
                  pl.BlockSpec((None, PAST_LEN, nk), lambda b, t: (b, 0, 0)),
                  pl.BlockSpec((None, nk, PAST_LEN), lambda b, t: (b, 0, 0))],
        out_specs=q_spec,
        out_shape=jax.ShapeDtypeStruct((N_LAT, nq), BF16),
        compiler_params=_params("arbitrary", "arbitrary"),
        name="gqa_lat",
    )(q, k, vt, kc, vct)


def _split_pair(qp):
    lo = lax.broadcasted_iota(jnp.int32, qp.shape, 1) < HEAD_DIM_B
    zero = jnp.zeros_like(qp)
    return jnp.concatenate([jnp.where(lo, qp, zero), jnp.where(lo, zero, qp)], axis=0)


def _merge_pair(o2, rows):
    lo = lax.broadcasted_iota(jnp.int32, (rows, LANES), 1) < HEAD_DIM_B
    return jnp.where(lo, o2[:rows], o2[rows:])


def _mha_ctx_kernel(q_ref, k_ref, v_ref, o_ref):
    for p in range(N_HEADS_B // 2):
        sl = slice(p * LANES, (p + 1) * LANES)
        o2 = _attend(_split_pair(q_ref[:, sl]), [(k_ref[:, sl], v_ref[:, sl])])
        o_ref[:, sl] = _merge_pair(o2, SEQ).astype(o_ref.dtype)


def _mha_ctx(q, k, v):
    return pl.pallas_call(
        _mha_ctx_kernel,
        grid=(BATCH,),
        in_specs=[_row_spec(SEQ, D_MODEL)] * 3,
        out_specs=_row_spec(SEQ, D_MODEL),
        out_shape=jax.ShapeDtypeStruct((N_CTX, D_MODEL), BF16),
        compiler_params=_params("arbitrary"),
        name="mha_ctx",
    )(q, k, v)


NAT_HALF = D_MODEL
NAT_PAIRS = NAT_HALF // LANES
NAT_R = 4
NAT_WROWS = 12
N_DROW = 2 * WIN_H - 1
MASKED_TILE = N_DROW


def _natten_kernel(q_ref, k_ref, vt_ref, kc_ref, vct_ref, tbl_ref, o_ref):
    r0 = pl.program_id(2) * NAT_R
    ws = jnp.clip(r0 - WIN_H // 2, 0, GRID_H - NAT_WROWS)
    row0 = pl.multiple_of(ws * GRID_W, LANES)
    blk0 = ws * GRID_W // LANES
    n_loc = NAT_WROWS * GRID_W

    def tile_index(a, w):
        r = r0 + a
        start = jnp.clip(r - WIN_H // 2, 0, GRID_H - WIN_H)
        wr = ws + w
        valid = (wr >= start) & (wr < start + WIN_H)
        return jnp.where(valid, wr - r + (WIN_H - 1), MASKED_TILE)

    idx = [[tile_index(a, w) for w in range(NAT_WROWS)] for a in range(NAT_R)]

    def pair_scores(p):
        sl = slice(p * LANES, (p + 1) * LANES)
        q2 = jnp.concatenate([_split_pair(q_ref[a * GRID_W:(a + 1) * GRID_W, sl]) for a in range(NAT_R)], axis=0)
        bias_t = jnp.concatenate(
            [jnp.concatenate([tbl_ref[idx[a][w], p] for w in range(NAT_WROWS)], axis=0) for a in range(NAT_R)],
            axis=1)
        s_loc, s_ctx = _scores_t(q2, [k_ref[pl.ds(row0, n_loc), sl], kc_ref[:, sl]])
        return [s_loc + bias_t, s_ctx]

    nxt = pair_scores(0)
    for p in range(NAT_PAIRS):
        sl = slice(p * LANES, (p + 1) * LANES)
        sc = nxt
        if p + 1 < NAT_PAIRS:
            nxt = pair_scores(p + 1)
        vt_win = jnp.concatenate([vt_ref[blk0 + j, sl, :] for j in range(n_loc // LANES)], axis=1)
        o2 = _softmax_pv_t(sc, [vt_win, vct_ref[sl, :]])
        for a in range(NAT_R):
            o_ref[a * GRID_W:(a + 1) * GRID_W, sl] = _merge_pair(
                o2[a * LANES:(a + 1) * LANES], GRID_W).astype(o_ref.dtype)


def _natten(q, k, vt, kc, vct, tbl):
    n_half = D_MODEL // NAT_HALF
    blocks = GRID_H // NAT_R
    seq_blocks = DEC_SEQ // LANES
    q_spec = pl.BlockSpec((NAT_R * GRID_W, NAT_HALF), lambda b, hh, r: (b * blocks + r, hh))
    return pl.pallas_call(
        _natten_kernel,
        grid=(DEC_BATCH, n_half, blocks),
        in_specs=[q_spec,
                  pl.BlockSpec((DEC_SEQ, NAT_HALF), lambda b, hh, r: (b, hh)),
                  pl.BlockSpec((seq_blocks, NAT_HALF, LANES), lambda b, hh, r: (b, hh, 0)),
                  pl.BlockSpec((None, PAST_LEN, NAT_HALF), lambda b, hh, r: (b, 0, hh)),
                  pl.BlockSpec((None, NAT_HALF, PAST_LEN), lambda b, hh, r: (b, hh, 0)),
                  pl.BlockSpec((N_DROW + 1, NAT_PAIRS, GRID_W, LANES), lambda b, hh, r: (0, hh, 0, 0))],
        out_specs=q_spec,
        out_shape=jax.ShapeDtypeStruct((N_LAT, D_MODEL), BF16),
        compiler_params=_params("arbitrary", "arbitrary", "arbitrary"),
        name="natten",
    )(q, k, vt, kc, vct, tbl)


def _bias_table_kernel(rpb_ref, o_ref):
    dr = pl.program_id(0)
    row = lax.broadcasted_iota(jnp.int32, (GRID_W, LANES), 0)
    col = lax.broadcasted_iota(jnp.int32, (GRID_W, LANES), 1)
    sub = col >= GRID_W
    qc = col % GRID_W
    kc = row
    dc = jnp.clip(kc - qc, -(WIN_W - 1), WIN_W - 1) + (WIN_W - 1)
    col_start = jnp.clip(qc - WIN_W // 2, 0, GRID_W - WIN_W)
    in_win = (kc >= col_start) & (kc < col_start + WIN_W) & (dr < N_DROW)
    n_dc = 2 * WIN_W - 1
    drc = jnp.minimum(dr, N_DROW - 1)
    for pr in range(N_HEADS_B // 2):
        b0 = ((2 * pr) * N_DROW + drc) * n_dc
        b1 = ((2 * pr + 1) * N_DROW + drc) * n_dc

        def body(j, acc, b0=b0, b1=b1):
            return jnp.where(dc == j, jnp.where(sub, rpb_ref[b1 + j], rpb_ref[b0 + j]), acc)

        acc = lax.fori_loop(0, n_dc, body, jnp.zeros((GRID_W, LANES), F32))
        o_ref[pr] = jnp.where(in_win, acc * LOG2_E, NEG_INF)


def _bias_table(rpb):
    return pl.pallas_call(
        _bias_table_kernel,
        grid=(N_DROW + 1,),
        in_specs=[pl.BlockSpec(memory_space=pltpu.SMEM)],
        out_specs=pl.BlockSpec((None, N_HEADS_B // 2, GRID_W, LANES), lambda d: (d, 0, 0, 0)),
        out_shape=jax.ShapeDtypeStruct((N_DROW + 1, N_HEADS_B // 2, GRID_W, LANES), F32),
        compiler_params=_params("arbitrary"),
        name="natten_bias_table",
    )(rpb.reshape(-1))


def _attn_residual(a_ref, wo_ref, x_ref, gate_ref, g_ref, b_ref):
    o = jnp.dot(a_ref[...], wo_ref[...], preferred_element_type=F32)
    return _layer_norm(ALPHA * x_ref[...] + gate_ref[...] * o, g_ref[...], b_ref[...])


def _attn_residual_specs(stream, tm):
    vec = _full_spec((1, D_MODEL))
    return [_row_spec(tm, D_MODEL), _full_spec((D_MODEL, D_MODEL)), _row_spec(tm, D_MODEL),
            _mod_spec(stream, tm), vec, vec]


def _wo_ln_kernel(a_ref, wo_ref, x_ref, gate_ref, g_ref, b_ref, o_ref):
    o_ref[...] = _attn_residual(a_ref, wo_ref, x_ref, gate_ref, g_ref, b_ref)


def _wo_ln(stream, attn, w_o, x, gate, ln_g, ln_b):
    return pl.pallas_call(
        _wo_ln_kernel,
        grid=(stream.n // TM,),
        in_specs=_attn_residual_specs(stream, TM),
        out_specs=_row_spec(TM, D_MODEL),
        out_shape=jax.ShapeDtypeStruct((stream.n, D_MODEL), F32),
        compiler_params=_params("arbitrary"),
        name="wo_postnorm",
    )(attn, w_o, x, gate, ln_g, ln_b)


def _swiglu(h, wg_ref, wu_ref, wd_ref):
    acc = None
    for lo, hi in FF_CHUNKS:
        g = jnp.dot(h, wg_ref[:, lo:hi], preferred_element_type=F32)
        u = jnp.dot(h, wu_ref[:, lo:hi], preferred_element_type=F32)
        a = (g * jax.nn.sigmoid(g) * u).astype(BF16)
        part = jnp.dot(a, wd_ref[lo:hi, :], preferred_element_type=F32)
        acc = part if acc is None else acc + part
    return acc


def _layer_dense_kernel(a_ref, wo_ref, x_ref, gate1_ref, g1_ref, b1_ref,
                        sh_ref, sc_ref, wg_ref, wu_ref, wd_ref, gate_ref, g_ref, b_ref, o_ref):
    half = TM_FFN // 2
    xs, fs = [], []
    for s in range(2):
        rows = slice(s * half, (s + 1) * half)
        o = jnp.dot(a_ref[rows, :], wo_ref[...], preferred_element_type=F32)
        x = _layer_norm(ALPHA * x_ref[rows, :] + gate1_ref[...] * o, g1_ref[...], b1_ref[...])
        h = (x * (1.0 + sc_ref[...]) + sh_ref[...]).astype(BF16)
        xs.append(x)
        fs.append(_swiglu(h, wg_ref, wu_ref, wd_ref))
    for s in range(2):
        y = ALPHA * xs[s] + gate_ref[...] * fs[s]
        o_ref[s * half:(s + 1) * half, :] = _layer_norm(y, g_ref[...], b_ref[...])


def _layer_dense(stream, attn, w_o, x, gate1, ln1_g, ln1_b, shift, scale, w_gu, w_down, gate, ln_g, ln_b):
    vec = _full_spec((1, D_MODEL))
    mod = _mod_spec(stream, TM_FFN)
    return pl.pallas_call(
        _layer_dense_kernel,
        grid=(stream.n // TM_FFN,),
        in_specs=_attn_residual_specs(stream, TM_FFN) + [
            mod, mod,
            pl.BlockSpec((D_MODEL, D_FF), lambda i: (0, 0)),
            pl.BlockSpec((D_MODEL, D_FF), lambda i: (0, 1)),
            _full_spec((D_FF, D_MODEL)),
            mod, vec, vec],
        out_specs=_row_spec(TM_FFN, D_MODEL),
        out_shape=jax.ShapeDtypeStruct((stream.n, D_MODEL), F32),
        compiler_params=_params("arbitrary"),
        name="wo_ffn_dense",
    )(attn, w_o, x, gate1, ln1_g, ln1_b, shift, scale, w_gu, w_gu, w_down, gate, ln_g, ln_b)


def _ffn_grouped_kernel(te_ref, tv_ref, x_ref, wg_ref, wu_ref, wd_ref, o_ref):
    i = pl.program_id(0)

    @pl.when(tv_ref[i] != 0)
    def _():
        x = _from_token_tiles(x_ref, TMG).astype(BF16)
        _to_token_tiles(o_ref, _swiglu(x, wg_ref, wu_ref, wd_ref))

    @pl.when(tv_ref[i] == 0)
    def _():
        o_ref[...] = jnp.zeros_like(o_ref)


def _ffn_grouped(tile_expert, tile_valid, xs, w_gu, w_down):
    row = pl.BlockSpec((TMG * TOKEN_TILE_ROWS, LANES), lambda i, te, tv: (i, 0))
    grid_spec = pltpu.PrefetchScalarGridSpec(
        num_scalar_prefetch=2,
        grid=(NT_MOE,),
        in_specs=[row,
                  pl.BlockSpec((None, D_MODEL, D_FF), lambda i, te, tv: (te[i], 0, 0)),
                  pl.BlockSpec((None, D_MODEL, D_FF), lambda i, te, tv: (te[i], 0, 1)),
                  pl.BlockSpec((None, D_FF, D_MODEL), lambda i, te, tv: (te[i], 0, 0))],
        out_specs=row,
    )
    return pl.pallas_call(
        _ffn_grouped_kernel,
        grid_spec=grid_spec,
        out_shape=jax.ShapeDtypeStruct((NT_MOE * TMG * TOKEN_TILE_ROWS, LANES), F32),
        compiler_params=_params("arbitrary"),
        name="ffn_grouped",
    )(tile_expert, tile_valid, xs, w_gu, w_gu, w_down)


META_E0, META_E1, META_W0, META_W1, META_R0, META_R1 = range(6)


def _router_kernel(x_ref, sh_ref, sc_ref, rw_ref, cnt_in_ref, h_ref, meta_ref, rec_ref, cnt_ref, carry_scr):
    i = pl.program_id(0)

    @pl.when(i == 0)
    def _():
        carry_scr[...] = cnt_in_ref[...]

    h = x_ref[...] * (1.0 + sc_ref[...]) + sh_ref[...]
    _to_token_tiles(h_ref, h)
    w = rw_ref[...]
    h_hi = h.astype(BF16)
    w_hi = w.astype(BF16)
    h_lo = (h - h_hi.astype(F32)).astype(BF16)
    w_lo = (w - w_hi.astype(F32)).astype(BF16)
    logits = (jnp.dot(h_hi, w_hi, preferred_element_type=F32)
              + (jnp.dot(h_lo, w_hi, preferred_element_type=F32)
                 + jnp.dot(h_hi, w_lo, preferred_element_type=F32)))
    lane = lax.broadcasted_iota(jnp.int32, logits.shape, 1).astype(F32)
    lg = jnp.where(lane < N_EXPERTS, logits, -jnp.inf)
    m1 = jnp.max(lg, axis=-1, keepdims=True)
    i1 = jnp.min(jnp.where(lg == m1, lane, float(LANES)), axis=-1, keepdims=True)
    lg2 = jnp.where(lane == i1, -jnp.inf, lg)
    m2 = jnp.max(lg2, axis=-1, keepdims=True)
    i2 = jnp.min(jnp.where(lg2 == m2, lane, float(LANES)), axis=-1, keepdims=True)
    e = jnp.exp(m2 - m1)
    w1 = 1.0 / (1.0 + e)
    w2 = e / (1.0 + e)

    sel1 = lane == i1
    sel2 = lane == i2
    onehot = jnp.where(sel1 | sel2, 1.0, 0.0)
    rr = lax.broadcasted_iota(jnp.int32, (TM, TM), 0)
    cc = lax.broadcasted_iota(jnp.int32, (TM, TM), 1)
    lower = jnp.where(cc < rr, 1.0, 0.0).astype(BF16)
    before = jnp.dot(lower, onehot.astype(BF16), preferred_element_type=F32) + carry_scr[0:1, :]
    r1 = jnp.sum(jnp.where(sel1, before, 0.0), axis=-1, keepdims=True)
    r2 = jnp.sum(jnp.where(sel2, before, 0.0), axis=-1, keepdims=True)
    carry_scr[...] = carry_scr[...] + jnp.sum(onehot, axis=0, keepdims=True)
    cnt_ref[...] = carry_scr[...]

    cols = [i1, i2, w1, w2, r1, r2]
    meta = jnp.zeros(logits.shape, F32)
    for c, val in enumerate(cols):
        meta = jnp.where(lane == c, val, meta)
    meta_ref[...] = meta
    rec_ref[...] = meta.T[:SUBLANES]


def _router(stream, x, shift, scale, rw_pad, cnt_in):
    cnt_spec = _full_spec((SUBLANES, LANES))
    return pl.pallas_call(
        _router_kernel,
        grid=(stream.n // TM,),
        in_specs=[_row_spec(TM, D_MODEL), _mod_spec(stream, TM), _mod_spec(stream, TM),
                  _full_spec((D_MODEL, LANES)), cnt_spec],
        out_specs=[_row_spec(TM * TOKEN_TILE_ROWS, LANES), _row_spec(TM, LANES),
                   pl.BlockSpec((SUBLANES, TM), lambda i: (0, i)), cnt_spec],
        out_shape=[jax.ShapeDtypeStruct((stream.n * TOKEN_TILE_ROWS, LANES), F32),
                   jax.ShapeDtypeStruct((stream.n, LANES), F32),
                   jax.ShapeDtypeStruct((SUBLANES, stream.n), F32),
                   jax.ShapeDtypeStruct((SUBLANES, LANES), F32)],
        scratch_shapes=[pltpu.VMEM((SUBLANES, LANES), F32)],
        compiler_params=_params("arbitrary"),
        name="moe_router",
    )(x, shift, scale, rw_pad, cnt_in)


def _tile_rows(token, count=1):
    start = token * TOKEN_TILE_ROWS
    if not isinstance(start, int):
        start = pl.multiple_of(start, TOKEN_TILE_ROWS)
    return pl.ds(start, count * TOKEN_TILE_ROWS)


def _to_token_tiles(ref, x):
    for j in range(TOKEN_TILE_ROWS):
        ref[pl.ds(j, x.shape[0], stride=TOKEN_TILE_ROWS), :] = x[:, j * LANES:(j + 1) * LANES]


def _from_token_tiles(ref, tokens):
    return jnp.concatenate([ref[pl.ds(j, tokens, stride=TOKEN_TILE_ROWS), :] for j in range(TOKEN_TILE_ROWS)], axis=1)


def _row_copy(src_ref, src_row, dst_ref, dst_row, sem):
    return pltpu.make_async_copy(src_ref.at[_tile_rows(src_row)], dst_ref.at[_tile_rows(dst_row)], sem)


def _scatter_rows(d0_ref, d1_ref, h_ref, buf_ref, sem):
    base = pl.program_id(0) * TM
    for r in range(TM):
        _row_copy(h_ref, r, buf_ref, d0_ref[base + r], sem).start(priority=0)
        _row_copy(h_ref, r, buf_ref, d1_ref[base + r], sem).start(priority=1)


def _scatter_wait(h_ref, buf_ref, sem, n_rows):
    pltpu.make_async_copy(h_ref.at[_tile_rows(0, n_rows)], buf_ref.at[_tile_rows(0, n_rows)], sem).wait()


def _scatter_kernel(d0_ref, d1_ref, pad_ref, hc_ref, hl_ref, buf_ref, zero_scr, sem, *, ctx_steps, pads_per_step):
    i = pl.program_id(0)

    @pl.when(i < ctx_steps)
    def _():
        _scatter_rows(d0_ref, d1_ref, hc_ref, buf_ref, sem)
        zero_scr[...] = jnp.zeros_like(zero_scr)
        pbase = i * pads_per_step
        for r in range(pads_per_step):
            _row_copy(zero_scr, 0, buf_ref, pad_ref[pbase + r], sem).start(priority=r % 2)
        _scatter_wait(hc_ref, buf_ref, sem, TM)
        _scatter_wait(hc_ref, buf_ref, sem, TM)
        _scatter_wait(hc_ref, buf_ref, sem, pads_per_step)

    @pl.when(i >= ctx_steps)
    def _():
        _scatter_rows(d0_ref, d1_ref, hl_ref, buf_ref, sem)
        _scatter_wait(hl_ref, buf_ref, sem, TM)
        _scatter_wait(hl_ref, buf_ref, sem, TM)


def _scatter(d0, d1, pad_rows, h_ctx, h_lat):
    ctx_steps = CTX.n // TM
    steps = ctx_steps + LAT.n // TM
    assert N_PAD_ROWS % ctx_steps == 0
    grid_spec = pltpu.PrefetchScalarGridSpec(
        num_scalar_prefetch=3,
        grid=(steps,),
        in_specs=[pl.BlockSpec((TM * TOKEN_TILE_ROWS, LANES), lambda i, a, b, c: (jnp.minimum(i, ctx_steps - 1), 0)),
                  pl.BlockSpec((TM * TOKEN_TILE_ROWS, LANES), lambda i, a, b, c: (jnp.maximum(i - ctx_steps, 0), 0))],
        out_specs=pl.BlockSpec(memory_space=pl.ANY),
        scratch_shapes=[pltpu.VMEM((TOKEN_TILE_ROWS, LANES), F32), pltpu.SemaphoreType.DMA],
    )
    return pl.pallas_call(
        functools.partial(_scatter_kernel, ctx_steps=ctx_steps, pads_per_step=N_PAD_ROWS // ctx_steps),
        grid_spec=grid_spec,
        out_shape=jax.ShapeDtypeStruct((NT_MOE * TMG * TOKEN_TILE_ROWS, LANES), F32),
        compiler_params=_params("arbitrary"),
        name="moe_scatter",
    )(d0, d1, pad_rows, h_ctx, h_lat)


def _gather_start(d0_ref, d1_ref, ys_ref, rows_scr, sems, tile, slot):
    base = tile * TM
    for r in range(TM):
        _row_copy(ys_ref, d0_ref[base + r], rows_scr.at[slot, 0], r, sems.at[slot]).start(priority=0)
        _row_copy(ys_ref, d1_ref[base + r], rows_scr.at[slot, 1], r, sems.at[slot]).start(priority=1)


def _gather_wait(ys_ref, rows_scr, sems, slot):
    for s in range(2):
        pltpu.make_async_copy(ys_ref.at[_tile_rows(0, TM)], rows_scr.at[slot, s], sems.at[slot]).wait()


def _combine_kernel(d0_ref, d1_ref, ys_ref, meta_ref, x_ref, gate_ref, g_ref, b_ref, o_ref, rows_scr, sems,
                    *, n_tiles):
    i = pl.program_id(0)
    slot = i % 2

    @pl.when(i == 0)
    def _():
        _gather_start(d0_ref, d1_ref, ys_ref, rows_scr, sems, 0, 0)

    _gather_start(d0_ref, d1_ref, ys_ref, rows_scr, sems, jnp.minimum(i + 1, n_tiles - 1), 1 - slot)
    _gather_wait(ys_ref, rows_scr, sems, slot)

    meta = meta_ref[...]
    w0 = meta[:, META_W0:META_W0 + 1]
    w1 = meta[:, META_W1:META_W1 + 1]
    f = w0 * _from_token_tiles(rows_scr.at[slot, 0], TM) + w1 * _from_token_tiles(rows_scr.at[slot, 1], TM)
    y = ALPHA * x_ref[...] + gate_ref[...] * f
    o_ref[...] = _layer_norm(y, g_ref[...], b_ref[...])

    @pl.when(i == n_tiles - 1)
    def _():
        _gather_wait(ys_ref, rows_scr, sems, 1 - slot)


def _combine(stream, d0, d1, ys, meta, x, gate, ln_g, ln_b):
    vec = pl.BlockSpec((1, D_MODEL), lambda i, a, b: (0, 0))
    grid_spec = pltpu.PrefetchScalarGridSpec(
        num_scalar_prefetch=2,
        grid=(stream.n // TM,),
        in_specs=[pl.BlockSpec(memory_space=pl.ANY), _row_spec(TM, LANES), _row_spec(TM, D_MODEL),
                  _mod_spec(stream, TM), vec, vec],
        out_specs=_row_spec(TM, D_MODEL),
        scratch_shapes=[pltpu.VMEM((2, 2, TM * TOKEN_TILE_ROWS, LANES), F32), pltpu.SemaphoreType.DMA((2,))],
    )
    return pl.pallas_call(
        functools.partial(_combine_kernel, n_tiles=stream.n // TM),
        grid_spec=grid_spec,
        out_shape=jax.ShapeDtypeStruct((stream.n, D_MODEL), F32),
        compiler_params=_params("arbitrary"),
        name="moe_combine",
    )(d0, d1, ys, meta, x, gate, ln_g, ln_b)


def _rope_tables():
    t = jnp.arange(DEC_SEQ)
    row = (t // GRID_W).astype(F32)
    col = (t % GRID_W).astype(F32)
    freqs = ROPE_THETA ** (-jnp.arange(ROT_FREQS, dtype=F32) / ROT_FREQS)
    ar = row[:, None] * freqs
    ac = col[:, None] * freqs
    cos = jnp.concatenate([jnp.cos(ar), jnp.cos(ar), jnp.cos(ac), jnp.cos(ac)], axis=1)
    sin = jnp.concatenate([-jnp.sin(ar), jnp.sin(ar), -jnp.sin(ac), jnp.sin(ac)], axis=1)
    return cos, sin


def _routing_plan(recs, counts):
    cnt = counts[0, :N_EXPERTS].astype(jnp.int32)
    tiles_e = (cnt + TMG - 1) // TMG
    tile_end = jnp.cumsum(tiles_e)
    tile_start = tile_end - tiles_e
    offs = tile_start * TMG
    dests = []
    for rec in recs:
        pair = []
        for ecol, rcol in ((META_E0, META_R0), (META_E1, META_R1)):
            e = rec[ecol].astype(jnp.int32)
            d = rec[rcol].astype(jnp.int32)
            for k in range(N_EXPERTS):
                d = d + jnp.where(e == k, offs[k], 0)
            pair.append(d)
        dests.append(tuple(pair))
    tid = jnp.arange(NT_MOE, dtype=jnp.int32)
    te = jnp.minimum(jnp.sum((tid[:, None] >= tile_end[None, :]).astype(jnp.int32), axis=1), N_EXPERTS - 1)
    total = tile_end[-1]
    valid = tid < total
    te_last = jnp.max(jnp.where(valid, te, 0))
    te = jnp.where(valid, te, te_last)
    pad_cnt = tiles_e * TMG - cnt
    pad_end = jnp.cumsum(pad_cnt)
    pad_start = pad_end - pad_cnt
    k = jnp.arange(N_PAD_ROWS, dtype=jnp.int32)
    grp = jnp.sum((k[:, None] >= pad_end[None, :]).astype(jnp.int32), axis=1)
    pad_rows = total * TMG + (k - pad_end[-1])
    for e in range(N_EXPERTS):
        pad_rows = jnp.where(grp == e, offs[e] + cnt[e] + (k - pad_start[e]), pad_rows)
    return dests, te, valid.astype(jnp.int32), pad_rows


def kernel(x_prompt, x_sample, cache_k_a, cache_v_a, cache_k_b, cache_v_b, c, c_ctx, ada_w, ada_b, ln_attn_g, ln_attn_b, ln_ffn_g, ln_ffn_b, wqkv_a, qnorm_a, knorm_a, wo_a, wqkv_b, rpb_b, wo_b, ffn_w_gu, ffn_w_down, router_w, moe_w_gu, moe_w_down):
    streams = (CTX, LAT)
    xs = [x_prompt.reshape(N_CTX, D_MODEL), x_sample.reshape(N_LAT, D_MODEL)]

    cond = jnp.zeros((GROUP_PAD, D_MODEL), F32).at[0].set(c_ctx).at[1:N_GROUPS].set(c)
    mods = _ada_mods(cond, ada_w, ada_b)
    vec = lambda a, l: a[l].reshape(1, D_MODEL)

    m = mods[0]
    w_qkv = wqkv_a[0].astype(BF16)
    w_o = wo_a[0].astype(BF16)
    w_gu = ffn_w_gu[0].astype(BF16)
    w_dn = ffn_w_down[0].astype(BF16)
    gains = (qnorm_a[0].reshape(1, HEAD_DIM_A), knorm_a[0].reshape(1, HEAD_DIM_A))
    nq, nk = N_HEADS_A * HEAD_DIM_A, N_KV_A * HEAD_DIM_A
    qscale = HEAD_DIM_A ** -0.5

    qp, kp, vp, kp32, vp32 = _qkv(CTX, xs[0], m[0], m[1], w_qkv, nq=nq, nk=nk, qscale=qscale, gains=gains,
                                  emit_f32=True)
    new_k_a = kp32.reshape(BATCH, 1, SEQ, N_KV_A, HEAD_DIM_A)
    new_v_a = vp32.reshape(BATCH, 1, SEQ, N_KV_A, HEAD_DIM_A)
    ql, kl, vlt = _qkv(LAT, xs[1], m[0], m[1], w_qkv, nq=nq, nk=nk, qscale=qscale * LOG2_E, gains=gains,
                       rope_tables=_rope_tables(), v_layout="t")
    cache_k = cache_k_a[:, 0].reshape(DEC_BATCH, PAST_LEN, nk).astype(BF16)
    cache_vt = jnp.swapaxes(cache_v_a[:, 0].reshape(DEC_BATCH, PAST_LEN, nk), 1, 2).astype(BF16)
    attn = [_gqa_ctx(qp, kp, vp), _gqa_lat_t(ql, kl, vlt, cache_k, cache_vt)]
    xs = [_layer_dense(s, a, w_o, x, m[2], vec(ln_attn_g, 0), vec(ln_attn_b, 0),
                       m[3], m[4], w_gu, w_dn, m[5], vec(ln_ffn_g, 0), vec(ln_ffn_b, 0))
          for s, a, x in zip(streams, attn, xs)]

    m = mods[1]
    w_qkv = wqkv_b[0].astype(BF16)
    w_o = wo_b[0].astype(BF16)
    qscale = HEAD_DIM_B ** -0.5
    qp, kp, vp, kp32, vp32 = _qkv(CTX, xs[0], m[0], m[1], w_qkv, nq=D_MODEL, nk=D_MODEL, qscale=qscale,
                                  emit_f32=True)
    new_k_b = kp32.reshape(BATCH, 1, SEQ, N_HEADS_B, HEAD_DIM_B)
    new_v_b = vp32.reshape(BATCH, 1, SEQ, N_HEADS_B, HEAD_DIM_B)
    ql, kl, vlt = _qkv(LAT, xs[1], m[0], m[1], w_qkv, nq=D_MODEL, nk=D_MODEL, qscale=qscale * LOG2_E,
                       v_layout="t_blocked")
    cache_k = cache_k_b[:, 0].reshape(DEC_BATCH, PAST_LEN, D_MODEL).astype(BF16)
    cache_vt = jnp.swapaxes(cache_v_b[:, 0].reshape(DEC_BATCH, PAST_LEN, D_MODEL), 1, 2).astype(BF16)
    attn = [_mha_ctx(qp, kp, vp), _natten(ql, kl, vlt, cache_k, cache_vt, _bias_table(rpb_b[0]))]
    xs = [_wo_ln(s, a, w_o, x, m[2], vec(ln_attn_g, 1), vec(ln_attn_b, 1)) for s, a, x in zip(streams, attn, xs)]

    rw_pad = jnp.zeros((D_MODEL, LANES), F32).at[:, :N_EXPERTS].set(router_w[0])
    counts = jnp.zeros((SUBLANES, LANES), F32)
    hs, metas, recs = [], [], []
    for s, x in zip(streams, xs):
        h, meta, rec, counts = _router(s, x, m[3], m[4], rw_pad, counts)
        hs.append(h)
        metas.append(meta)
        recs.append(rec)
    dests, tile_expert, tile_valid, pad_rows = _routing_plan(recs, counts)
    d0_all = jnp.concatenate([dests[0][0], dests[1][0]])
    d1_all = jnp.concatenate([dests[0][1], dests[1][1]])
    sorted_rows = _scatter(d0_all, d1_all, pad_rows, hs[0], hs[1])
    ys = _ffn_grouped(tile_expert, tile_valid, sorted_rows, moe_w_gu[0].astype(BF16), moe_w_down[0].astype(BF16))
    outs = [_combine(s, d0, d1, ys, meta, x, m[5], vec(ln_ffn_g, 1), vec(ln_ffn_b, 1))
            for s, (d0, d1), meta, x in zip(streams, dests, metas, xs)]

    y_prompt = outs[0].reshape(BATCH, SEQ, D_MODEL)
    y_sample = outs[1].reshape(DEC_BATCH, DEC_SEQ, D_MODEL)
    return (y_prompt, y_sample, new_k_a, new_v_a, new_k_b, new_v_b)
```
